```python
import jax, jax.numpy as jnp
from jax import lax
import numpy as np


D_MODEL = 2048
BATCH = 1
SEQ = 8192
DEPTH = 1

GRID_W = 64
CTX_LEN = 256
D_MIX = D_MODEL
ATT_WIDTH = D_MIX // 2
HEAD_DIM = 128
N_Q_HEADS = ATT_WIDTH // HEAD_DIM
N_KV_HEADS = N_Q_HEADS // 4
Q_PER_KV = N_Q_HEADS // N_KV_HEADS
AXIS_DIM = HEAD_DIM // 2
ROPE_THETA = 10000.0
Q_BLOCK = 128
M_WIDTH = D_MIX - ATT_WIDTH
M_HEADS = 4
M_V = M_WIDTH // M_HEADS
M_QK = M_V // 2
M_CHUNK = 64
N_DIR = 2
D_FF = 5632
CONV_W = 3
NORM_EPS = 1e-6

OFF_AQ = 0
OFF_AK = OFF_AQ + N_Q_HEADS * HEAD_DIM
OFF_AV = OFF_AK + N_KV_HEADS * HEAD_DIM
OFF_MQ = OFF_AV + N_KV_HEADS * HEAD_DIM
OFF_MK = OFF_MQ + M_HEADS * M_QK
OFF_MV = OFF_MK + M_HEADS * M_QK
OFF_MO = OFF_MV + M_WIDTH
OFF_G = OFF_MO + M_WIDTH
IN_PROJ = OFF_G + N_DIR * 2 * M_HEADS

kernel_name = 'hybrid_mlstm_gqa_convffn_dit_layer'


def rmsnorm(x, g):
    xf = x.astype(jnp.float32)
    xf = xf * lax.rsqrt(jnp.mean(xf * xf, axis=-1, keepdims=True) + NORM_EPS)
    return xf.astype(x.dtype) * g


def modulate(h, shift, scale):
    return h * (1.0 + scale) + shift


def rope_rotate(x, cos, sin):
    r = x.shape[-1] // 2
    x1, x2 = x[..., :r], x[..., r:]
    c = cos[None, :, None, :]
    s = sin[None, :, None, :]
    return jnp.concatenate([x1 * c - x2 * s, x2 * c + x1 * s], axis=-1)


def axial_rope(x, cos_r, sin_r, cos_c, sin_c):
    return jnp.concatenate([rope_rotate(x[..., :AXIS_DIM], cos_r, sin_r),
                            rope_rotate(x[..., AXIS_DIM:], cos_c, sin_c)], axis=-1)


def attn_qkv(p, q_gain, k_gain):
    B, T, _ = p.shape
    q = p[..., OFF_AQ:OFF_AK].reshape(B, T, N_Q_HEADS, HEAD_DIM)
    k = p[..., OFF_AK:OFF_AV].reshape(B, T, N_KV_HEADS, HEAD_DIM)
    v = p[..., OFF_AV:OFF_MQ].reshape(B, T, N_KV_HEADS, HEAD_DIM)
    return rmsnorm(q, q_gain), rmsnorm(k, k_gain), v


def block_attention(q, k, v):
    B, T = q.shape[:2]
    nb = T // Q_BLOCK
    qb = q.reshape(B, nb, Q_BLOCK, N_KV_HEADS, Q_PER_KV, HEAD_DIM).swapaxes(0, 1)
    scale = HEAD_DIM ** -0.5

    def one_block(qblk):
        s = jnp.einsum('bqkgd,bskd->bkgqs', qblk, k).astype(jnp.float32) * scale
        p = jax.nn.softmax(s, axis=-1).astype(v.dtype)
        return jnp.einsum('bkgqs,bskd->bqkgd', p, v)

    o = lax.map(one_block, qb)
    return o.swapaxes(0, 1).reshape(B, T, N_Q_HEADS * HEAD_DIM)


def mlstm_inputs(p, b_i, b_f):
    B, T, _ = p.shape
    f32 = jnp.float32
    q = p[..., OFF_MQ:OFF_MK].reshape(B, T, M_HEADS, M_QK)
    k = p[..., OFF_MK:OFF_MV].reshape(B, T, M_HEADS, M_QK) * (M_QK ** -0.5)
    v = p[..., OFF_MV:OFF_MO].reshape(B, T, M_HEADS, M_V)
    o = jax.nn.sigmoid(p[..., OFF_MO:OFF_G])
    g = p[..., OFF_G:IN_PROJ].astype(f32).reshape(B, T, N_DIR, 2, M_HEADS)
    i_pre = g[:, :, :, 0] + b_i.astype(f32)
    logf = jax.nn.log_sigmoid(g[:, :, :, 1] + b_f.astype(f32))
    return q, k, v, o, i_pre, logf


def zero_state(batch):
    f32 = jnp.float32
    return (jnp.zeros((batch, M_HEADS, M_QK, M_V), f32),
            jnp.zeros((batch, M_HEADS, M_QK), f32),
            jnp.zeros((batch, M_HEADS), f32))


def mlstm_scan(q, k, v, i_pre, logf, state):
    B, T, H, _ = q.shape
    nc = T // M_CHUNK

    def chunks(a):
        a = a.astype(jnp.float32).reshape((B, nc, M_CHUNK) + a.shape[2:])
        return a.transpose((1, 0, 3, 2) + tuple(range(4, a.ndim)))

    tril = jnp.tril(jnp.ones((M_CHUNK, M_CHUNK), dtype=bool))

    def step(carry, inp):
        C, n, m = carry
        qc, kc, vc, ic, fc = inp
        b = jnp.cumsum(fc, axis=-1)
        log_d = jnp.where(tril, b[..., :, None] - b[..., None, :] + ic[..., None, :], -jnp.inf)
        log_inter = b + m[..., None]
        m_t = jnp.maximum(jnp.max(log_d, axis=-1), log_inter)
        d = jnp.exp(log_d - m_t[..., None])
        w_inter = jnp.exp(log_inter - m_t)
        s = jnp.einsum('bhld,bhsd->bhls', qc, kc) * d
        num = w_inter[..., None] * jnp.einsum('bhld,bhde->bhle', qc, C) + jnp.einsum('bhls,bhse->bhle', s, vc)
        den = w_inter * jnp.einsum('bhld,bhd->bhl', qc, n) + jnp.sum(s, axis=-1)
        h = num / jnp.maximum(jnp.abs(den), jnp.exp(-m_t))[..., None]
        b_last = b[..., -1:]
        log_w = b_last - b + ic
        m_new = jnp.maximum(b_last[..., 0] + m, jnp.max(log_w, axis=-1))
        w_prev = jnp.exp(b_last[..., 0] + m - m_new)
        kw = kc * jnp.exp(log_w - m_new[..., None])[..., None]
        C_new = w_prev[..., None, None] * C + jnp.einsum('bhsd,bhse->bhde', kw, vc)
        n_new = w_prev[..., None] * n + jnp.sum(kw, axis=2)
        return (C_new, n_new, m_new), h

    new_state, hs = lax.scan(step, state, (chunks(q), chunks(k), chunks(v), chunks(i_pre), chunks(logf)))
    h = hs.transpose(1, 0, 3, 2, 4).reshape(B, T, H, v.shape[-1])
    return h.astype(v.dtype), new_state


def mlstm_bidir(q, k, v, i_pre, logf, state_f, state_b):
    hf, sf = mlstm_scan(q, k, v, i_pre[:, :, 0], logf[:, :, 0], state_f)
    rev = lambda a: jnp.flip(a, axis=1)
    hb, sb = mlstm_scan(rev(q), rev(k), rev(v), rev(i_pre[:, :, 1]), rev(logf[:, :, 1]), state_b)
    return hf + rev(hb), sf, sb


def mlstm_readout(h, o, gain):
    B, T = h.shape[:2]
    h = rmsnorm(h, gain.reshape(M_HEADS, M_V))
    return h.reshape(B, T, M_WIDTH) * o


def conv_ffn(h, w_up, conv_w, conv_b, w_down):
    u = h @ w_up
    up = jnp.pad(u, ((0, 0), (1, 1), (0, 0)))
    u = conv_w[0] * up[:, :-2] + conv_w[1] * up[:, 1:-1] + conv_w[2] * up[:, 2:] + conv_b
    g, val = jnp.split(u, 2, axis=-1)
    return (jax.nn.silu(g) * val) @ w_down


def setup_inputs(seed: int = 0) -> dict:
    key = jax.random.key(seed)
    ks = jax.random.split(key, 20)
    f32 = jnp.float32
    nrm = lambda k, shape, s: jax.random.normal(k, shape, f32) * s
    gain = lambda k, shape: 1.0 + 0.01 * jax.random.normal(k, shape, f32)
    return {
        'x': nrm(ks[0], (BATCH, SEQ, D_MODEL), 1.0),
        'c': nrm(ks[1], (BATCH, D_MODEL), 1.0),
        'ctx': nrm(ks[2], (BATCH, CTX_LEN, D_MODEL), 1.0),
        'c_ctx': nrm(ks[3], (D_MODEL,), 1.0),
        'w_mod': nrm(ks[4], (DEPTH, D_MODEL, 6 * D_MODEL), 0.5 * D_MODEL ** -0.5),
        'b_mod': nrm(ks[5], (DEPTH, 6 * D_MODEL), 0.01),
        'norm1': gain(ks[6], (DEPTH, D_MODEL)),
        'w_in': nrm(ks[7], (DEPTH, D_MODEL, IN_PROJ), D_MODEL ** -0.5),
        'q_norm': gain(ks[8], (DEPTH, HEAD_DIM)),
        'k_norm': gain(ks[9], (DEPTH, HEAD_DIM)),
        'b_igate': -1.0 + nrm(ks[10], (DEPTH, N_DIR, M_HEADS), 0.1),
        'b_fgate': jnp.linspace(3.0, 6.0, M_HEADS, dtype=f32) + nrm(ks[11], (DEPTH, N_DIR, M_HEADS), 0.1),
        'm_norm': gain(ks[12], (DEPTH, M_WIDTH)),
        'w_out': nrm(ks[13], (DEPTH, D_MIX, D_MODEL), D_MIX ** -0.5),
        'norm2': gain(ks[14], (DEPTH, D_MODEL)),
        'w_up': nrm(ks[15], (DEPTH, D_MODEL, 2 * D_FF), D_MODEL ** -0.5),
        'conv_w': nrm(ks[16], (DEPTH, CONV_W, 2 * D_FF), CONV_W ** -0.5),
        'conv_b': nrm(ks[17], (DEPTH, 2 * D_FF), 0.01),
        'w_down': nrm(ks[18], (DEPTH, D_FF, D_MODEL), D_FF ** -0.5),
        'norm_f': gain(ks[19], (D_MODEL,)),
    }


def reference(x, c, ctx, c_ctx, w_mod, b_mod, norm1, w_in, q_norm, k_norm, b_igate, b_fgate,
              m_norm, w_out, norm2, w_up, conv_w, conv_b, w_down, norm_f):
    B, n_tok, _ = x.shape
    rows = n_tok // GRID_W
    row = jnp.repeat(jnp.arange(rows, dtype=jnp.float32), GRID_W)
    col = jnp.tile(jnp.arange(GRID_W, dtype=jnp.float32), rows)
    inv_freq = ROPE_THETA ** (-jnp.arange(0, AXIS_DIM, 2, dtype=jnp.float32) / AXIS_DIM)
    ang_r = row[:, None] * inv_freq[None, :]
    ang_c = col[:, None] * inv_freq[None, :]
    rope = (jnp.cos(ang_r).astype(x.dtype), jnp.sin(ang_r).astype(x.dtype),
            jnp.cos(ang_c).astype(x.dtype), jnp.sin(ang_c).astype(x.dtype))

    for layer in range(DEPTH):
        mod_x = (jax.nn.silu(c) @ w_mod[layer] + b_mod[layer])[:, None, :]
        mod_c = (jax.nn.silu(c_ctx) @ w_mod[layer] + b_mod[layer])[None, None, :]
        sh1x, sc1x, g1x, sh2x, sc2x, g2x = jnp.split(mod_x, 6, axis=-1)
        sh1c, sc1c, g1c, sh2c, sc2c, g2c = jnp.split(mod_c, 6, axis=-1)

        px = modulate(rmsnorm(x, norm1[layer]), sh1x, sc1x) @ w_in[layer]
        pc = modulate(rmsnorm(ctx, norm1[layer]), sh1c, sc1c) @ w_in[layer]

        qx, kx, vx = attn_qkv(px, q_norm[layer], k_norm[layer])
        qc, kc, vc = attn_qkv(pc, q_norm[layer], k_norm[layer])
        qx = axial_rope(qx, *rope)
        kx = axial_rope(kx, *rope)
        att_x = block_attention(qx, jnp.concatenate([kc, kx], axis=1), jnp.concatenate([vc, vx], axis=1))

        mq_c, mk_c, mv_c, mo_c, mi_c, mf_c = mlstm_inputs(pc, b_igate[layer], b_fgate[layer])
        mq_x, mk_x, mv_x, mo_x, mi_x, mf_x = mlstm_inputs(px, b_igate[layer], b_fgate[layer])
        h_c, st_f, st_b = mlstm_bidir(mq_c, mk_c, mv_c, mi_c, mf_c, zero_state(B), zero_state(B))
        h_x, _, _ = mlstm_bidir(mq_x, mk_x, mv_x, mi_x, mf_x, st_f, st_b)

        mix_x = jnp.concatenate([att_x, mlstm_readout(h_x, mo_x, m_norm[layer])], axis=-1)
        x = x + g1x * (mix_x @ w_out[layer])
        x = x + g2x * conv_ffn(modulate(rmsnorm(x, norm2[layer]), sh2x, sc2x),
                               w_up[layer], conv_w[layer], conv_b[layer], w_down[layer])

        if layer + 1 < DEPTH:
            att_c = block_attention(qc, kc, vc)
            mix_c = jnp.concatenate([att_c, mlstm_readout(h_c, mo_c, m_norm[layer])], axis=-1)
            ctx = ctx + g1c * (mix_c @ w_out[layer])
            ctx = ctx + g2c * conv_ffn(modulate(rmsnorm(ctx, norm2[layer]), sh2c, sc2c),
                                       w_up[layer], conv_w[layer], conv_b[layer], w_down[layer])

    return rmsnorm(x, norm_f)
```

```python
import functools
import math

import jax
import jax.numpy as jnp
from jax import lax
from jax.experimental import pallas as pl
from jax.experimental.pallas import tpu as pltpu

F32 = jnp.float32
BF16 = jnp.bfloat16

GRID_W = 64
HEAD_DIM = 128
N_Q_HEADS = 8
N_KV_HEADS = 2
Q_PER_KV = N_Q_HEADS // N_KV_HEADS
AXIS_DIM = HEAD_DIM // 2
ROPE_THETA = 10000.0
M_HEADS = 4
M_V = 256
M_QK = 128
N_DIR = 2
NORM_EPS = 1e-6
ATT_WIDTH = N_Q_HEADS * HEAD_DIM
KV_WIDTH = N_KV_HEADS * HEAD_DIM
M_WIDTH = M_HEADS * M_V
MQK_WIDTH = M_HEADS * M_QK
N_GATES = N_DIR * 2 * M_HEADS

OFF_AQ = 0
OFF_AK = OFF_AQ + ATT_WIDTH
OFF_AV = OFF_AK + KV_WIDTH
OFF_MQ = OFF_AV + KV_WIDTH
OFF_MK = OFF_MQ + MQK_WIDTH
OFF_MV = OFF_MK + MQK_WIDTH
OFF_MO = OFF_MV + M_WIDTH
OFF_G = OFF_MO + M_WIDTH

LANES = 128
MXU_WIDTH = 256
BF16_SUBLANES = 16
VMEM_LIMIT = 50 * 1024 * 1024

ROW_TILE = 256
MLSTM_CHUNK = 256
ATTN_TQ = 256
ATTN_TK_CAP = 768
OUT_TM = 512
FFN_TM = 512
FFN_TF = 512
HALO = BF16_SUBLANES
MOD_TN = 512
LOG2E = 1.4426950408889634


def _largest_divisor(n, cap, mult):
    best = None
    for d in range(mult, cap + 1, mult):
        if n % d == 0:
            best = d
    assert best is not None, (n, cap, mult)
    return best


def _cparams(sem, vmem=VMEM_LIMIT):
    return pltpu.CompilerParams(dimension_semantics=sem, vmem_limit_bytes=vmem)


def _rms(x, eps=NORM_EPS):
    return x * lax.rsqrt(jnp.mean(x * x, axis=-1, keepdims=True) + eps)


def _mod_kernel(cs_ref, w_ref, b_ref, o_ref):
    cs = cs_ref[...]
    s = cs * jax.nn.sigmoid(cs)
    w = w_ref[...]
    b = b_ref[...]
    o_ref[0:1, :] = jnp.sum(s[:, 0:1] * w, axis=0, keepdims=True) + b
    o_ref[1:2, :] = jnp.sum(s[:, 1:2] * w, axis=0, keepdims=True) + b


def _modulation(cs, w_mod, b_mod):
    d, n = w_mod.shape
    tn = MOD_TN
    return pl.pallas_call(
        _mod_kernel,
        grid=(n // tn,),
        in_specs=[
            pl.BlockSpec((d, 2), lambda j: (0, 0)),
            pl.BlockSpec((d, tn), lambda j: (0, j)),
            pl.BlockSpec((1, tn), lambda j: (0, j)),
        ],
        out_specs=pl.BlockSpec((2, tn), lambda j: (0, j)),
        out_shape=jax.ShapeDtypeStruct((2, n), F32),
        compiler_params=_cparams(("arbitrary",)),
        name="modulation",
    )(cs, w_mod, b_mod.reshape(1, n))


def _swap_rope_halves(y):
    lane = lax.broadcasted_iota(jnp.int32, y.shape, 1)
    fwd = pltpu.roll(y, LANES - AXIS_DIM // 2, axis=1)
    bwd = pltpu.roll(y, AXIS_DIM // 2, axis=1)
    return jnp.where((lane & (AXIS_DIM // 2)) == 0, fwd, bwd)


def _inproj_kernel(x_ref, ctx_ref, mod_ref, n1_ref, w_ref, wg_ref, qg_ref, kg_ref, cos_ref, sin_ref,
                   q_ref, k_ref, v_ref, mq_ref, mk_ref, mv_ref, mo_ref, g_ref):
    d = x_ref.shape[1]
    is_ctx = pl.program_id(0) == 0
    xin = jnp.where(is_ctx, ctx_ref[...], x_ref[...])
    shift = jnp.where(is_ctx, mod_ref[1:2, 0:d], mod_ref[0:1, 0:d])
    scale = jnp.where(is_ctx, mod_ref[1:2, d:2 * d], mod_ref[0:1, d:2 * d])
    hf = _rms(xin) * n1_ref[...] * (1.0 + scale) + shift
    h_hi = hf.astype(BF16)
    h_lo = (hf - h_hi.astype(F32)).astype(BF16)

    def proj(c0, width):
        return jnp.dot(h_hi, w_ref[:, c0:c0 + width], preferred_element_type=F32)

    cos = cos_ref[...]
    sin = sin_ref[...]

    def norm_rope(y, gain):
        yn = _rms(y) * gain
        return yn * cos + _swap_rope_halves(yn) * sin

    for j in range(ATT_WIDTH // MXU_WIDTH):
        y = proj(OFF_AQ + j * MXU_WIDTH, MXU_WIDTH)
        for t in range(MXU_WIDTH // HEAD_DIM):
            c = j * MXU_WIDTH + t * HEAD_DIM
            q_ref[:, c:c + HEAD_DIM] = norm_rope(y[:, t * HEAD_DIM:(t + 1) * HEAD_DIM], qg_ref[...]).astype(BF16)
    y = proj(OFF_AK, KV_WIDTH)
    for t in range(N_KV_HEADS):
        k_ref[:, t * HEAD_DIM:(t + 1) * HEAD_DIM] = norm_rope(
            y[:, t * HEAD_DIM:(t + 1) * HEAD_DIM], kg_ref[...]).astype(BF16)
    v_ref[...] = proj(OFF_AV, KV_WIDTH).astype(BF16)
    for j in range(MQK_WIDTH // MXU_WIDTH):
        c = j * MXU_WIDTH
        mq_ref[:, c:c + MXU_WIDTH] = proj(OFF_MQ + c, MXU_WIDTH).astype(BF16)
        mk_ref[:, c:c + MXU_WIDTH] = (proj(OFF_MK + c, MXU_WIDTH) * (M_QK ** -0.5)).astype(BF16)
    for j in range(M_WIDTH // MXU_WIDTH):
        c = j * MXU_WIDTH
        mv_ref[:, c:c + MXU_WIDTH] = proj(OFF_MV + c, MXU_WIDTH).astype(BF16)
        mo_ref[:, c:c + MXU_WIDTH] = jax.nn.sigmoid(proj(OFF_MO + c, MXU_WIDTH)).astype(BF16)
    gg = (jnp.dot(h_hi, wg_ref[...], preferred_element_type=F32)
          + jnp.dot(h_lo, wg_ref[...], preferred_element_type=F32))
    gg = gg + pltpu.roll(gg, LANES - N_GATES, axis=1)
    g_ref[...] = gg[:, 0:N_GATES]


def _in_proj(x, ctx, mod, norm1, w_main, w_gate, q_gain, k_gain, cos_t, sin_t):
    t, d = x.shape
    n_ctx = ctx.shape[0]
    assert n_ctx == ROW_TILE and t % ROW_TILE == 0
    rows = n_ctx + t
    steps = rows // ROW_TILE
    full = lambda i: (0, 0)
    row_all = lambda i: (i, 0)
    row_x = lambda i: (jnp.maximum(i - 1, 0), 0)
    out_shape = [
        jax.ShapeDtypeStruct((t, ATT_WIDTH), BF16),
        jax.ShapeDtypeStruct((rows, KV_WIDTH), BF16),
        jax.ShapeDtypeStruct((rows, KV_WIDTH), BF16),
        jax.ShapeDtypeStruct((rows, MQK_WIDTH), BF16),
        jax.ShapeDtypeStruct((rows, MQK_WIDTH), BF16),
        jax.ShapeDtypeStruct((rows, M_WIDTH), BF16),
        jax.ShapeDtypeStruct((t, M_WIDTH), BF16),
        jax.ShapeDtypeStruct((rows, N_GATES), F32),
    ]
    out_specs = [
        pl.BlockSpec((ROW_TILE, ATT_WIDTH), row_x),
        pl.BlockSpec((ROW_TILE, KV_WIDTH), row_all),
        pl.BlockSpec((ROW_TILE, KV_WIDTH), row_all),
        pl.BlockSpec((ROW_TILE, MQK_WIDTH), row_all),
        pl.BlockSpec((ROW_TILE, MQK_WIDTH), row_all),
        pl.BlockSpec((ROW_TILE, M_WIDTH), row_all),
        pl.BlockSpec((ROW_TILE, M_WIDTH), row_x),
        pl.BlockSpec((ROW_TILE, N_GATES), row_all),
    ]
    in_specs = [
        pl.BlockSpec((ROW_TILE, d), row_x),
        pl.BlockSpec((ROW_TILE, d), full),
        pl.BlockSpec(mod.shape, full),
        pl.BlockSpec((1, d), full),
        pl.BlockSpec(w_main.shape, full),
        pl.BlockSpec(w_gate.shape, full),
        pl.BlockSpec((1, HEAD_DIM), full),
        pl.BlockSpec((1, HEAD_DIM), full),
        pl.BlockSpec((ROW_TILE, HEAD_DIM), row_all),
        pl.BlockSpec((ROW_TILE, HEAD_DIM), row_all),
    ]
    return pl.pallas_call(
        _inproj_kernel,
        grid=(steps,),
        in_specs=in_specs,
        out_specs=out_specs,
        out_shape=out_shape,
        compiler_params=_cparams(("arbitrary",)),
        name="in_proj",
    )(x, ctx, mod, norm1, w_main, w_gate, q_gain, k_gain, cos_t, sin_t)


def _attn_kernel(q_ref, k_ref, v_ref, o_ref, qs_ref, m_ref, l_ref, acc_ref, *, tk):
    tq = q_ref.shape[0]
    n_chunks = k_ref.shape[0] // tk
    for g in range(Q_PER_KV):
        qs_ref[g * tq:(g + 1) * tq, :] = q_ref[:, g * HEAD_DIM:(g + 1) * HEAD_DIM]
    m_ref[...] = jnp.full(m_ref.shape, -jnp.inf, F32)
    l_ref[...] = jnp.zeros(l_ref.shape, F32)
    acc_ref[...] = jnp.zeros(acc_ref.shape, F32)

    def body(c, carry):
        r0 = pl.multiple_of(c * tk, tk)
        k = k_ref[pl.ds(r0, tk), :]
        v = v_ref[pl.ds(r0, tk), :]
        s = lax.dot_general(qs_ref[...], k, (((1,), (1,)), ((), ())), preferred_element_type=F32)
        m_prev = m_ref[...]
        m_new = jnp.maximum(m_prev, jnp.max(s, axis=1, keepdims=True))
        p = jnp.exp2(s - pltpu.repeat(m_new, tk // LANES, axis=1))
        alpha = jnp.exp2(m_prev - m_new)
        l_ref[...] = alpha * l_ref[...] + jnp.sum(p, axis=1, keepdims=True)
        acc_ref[...] = alpha * acc_ref[...] + jnp.dot(p.astype(BF16), v, preferred_element_type=F32)
        m_ref[...] = m_new
        return carry

    lax.fori_loop(0, n_chunks, body, 0)
    out = acc_ref[...] / l_ref[...]
    for g in range(Q_PER_KV):
        o_ref[:, g * HEAD_DIM:(g + 1) * HEAD_DIM] = out[g * tq:(g + 1) * tq, :].astype(o_ref.dtype)


def _attention(q, k, v):
    t = q.shape[0]
    s_len = k.shape[0]
    tq = ATTN_TQ
    tk = _largest_divisor(s_len, ATTN_TK_CAP, LANES)
    group_w = Q_PER_KV * HEAD_DIM
    return pl.pallas_call(
        functools.partial(_attn_kernel, tk=tk),
        grid=(N_KV_HEADS, t // tq),
        in_specs=[
            pl.BlockSpec((tq, group_w), lambda h, i: (i, h)),
            pl.BlockSpec((s_len, HEAD_DIM), lambda h, i: (0, h), pipeline_mode=pl.Buffered(1)),
            pl.BlockSpec((s_len, HEAD_DIM), lambda h, i: (0, h), pipeline_mode=pl.Buffered(1)),
        ],
        out_specs=pl.BlockSpec((tq, group_w), lambda h, i: (i, h)),
        out_shape=jax.ShapeDtypeStruct((t, ATT_WIDTH), BF16),
        scratch_shapes=[
            pltpu.VMEM((Q_PER_KV * tq, HEAD_DIM), BF16),
            pltpu.VMEM((Q_PER_KV * tq, LANES), F32),
            pltpu.VMEM((Q_PER_KV * tq, LANES), F32),
            pltpu.VMEM((Q_PER_KV * tq, HEAD_DIM), F32),
        ],
        compiler_params=_cparams(("arbitrary", "arbitrary")),
        name="attention",
    )(q, k, v)


def _log_sigmoid(x):
    return jnp.minimum(x, 0.0) - jnp.log1p(jnp.exp(-jnp.abs(x)))


def _mlstm_unit(q, k, v, i_col, f_col, i_row, f_row, valid, valid_t, c_prev, n_prev, m_prev):
    cum_col = jnp.sum(jnp.where(valid, f_row, 0.0), axis=1, keepdims=True)
    cum_row = jnp.sum(jnp.where(valid_t, f_col, 0.0), axis=0, keepdims=True)
    total = jnp.sum(f_row, axis=1, keepdims=True)
    log_d = jnp.where(valid, cum_col - cum_row + i_row, -jnp.inf)
    log_inter = cum_col + m_prev
    m_t = jnp.maximum(jnp.max(log_d, axis=1, keepdims=True), log_inter)
    dmat = jnp.exp(log_d - m_t)
    w_inter = jnp.exp(log_inter - m_t)
    qk = lax.dot_general(q, k, (((1,), (1,)), ((), ())), preferred_element_type=F32)
    s = qk * dmat
    num = (w_inter * jnp.dot(q, c_prev.astype(BF16), preferred_element_type=F32)
           + jnp.dot(s.astype(BF16), v, preferred_element_type=F32))
    den = (w_inter * jnp.sum(q.astype(F32) * n_prev, axis=1, keepdims=True)
           + jnp.sum(s, axis=1, keepdims=True))
    h = num / jnp.maximum(jnp.abs(den), jnp.exp(-m_t))
    log_w = total - cum_col + i_col
    m_new = jnp.maximum(total + m_prev, jnp.max(log_w, axis=0, keepdims=True))
    w_prev = jnp.exp(total + m_prev - m_new)
    kw = k.astype(F32) * jnp.exp(log_w - m_new)
    c_new = w_prev * c_prev + lax.dot_general(kw.astype(BF16), v, (((0,), (0,)), ((), ())),
                                              preferred_element_type=F32)
    n_new = w_prev * n_prev + jnp.sum(kw, axis=0, keepdims=True)
    return h, c_new, n_new, m_new


def _mlstm_kernel(qf_ref, kf_ref, vf_ref, gf_ref, gtf_ref, qb_ref, kb_ref, vb_ref, gb_ref, gtb_ref,
                  bi_ref, bf_ref, hf_ref, hb_ref, c_scr, n_scr, m_scr):
    L = qf_ref.shape[0]

    @pl.when(pl.program_id(0) == 0)
    def _():
        c_scr[...] = jnp.zeros(c_scr.shape, F32)
        n_scr[...] = jnp.zeros(n_scr.shape, F32)
        m_scr[...] = jnp.zeros(m_scr.shape, F32)

    row = lax.broadcasted_iota(jnp.int32, (L, L), 0)
    col = lax.broadcasted_iota(jnp.int32, (L, L), 1)
    lower = col <= row
    upper = col >= row
    dirs = (
        (qf_ref, kf_ref, vf_ref, gf_ref, gtf_ref, hf_ref, lower, upper),
        (qb_ref, kb_ref, vb_ref, gb_ref, gtb_ref, hb_ref, upper, lower),
    )
    for d, (q_ref, k_ref, v_ref, g_ref, gt_ref, h_ref, valid, valid_t) in enumerate(dirs):
        g = g_ref[...]
        gt = gt_ref[...]
        for hd in range(M_HEADS):
            u = d * M_HEADS + hd
            ci = d * 2 * M_HEADS + hd
            cf = ci + M_HEADS
            b_i = bi_ref[d:d + 1, hd:hd + 1]
            b_f = bf_ref[d:d + 1, hd:hd + 1]
            i_col = g[:, ci:ci + 1] + b_i
            f_col = _log_sigmoid(g[:, cf:cf + 1] + b_f)
            i_row = gt[ci:ci + 1, :] + b_i
            f_row = _log_sigmoid(gt[cf:cf + 1, :] + b_f)
            h, c_new, n_new, m_new = _mlstm_unit(
                q_ref[:, hd * M_QK:(hd + 1) * M_QK], k_ref[:, hd * M_QK:(hd + 1) * M_QK],
                v_ref[:, hd * M_V:(hd + 1) * M_V], i_col, f_col, i_row, f_row, valid, valid_t,
                c_scr[u], n_scr[u][0:1, :], m_scr[u][0:1, 0:1])
            h_ref[:, hd * M_V:(hd + 1) * M_V] = h
            c_scr[u] = c_new
            n_scr[u] = jnp.broadcast_to(n_new, n_scr.shape[1:])
            m_scr[u] = jnp.broadcast_to(m_new, m_scr.shape[1:])


def _mlstm(mq, mk, mv, gates, gates_t, b_i, b_f, n_x):
    L = MLSTM_CHUNK
    rows = mq.shape[0]
    assert rows % L == 0 and n_x % L == 0 and rows - n_x == L
    nx = n_x // L
    steps = nx + 1
    f_in = lambda g: (g, 0)
    b_in = lambda g: (jnp.where(g == 0, 0, nx + 1 - g), 0)
    f_in_t = lambda g: (0, g)
    b_in_t = lambda g: (0, jnp.where(g == 0, 0, nx + 1 - g))
    f_out = lambda g: (jnp.maximum(g - 1, 0), 0)
    b_out = lambda g: (jnp.where(g == 0, nx - 1, nx - g), 0)
    small = lambda g: (0, 0)
    spec = lambda w, im: pl.BlockSpec((L, w), im)
    in_specs = [
        spec(MQK_WIDTH, f_in), spec(MQK_WIDTH, f_in), spec(M_WIDTH, f_in), spec(N_GATES, f_in),
        pl.BlockSpec((N_GATES, L), f_in_t),
        spec(MQK_WIDTH, b_in), spec(MQK_WIDTH, b_in), spec(M_WIDTH, b_in), spec(N_GATES, b_in),
        pl.BlockSpec((N_GATES, L), b_in_t),
        pl.BlockSpec(b_i.shape, small), pl.BlockSpec(b_f.shape, small),
    ]
    n_units = N_DIR * M_HEADS
    return pl.pallas_call(
        _mlstm_kernel,
        grid=(steps,),
        in_specs=in_specs,
        out_specs=[spec(M_WIDTH, f_out), spec(M_WIDTH, b_out)],
        out_shape=[jax.ShapeDtypeStruct((n_x, M_WIDTH), F32)] * 2,
        scratch_shapes=[
            pltpu.VMEM((n_units, M_QK, M_V), F32),
            pltpu.VMEM((n_units, 8, M_QK), F32),
            pltpu.VMEM((n_units, 8, LANES), F32),
        ],
        compiler_params=_cparams(("arbitrary",)),
        name="mlstm",
    )(mq, mk, mv, gates, gates_t, mq, mk, mv, gates, gates_t, b_i, b_f)


def _outproj_kernel(att_ref, hf_ref, hb_ref, mo_ref, mg_ref, w_ref, x_ref, mod_ref, n2_ref,
                    x1_ref, h2_ref):
    d = x_ref.shape[1]
    hsum = hf_ref[...] + hb_ref[...]
    y = jnp.dot(att_ref[...], w_ref[0:ATT_WIDTH, :], preferred_element_type=F32)
    for hd in range(M_HEADS):
        c = hd * M_V
        r = (_rms(hsum[:, c:c + M_V]) * mg_ref[:, c:c + M_V] * mo_ref[:, c:c + M_V].astype(F32)).astype(BF16)
        y = y + jnp.dot(r, w_ref[ATT_WIDTH + c:ATT_WIDTH + c + M_V, :], preferred_element_type=F32)
    x1 = x_ref[...] + mod_ref[0:1, 2 * d:3 * d] * y
    x1_ref[...] = x1
    h2 = _rms(x1) * n2_ref[...] * (1.0 + mod_ref[0:1, 4 * d:5 * d]) + mod_ref[0:1, 3 * d:4 * d]
    h2_ref[...] = h2.astype(BF16)


def _out_proj(att, hf, hb, mo, m_gain, w_out, x, mod, norm2):
    t, d = x.shape
    tm = OUT_TM
    row = lambda i: (i, 0)
    full = lambda i: (0, 0)
    return pl.pallas_call(
        _outproj_kernel,
        grid=(t // tm,),
        in_specs=[
            pl.BlockSpec((tm, ATT_WIDTH), row),
            pl.BlockSpec((tm, M_WIDTH), row),
            pl.BlockSpec((tm, M_WIDTH), row),
            pl.BlockSpec((tm, M_WIDTH), row),
            pl.BlockSpec((1, M_WIDTH), full),
            pl.BlockSpec(w_out.shape, full),
            pl.BlockSpec((tm, d), row),
            pl.BlockSpec(mod.shape, full),
            pl.BlockSpec((1, d), full),
        ],
        out_specs=[pl.BlockSpec((tm, d), row), pl.BlockSpec((tm, d), row)],
        out_shape=[jax.ShapeDtypeStruct((t, d), F32), jax.ShapeDtypeStruct((t, d), BF16)],
        compiler_params=_cparams(("arbitrary",)),
        name="out_proj",
    )(att, hf, hb, mo, m_gain, w_out, x, mod, norm2)


def _ffn_kernel(h_ref, hp_ref, hn_ref, wg_ref, wv_ref, cwg_ref, cwv_ref, cbg_ref, cbv_ref, wd_ref,
                x1_ref, mod_ref, nf_ref, o_ref, hext_ref, ug_ref, uv_ref):
    i = pl.program_id(0)
    j = pl.program_id(1)
    tm = h_ref.shape[0]
    d = x1_ref.shape[1]

    @pl.when(j == 0)
    def _():
        hext_ref[0:HALO, :] = jnp.where(i == 0, jnp.zeros_like(hp_ref[...]), hp_ref[...])
        hext_ref[HALO:HALO + tm, :] = h_ref[...]
        hext_ref[HALO + tm:, :] = jnp.where(i == pl.num_programs(0) - 1, jnp.zeros_like(hn_ref[...]), hn_ref[...])
        o_ref[...] = jnp.zeros(o_ref.shape, F32)

    hext = hext_ref[...]
    ug_ref[...] = jnp.dot(hext, wg_ref[...], preferred_element_type=F32)
    uv_ref[...] = jnp.dot(hext, wv_ref[...], preferred_element_type=F32)

    def conv(u_ref, cw_ref, cb_ref):
        return (cw_ref[0:1, :] * u_ref[HALO - 1:HALO - 1 + tm, :]
                + cw_ref[1:2, :] * u_ref[HALO:HALO + tm, :]
                + cw_ref[2:3, :] * u_ref[HALO + 1:HALO + 1 + tm, :]
                + cb_ref[...])

    g = conv(ug_ref, cwg_ref, cbg_ref)
    val = conv(uv_ref, cwv_ref, cbv_ref)
    a = (g * jax.nn.sigmoid(g) * val).astype(BF16)
    o_ref[...] += jnp.dot(a, wd_ref[...], preferred_element_type=F32)

    @pl.when(j == pl.num_programs(1) - 1)
    def _():
        y = x1_ref[...] + mod_ref[0:1, 5 * d:6 * d] * o_ref[...]
        o_ref[...] = _rms(y) * nf_ref[...]


def _conv_ffn(h2, w_up, conv_w, conv_b, w_down, x1, mod, norm_f):
    t, d = h2.shape
    d_ff = w_down.shape[0]
    tm, tf = FFN_TM, FFN_TF
    nf = d_ff // tf
    hb = tm // HALO
    last_halo = t // HALO - 1
    in_specs = [
        pl.BlockSpec((tm, d), lambda i, j: (i, 0)),
        pl.BlockSpec((HALO, d), lambda i, j: (jnp.maximum(i * hb - 1, 0), 0)),
        pl.BlockSpec((HALO, d), lambda i, j: (jnp.minimum((i + 1) * hb, last_halo), 0)),
        pl.BlockSpec((d, tf), lambda i, j: (0, j)),
        pl.BlockSpec((d, tf), lambda i, j: (0, j + nf)),
        pl.BlockSpec((3, tf), lambda i, j: (0, j)),
        pl.BlockSpec((3, tf), lambda i, j: (0, j + nf)),
        pl.BlockSpec((1, tf), lambda i, j: (0, j)),
        pl.BlockSpec((1, tf), lambda i, j: (0, j + nf)),
        pl.BlockSpec((tf, d), lambda i, j: (j, 0)),
        pl.BlockSpec((tm, d), lambda i, j: (i, 0)),
        pl.BlockSpec(mod.shape, lambda i, j: (0, 0)),
        pl.BlockSpec((1, d), lambda i, j: (0, 0)),
    ]
    return pl.pallas_call(
        _ffn_kernel,
        grid=(t // tm, nf),
        in_specs=in_specs,
        out_specs=pl.BlockSpec((tm, d), lambda i, j: (i, 0)),
        out_shape=jax.ShapeDtypeStruct((t, d), F32),
        scratch_shapes=[
            pltpu.VMEM((tm + 2 * HALO, d), BF16),
            pltpu.VMEM((tm + 2 * HALO, tf), F32),
            pltpu.VMEM((tm + 2 * HALO, tf), F32),
        ],
        compiler_params=_cparams(("arbitrary", "arbitrary")),
        name="conv_ffn",
    )(h2, h2, h2, w_up, w_up, conv_w, conv_w, conv_b, conv_b, w_down, x1, mod, norm_f)


def _rope_tables(n_ctx, n_tok):
    rows = n_tok // GRID_W
    row = jnp.repeat(jnp.arange(rows, dtype=F32), GRID_W)
    col = jnp.tile(jnp.arange(GRID_W, dtype=F32), rows)
    inv_freq = ROPE_THETA ** (-jnp.arange(0, AXIS_DIM, 2, dtype=F32) / AXIS_DIM)
    ang_r = row[:, None] * inv_freq[None, :]
    ang_c = col[:, None] * inv_freq[None, :]
    cos = jnp.concatenate([jnp.cos(ang_r), jnp.cos(ang_r), jnp.cos(ang_c), jnp.cos(ang_c)], axis=1)
    sin = jnp.concatenate([-jnp.sin(ang_r), jnp.sin(ang_r), -jnp.sin(ang_c), jnp.sin(ang_c)], axis=1)
    cos = jnp.concatenate([jnp.ones((n_ctx, HEAD_DIM), F32), cos], axis=0)
    sin = jnp.concatenate([jnp.zeros((n_ctx, HEAD_DIM), F32), sin], axis=0)
    return cos, sin


def kernel(x, c, ctx, c_ctx, w_mod, b_mod, norm1, w_in, q_norm, k_norm, b_igate, b_fgate, m_norm,
           w_out, norm2, w_up, conv_w, conv_b, w_down, norm_f):
    batch, n_tok, d = x.shape
    assert batch == 1 and w_mod.shape[0] == 1
    n_ctx = ctx.shape[1]
    x2 = x[0]
    ctx2 = ctx[0]

    cs = jnp.stack([c[0], c_ctx], axis=1)
    mod = _modulation(cs, w_mod[0], b_mod[0])

    w_in0 = w_in[0]
    w_main = w_in0[:, :OFF_G].astype(BF16)
    wg = w_in0[:, OFF_G:]
    wg_hi = wg.astype(BF16)
    wg_lo = (wg - wg_hi.astype(F32)).astype(BF16)
    w_gate = jnp.concatenate([wg_hi, wg_lo, jnp.zeros((d, LANES - 2 * N_GATES), BF16)], axis=1)
    cos_t, sin_t = _rope_tables(n_ctx, n_tok)
    q_gain = (q_norm[0] * (HEAD_DIM ** -0.5 * LOG2E)).reshape(1, HEAD_DIM)
    k_gain = k_norm[0].reshape(1, HEAD_DIM)
    q, k, v, mq, mk, mv, mo, gates = _in_proj(
        x2, ctx2, mod, norm1[0].reshape(1, d), w_main, w_gate, q_gain, k_gain, cos_t, sin_t)

    att = _attention(q, k, v)

    hf, hb = _mlstm(mq, mk, mv, gates, gates.T, b_igate[0], b_fgate[0], n_tok)

    x1, h2 = _out_proj(att, hf, hb, mo, m_norm[0].reshape(1, M_WIDTH), w_out[0].astype(BF16), x2, mod,
                       norm2[0].reshape(1, d))

    out = _conv_ffn(h2, w_up[0].astype(BF16), conv_w[0], conv_b[0].reshape(1, -1), w_down[0].astype(BF16),
                    x1, mod, norm_f.reshape(1, d))
    return out[None]
```

```python
import functools
import math

import jax
import jax.numpy as jnp
import numpy as np
from jax import lax
from jax.experimental import pallas as pl
from jax.experimental.pallas import tpu as pltpu

F32 = jnp.float32
BF16 = jnp.bfloat16

GRID_W = 64
HEAD_DIM = 128
N_Q_HEADS = 8
N_KV_HEADS = 2
Q_PER_KV = N_Q_HEADS // N_KV_HEADS
AXIS_DIM = HEAD_DIM // 2
ROPE_THETA = 10000.0
M_HEADS = 4
M_V = 256
M_QK = 128
N_DIR = 2
NORM_EPS = 1e-6
ATT_WIDTH = N_Q_HEADS * HEAD_DIM
KV_WIDTH = N_KV_HEADS * HEAD_DIM
M_WIDTH = M_HEADS * M_V
MQK_WIDTH = M_HEADS * M_QK
N_GATES = N_DIR * 2 * M_HEADS

OFF_AQ = 0
OFF_AK = OFF_AQ + ATT_WIDTH
OFF_AV = OFF_AK + KV_WIDTH
OFF_MQ = OFF_AV + KV_WIDTH
OFF_MK = OFF_MQ + MQK_WIDTH
OFF_MV = OFF_MK + MQK_WIDTH
OFF_MO = OFF_MV + M_WIDTH
OFF_G = OFF_MO + M_WIDTH

LANES = 128
MXU_WIDTH = 256
BF16_SUBLANES = 16
VMEM_LIMIT = 50 * 1024 * 1024

ROW_TILE = 256
MLSTM_CHUNK = 256
ATTN_TQ = 256
ATTN_TK_CAP = 768
ATTN_LOOKAHEAD = 4
OUT_TM = 512
FFN_TM = 512
FFN_TF = 512
HALO = BF16_SUBLANES
MOD_TN = 512
WCAST_TN = 512
LOG2E = 1.4426950408889634


def _largest_divisor(n, cap, mult):
    best = None
    for d in range(mult, cap + 1, mult):
        if n % d == 0:
            best = d
    assert best is not None, (n, cap, mult)
    return best


def _cparams(sem, vmem=VMEM_LIMIT):
    return pltpu.CompilerParams(dimension_semantics=sem, vmem_limit_bytes=vmem)


def _rms(x, eps=NORM_EPS):
    return x * lax.rsqrt(jnp.mean(x * x, axis=-1, keepdims=True) + eps)


def _mod_kernel(cs_ref, w_ref, b_ref, o_ref):
    cs = cs_ref[...]
    s = cs * jax.nn.sigmoid(cs)
    w = w_ref[...]
    b = b_ref[...]
    o_ref[0:1, :] = jnp.sum(s[:, 0:1] * w, axis=0, keepdims=True) + b
    o_ref[1:2, :] = jnp.sum(s[:, 1:2] * w, axis=0, keepdims=True) + b


def _modulation(cs, w_mod, b_mod):
    d, n = w_mod.shape
    tn = MOD_TN
    return pl.pallas_call(
        _mod_kernel,
        grid=(n // tn,),
        in_specs=[
            pl.BlockSpec((d, 2), lambda j: (0, 0)),
            pl.BlockSpec((d, tn), lambda j: (0, j)),
            pl.BlockSpec((1, tn), lambda j: (0, j)),
        ],
        out_specs=pl.BlockSpec((2, tn), lambda j: (0, j)),
        out_shape=jax.ShapeDtypeStruct((2, n), F32),
        compiler_params=_cparams(("arbitrary",)),
        name="modulation",
    )(cs, w_mod, b_mod.reshape(1, n))


def _gate_weight_kernel(wt_ref, o_ref):
    g = wt_ref[...]
    hi = g.astype(BF16)
    lo = (g - hi.astype(F32)).astype(BF16)
    pad = jnp.zeros((o_ref.shape[0] - 2 * N_GATES, g.shape[1]), BF16)
    o_ref[...] = jnp.concatenate([hi, lo, pad], axis=0)


def _gate_weights(w_t, row0):
    n, d = w_t.shape
    assert row0 % N_GATES == 0 and row0 + N_GATES == n
    return pl.pallas_call(
        _gate_weight_kernel,
        grid=(1,),
        in_specs=[pl.BlockSpec((N_GATES, d), lambda i: (row0 // N_GATES, 0))],
        out_specs=pl.BlockSpec((LANES, d), lambda i: (0, 0)),
        out_shape=jax.ShapeDtypeStruct((LANES, d), BF16),
        compiler_params=_cparams(("arbitrary",)),
        name="w_gate_split",
    )(w_t)


def _wcast_kernel(wt_ref, o_ref):
    o_ref[...] = wt_ref[...].T.astype(BF16)


def _cast_main_columns(w_t, n_main):
    n, d = w_t.shape
    tn = WCAST_TN
    assert n_main % tn == 0 and n_main <= n
    return pl.pallas_call(
        _wcast_kernel,
        grid=(n_main // tn,),
        in_specs=[pl.BlockSpec((tn, d), lambda j: (j, 0))],
        out_specs=pl.BlockSpec((d, tn), lambda j: (0, j)),
        out_shape=jax.ShapeDtypeStruct((d, n_main), BF16),
        compiler_params=_cparams(("arbitrary",)),
        name="w_in_cast",
    )(w_t)


def _swap_rope_halves(y):
    lane = lax.broadcasted_iota(jnp.int32, y.shape, 1)
    fwd = pltpu.roll(y, LANES - AXIS_DIM // 2, axis=1)
    bwd = pltpu.roll(y, AXIS_DIM // 2, axis=1)
    return jnp.where((lane & (AXIS_DIM // 2)) == 0, fwd, bwd)


def _inproj_kernel(x_ref, ctx_ref, mod_ref, n1_ref, w_ref, wg_ref, qg_ref, kg_ref, cos_ref, sin_ref,
                   q_ref, k_ref, vt_ref, mq_ref, mk_ref, mv_ref, mo_ref, g_ref):
    d = x_ref.shape[1]
    is_ctx = pl.program_id(0) == 0
    xin = jnp.where(is_ctx, ctx_ref[...], x_ref[...])
    shift = jnp.where(is_ctx, mod_ref[1:2, 0:d], mod_ref[0:1, 0:d])
    scale = jnp.where(is_ctx, mod_ref[1:2, d:2 * d], mod_ref[0:1, d:2 * d])
    hf = _rms(xin) * n1_ref[...] * (1.0 + scale) + shift
    h_hi = hf.astype(BF16)
    h_lo = (hf - h_hi.astype(F32)).astype(BF16)

    def proj(c0, width):
        return jnp.dot(h_hi, w_ref[:, c0:c0 + width], preferred_element_type=F32)

    cos = cos_ref[...]
    sin = sin_ref[...]

    def norm_rope(y, gain):
        yn = _rms(y) * gain
        return yn * cos + _swap_rope_halves(yn) * sin

    for j in range(ATT_WIDTH // MXU_WIDTH):
        y = proj(OFF_AQ + j * MXU_WIDTH, MXU_WIDTH)
        for t in range(MXU_WIDTH // HEAD_DIM):
            c = j * MXU_WIDTH + t * HEAD_DIM
            q_ref[:, c:c + HEAD_DIM] = norm_rope(y[:, t * HEAD_DIM:(t + 1) * HEAD_DIM], qg_ref[...]).astype(BF16)
    y = proj(OFF_AK, KV_WIDTH)
    for t in range(N_KV_HEADS):
        k_ref[:, t * HEAD_DIM:(t + 1) * HEAD_DIM] = norm_rope(
            y[:, t * HEAD_DIM:(t + 1) * HEAD_DIM], kg_ref[...]).astype(BF16)
    vt_ref[...] = proj(OFF_AV, KV_WIDTH).T.astype(BF16)
    for j in range(MQK_WIDTH // MXU_WIDTH):
        c = j * MXU_WIDTH
        mq_ref[:, c:c + MXU_WIDTH] = proj(OFF_MQ + c, MXU_WIDTH).astype(BF16)
        mk_ref[:, c:c + MXU_WIDTH] = (proj(OFF_MK + c, MXU_WIDTH) * (M_QK ** -0.5)).astype(BF16)
    for j in range(M_WIDTH // MXU_WIDTH):
        c = j * MXU_WIDTH
        mv_ref[:, c:c + MXU_WIDTH] = proj(OFF_MV + c, MXU_WIDTH).astype(BF16)
        mo_ref[:, c:c + MXU_WIDTH] = jax.nn.sigmoid(proj(OFF_MO + c, MXU_WIDTH)).astype(BF16)
    nt = (((1,), (1,)), ((), ()))
    gg = (lax.dot_general(h_hi, wg_ref[...], nt, preferred_element_type=F32)
          + lax.dot_general(h_lo, wg_ref[...], nt, preferred_element_type=F32))
    gg = gg + pltpu.roll(gg, LANES - N_GATES, axis=1)
    g_ref[...] = gg[:, 0:N_GATES]


def _in_proj(x, ctx, mod, norm1, w_main, w_gate, q_gain, k_gain, cos_t, sin_t):
    t, d = x.shape
    n_ctx = ctx.shape[0]
    assert n_ctx == ROW_TILE and t % ROW_TILE == 0
    rows = n_ctx + t
    steps = rows // ROW_TILE
    full = lambda i: (0, 0)
    row_all = lambda i: (i, 0)
    row_x = lambda i: (jnp.maximum(i - 1, 0), 0)
    out_shape = [
        jax.ShapeDtypeStruct((t, ATT_WIDTH), BF16),
        jax.ShapeDtypeStruct((rows, KV_WIDTH), BF16),
        jax.ShapeDtypeStruct((KV_WIDTH, rows), BF16),
        jax.ShapeDtypeStruct((rows, MQK_WIDTH), BF16),
        jax.ShapeDtypeStruct((rows, MQK_WIDTH), BF16),
        jax.ShapeDtypeStruct((rows, M_WIDTH), BF16),
        jax.ShapeDtypeStruct((t, M_WIDTH), BF16),
        jax.ShapeDtypeStruct((rows, N_GATES), F32),
    ]
    out_specs = [
        pl.BlockSpec((ROW_TILE, ATT_WIDTH), row_x),
        pl.BlockSpec((ROW_TILE, KV_WIDTH), row_all),
        pl.BlockSpec((KV_WIDTH, ROW_TILE), lambda i: (0, i)),
        pl.BlockSpec((ROW_TILE, MQK_WIDTH), row_all),
        pl.BlockSpec((ROW_TILE, MQK_WIDTH), row_all),
        pl.BlockSpec((ROW_TILE, M_WIDTH), row_all),
        pl.BlockSpec((ROW_TILE, M_WIDTH), row_x),
        pl.BlockSpec((ROW_TILE, N_GATES), row_all),
    ]
    in_specs = [
        pl.BlockSpec((ROW_TILE, d), row_x),
        pl.BlockSpec((ROW_TILE, d), full),
        pl.BlockSpec(mod.shape, full),
        pl.BlockSpec((1, d), full),
        pl.BlockSpec(w_main.shape, full),
        pl.BlockSpec(w_gate.shape, full),
        pl.BlockSpec((1, HEAD_DIM), full),
        pl.BlockSpec((1, HEAD_DIM), full),
        pl.BlockSpec((ROW_TILE, HEAD_DIM), row_all),
        pl.BlockSpec((ROW_TILE, HEAD_DIM), row_all),
    ]
    return pl.pallas_call(
        _inproj_kernel,
        grid=(steps,),
        in_specs=in_specs,
        out_specs=out_specs,
        out_shape=out_shape,
        compiler_params=_cparams(("arbitrary",)),
        name="in_proj",
    )(x, ctx, mod, norm1, w_main, w_gate, q_gain, k_gain, cos_t, sin_t)


def _attn_kernel(q_ref, k_ref, vt_ref, o_ref, qt_scr, acc_scr, *, tk):
    tq = q_ref.shape[0]
    n_chunks = k_ref.shape[0] // tk
    for g in range(Q_PER_KV):
        qt_scr[g] = q_ref[:, g * HEAD_DIM:(g + 1) * HEAD_DIM].astype(F32).T.astype(BF16)
    m = [jnp.full((1, tq), -jnp.inf, F32) for _ in range(Q_PER_KV)]
    l = [jnp.zeros((1, tq), F32) for _ in range(Q_PER_KV)]
    units = [(c, g) for c in range(n_chunks) for g in range(Q_PER_KV)]

    def scores(c, g):
        return jnp.dot(k_ref[c * tk:(c + 1) * tk, :], qt_scr[g], preferred_element_type=F32)

    pending = [scores(*u) for u in units[:ATTN_LOOKAHEAD]]
    for n, (c, g) in enumerate(units):
        st = pending.pop(0)
        if n + ATTN_LOOKAHEAD < len(units):
            pending.append(scores(*units[n + ATTN_LOOKAHEAD]))
        m_new = jnp.maximum(m[g], jnp.max(st, axis=0, keepdims=True))
        p = jnp.exp2(st - m_new)
        alpha = jnp.exp2(m[g] - m_new)
        l[g] = alpha * l[g] + jnp.sum(p, axis=0, keepdims=True)
        pv = jnp.dot(vt_ref[:, c * tk:(c + 1) * tk], p.astype(BF16), preferred_element_type=F32)
        if c == 0:
            acc_scr[g] = pv
        else:
            acc_scr[g] = alpha * acc_scr[g] + pv
        m[g] = m_new
    for g in range(Q_PER_KV):
        out = (acc_scr[g] / l[g]).T
        o_ref[:, g * HEAD_DIM:(g + 1) * HEAD_DIM] = out.astype(o_ref.dtype)


def _attention(q, k, vt):
    t = q.shape[0]
    s_len = k.shape[0]
    tq = ATTN_TQ
    tk = _largest_divisor(s_len, ATTN_TK_CAP, MXU_WIDTH)
    group_w = Q_PER_KV * HEAD_DIM
    return pl.pallas_call(
        functools.partial(_attn_kernel, tk=tk),
        grid=(N_KV_HEADS, t // tq),
        in_specs=[
            pl.BlockSpec((tq, group_w), lambda h, i: (i, h)),
            pl.BlockSpec((s_len, HEAD_DIM), lambda h, i: (0, h), pipeline_mode=pl.Buffered(1)),
            pl.BlockSpec((HEAD_DIM, s_len), lambda h, i: (h, 0), pipeline_mode=pl.Buffered(1)),
        ],
        out_specs=pl.BlockSpec((tq, group_w), lambda h, i: (i, h)),
        out_shape=jax.ShapeDtypeStruct((t, ATT_WIDTH), BF16),
        scratch_shapes=[
            pltpu.VMEM((Q_PER_KV, HEAD_DIM, tq), BF16),
            pltpu.VMEM((Q_PER_KV, HEAD_DIM, tq), F32),
        ],
        compiler_params=_cparams(("arbitrary", "arbitrary")),
        name="attention",
    )(q, k, vt)


def _log_sigmoid(x):
    return jnp.minimum(x, 0.0) - jnp.log1p(jnp.exp(-jnp.abs(x)))


def _mlstm_unit(q, k, v, i_col, f_col, i_row, f_row, valid, valid_t, c_prev, n_prev, m_prev):
    cum_col = jnp.sum(jnp.where(valid, f_row, 0.0), axis=1, keepdims=True)
    cum_row = jnp.sum(jnp.where(valid_t, f_col, 0.0), axis=0, keepdims=True)
    total = jnp.sum(f_row, axis=1, keepdims=True)
    log_d = jnp.where(valid, cum_col - cum_row + i_row, -jnp.inf)
    log_inter = cum_col + m_prev
    m_t = jnp.maximum(jnp.max(log_d, axis=1, keepdims=True), log_inter)
    dmat = jnp.exp(log_d - m_t)
    w_inter = jnp.exp(log_inter - m_t)
    qk = lax.dot_general(q, k, (((1,), (1,)), ((), ())), preferred_element_type=F32)
    s = qk * dmat
    num = (w_inter * jnp.dot(q, c_prev.astype(BF16), preferred_element_type=F32)
           + jnp.dot(s.astype(BF16), v, preferred_element_type=F32))
    den = (w_inter * jnp.sum(q.astype(F32) * n_prev, axis=1, keepdims=True)
           + jnp.sum(s, axis=1, keepdims=True))
    h = num / jnp.maximum(jnp.abs(den), jnp.exp(-m_t))
    log_w = total - cum_col + i_col
    m_new = jnp.maximum(total + m_prev, jnp.max(log_w, axis=0, keepdims=True))
    w_prev = jnp.exp(total + m_prev - m_new)
    kw = k.astype(F32) * jnp.exp(log_w - m_new)
    c_new = w_prev * c_prev + lax.dot_general(kw.astype(BF16), v, (((0,), (0,)), ((), ())),
                                              preferred_element_type=F32)
    n_new = w_prev * n_prev + jnp.sum(kw, axis=0, keepdims=True)
    return h, c_new, n_new, m_new


def _mlstm_kernel(qf_ref, kf_ref, vf_ref, gf_ref, gtf_ref, qb_ref, kb_ref, vb_ref, gb_ref, gtb_ref,
                  brow_ref, bcol_ref, hf_ref, hb_ref, c_scr, n_scr, m_scr):
    L = qf_ref.shape[0]

    @pl.when(pl.program_id(0) == 0)
    def _():
        c_scr[...] = jnp.zeros(c_scr.shape, F32)
        n_scr[...] = jnp.zeros(n_scr.shape, F32)
        m_scr[...] = jnp.zeros(m_scr.shape, F32)

    row = lax.broadcasted_iota(jnp.int32, (L, L), 0)
    col = lax.broadcasted_iota(jnp.int32, (L, L), 1)
    lower = col <= row
    upper = col >= row
    dirs = (
        (qf_ref, kf_ref, vf_ref, gf_ref, gtf_ref, hf_ref, lower, upper),
        (qb_ref, kb_ref, vb_ref, gb_ref, gtb_ref, hb_ref, upper, lower),
    )
    for d, (q_ref, k_ref, v_ref, g_ref, gt_ref, h_ref, valid, valid_t) in enumerate(dirs):
        g = g_ref[...] + brow_ref[...]
        gt = gt_ref[...] + bcol_ref[...]
        ls = _log_sigmoid(g)
        lst = _log_sigmoid(gt)
        for hd in range(M_HEADS):
            u = d * M_HEADS + hd
            ci = d * 2 * M_HEADS + hd
            cf = ci + M_HEADS
            i_col = g[:, ci:ci + 1]
            f_col = ls[:, cf:cf + 1]
            i_row = gt[ci:ci + 1, :]
            f_row = lst[cf:cf + 1, :]
            h, c_new, n_new, m_new = _mlstm_unit(
                q_ref[:, hd * M_QK:(hd + 1) * M_QK], k_ref[:, hd * M_QK:(hd + 1) * M_QK],
                v_ref[:, hd * M_V:(hd + 1) * M_V], i_col, f_col, i_row, f_row, valid, valid_t,
                c_scr[u], n_scr[u][0:1, :], m_scr[u][0:1, 0:1])
            h_ref[:, hd * M_V:(hd + 1) * M_V] = h
            c_scr[u] = c_new
            n_scr[u] = jnp.broadcast_to(n_new, n_scr.shape[1:])
            m_scr[u] = jnp.broadcast_to(m_new, m_scr.shape[1:])


def _mlstm(mq, mk, mv, gates, gates_t, b_i, b_f, n_x):
    L = MLSTM_CHUNK
    rows = mq.shape[0]
    assert rows % L == 0 and n_x % L == 0 and rows - n_x == L
    nx = n_x // L
    steps = nx + 1
    f_in = lambda g: (g, 0)
    b_in = lambda g: (jnp.where(g == 0, 0, nx + 1 - g), 0)
    f_in_t = lambda g: (0, g)
    b_in_t = lambda g: (0, jnp.where(g == 0, 0, nx + 1 - g))
    f_out = lambda g: (jnp.maximum(g - 1, 0), 0)
    b_out = lambda g: (jnp.where(g == 0, nx - 1, nx - g), 0)
    small = lambda g: (0, 0)
    spec = lambda w, im: pl.BlockSpec((L, w), im)
    in_specs = [
        spec(MQK_WIDTH, f_in), spec(MQK_WIDTH, f_in), spec(M_WIDTH, f_in), spec(N_GATES, f_in),
        pl.BlockSpec((N_GATES, L), f_in_t),
        spec(MQK_WIDTH, b_in), spec(MQK_WIDTH, b_in), spec(M_WIDTH, b_in), spec(N_GATES, b_in),
        pl.BlockSpec((N_GATES, L), b_in_t),
        pl.BlockSpec((1, N_GATES), small), pl.BlockSpec((N_GATES, 1), small),
    ]
    bias = jnp.concatenate([b_i, b_f], axis=1).reshape(1, N_GATES)
    n_units = N_DIR * M_HEADS
    return pl.pallas_call(
        _mlstm_kernel,
        grid=(steps,),
        in_specs=in_specs,
        out_specs=[spec(M_WIDTH, f_out), spec(M_WIDTH, b_out)],
        out_shape=[jax.ShapeDtypeStruct((n_x, M_WIDTH), F32)] * 2,
        scratch_shapes=[
            pltpu.VMEM((n_units, M_QK, M_V), F32),
            pltpu.VMEM((n_units, 8, M_QK), F32),
            pltpu.VMEM((n_units, 8, LANES), F32),
        ],
        compiler_params=_cparams(("arbitrary",)),
        name="mlstm",
    )(mq, mk, mv, gates, gates_t, mq, mk, mv, gates, gates_t, bias, bias.reshape(N_GATES, 1))


def _outproj_kernel(att_ref, hf_ref, hb_ref, mo_ref, mg_ref, w_ref, x_ref, mod_ref, n2_ref,
                    x1_ref, h2_ref):
    d = x_ref.shape[1]
    hsum = hf_ref[...] + hb_ref[...]
    y = jnp.dot(att_ref[...], w_ref[0:ATT_WIDTH, :], preferred_element_type=F32)
    for hd in range(M_HEADS):
        c = hd * M_V
        r = (_rms(hsum[:, c:c + M_V]) * mg_ref[:, c:c + M_V] * mo_ref[:, c:c + M_V].astype(F32)).astype(BF16)
        y = y + jnp.dot(r, w_ref[ATT_WIDTH + c:ATT_WIDTH + c + M_V, :], preferred_element_type=F32)
    x1 = x_ref[...] + mod_ref[0:1, 2 * d:3 * d] * y
    x1_ref[...] = x1
    h2 = _rms(x1) * n2_ref[...] * (1.0 + mod_ref[0:1, 4 * d:5 * d]) + mod_ref[0:1, 3 * d:4 * d]
    h2_ref[...] = h2.astype(BF16)


def _out_proj(att, hf, hb, mo, m_gain, w_out, x, mod, norm2):
    t, d = x.shape
    tm = OUT_TM
    row = lambda i: (i, 0)
    full = lambda i: (0, 0)
    return pl.pallas_call(
        _outproj_kernel,
        grid=(t // tm,),
        in_specs=[
            pl.BlockSpec((tm, ATT_WIDTH), row),
            pl.BlockSpec((tm, M_WIDTH), row),
            pl.BlockSpec((tm, M_WIDTH), row),
            pl.BlockSpec((tm, M_WIDTH), row),
            pl.BlockSpec((1, M_WIDTH), full),
            pl.BlockSpec(w_out.shape, full),
            pl.BlockSpec((tm, d), row),
            pl.BlockSpec(mod.shape, full),
            pl.BlockSpec((1, d), full),
        ],
        out_specs=[pl.BlockSpec((tm, d), row), pl.BlockSpec((tm, d), row)],
        out_shape=[jax.ShapeDtypeStruct((t, d), F32), jax.ShapeDtypeStruct((t, d), BF16)],
        compiler_params=_cparams(("arbitrary",)),
        name="out_proj",
    )(att, hf, hb, mo, m_gain, w_out, x, mod, norm2)


def _ffn_kernel(h_ref, hp_ref, hn_ref, wg_ref, wv_ref, cwg_ref, cwv_ref, cbg_ref, cbv_ref, wd_ref,
                x1_ref, mod_ref, nf_ref, o_ref, hext_ref, ug_ref, uv_ref):
    i = pl.program_id(0)
    j = pl.program_id(1)
    tm = h_ref.shape[0]
    d = x1_ref.shape[1]

    @pl.when(j == 0)
    def _():
        hext_ref[0:HALO, :] = jnp.where(i == 0, jnp.zeros_like(hp_ref[...]), hp_ref[...])
        hext_ref[HALO:HALO + tm, :] = h_ref[...]
        hext_ref[HALO + tm:, :] = jnp.where(i == pl.num_programs(0) - 1, jnp.zeros_like(hn_ref[...]), hn_ref[...])
        o_ref[...] = jnp.zeros(o_ref.shape, F32)

    hext = hext_ref[...]
    ug_ref[...] = jnp.dot(hext, wg_ref[...], preferred_element_type=F32)
    uv_ref[...] = jnp.dot(hext, wv_ref[...], preferred_element_type=F32)

    def conv(u_ref, cw_ref, cb_ref):
        return (cw_ref[0:1, :] * u_ref[HALO - 1:HALO - 1 + tm, :]
                + cw_ref[1:2, :] * u_ref[HALO:HALO + tm, :]
                + cw_ref[2:3, :] * u_ref[HALO + 1:HALO + 1 + tm, :]
                + cb_ref[...])

    g = conv(ug_ref, cwg_ref, cbg_ref)
    val = conv(uv_ref, cwv_ref, cbv_ref)
    a = (g * jax.nn.sigmoid(g) * val).astype(BF16)
    o_ref[...] += jnp.dot(a, wd_ref[...], preferred_element_type=F32)

    @pl.when(j == pl.num_programs(1) - 1)
    def _():
        y = x1_ref[...] + mod_ref[0:1, 5 * d:6 * d] * o_ref[...]
        o_ref[...] = _rms(y) * nf_ref[...]


def _conv_ffn(h2, w_up, conv_w, conv_b, w_down, x1, mod, norm_f):
    t, d = h2.shape
    d_ff = w_down.shape[0]
    tm, tf = FFN_TM, FFN_TF
    nf = d_ff // tf
    hb = tm // HALO
    last_halo = t // HALO - 1
    in_specs = [
        pl.BlockSpec((tm, d), lambda i, j: (i, 0)),
        pl.BlockSpec((HALO, d), lambda i, j: (jnp.maximum(i * hb - 1, 0), 0)),
        pl.BlockSpec((HALO, d), lambda i, j: (jnp.minimum((i + 1) * hb, last_halo), 0)),
        pl.BlockSpec((d, tf), lambda i, j: (0, j)),
        pl.BlockSpec((d, tf), lambda i, j: (0, j + nf)),
        pl.BlockSpec((3, tf), lambda i, j: (0, j)),
        pl.BlockSpec((3, tf), lambda i, j: (0, j + nf)),
        pl.BlockSpec((1, tf), lambda i, j: (0, j)),
        pl.BlockSpec((1, tf), lambda i, j: (0, j + nf)),
        pl.BlockSpec((tf, d), lambda i, j: (j, 0)),
        pl.BlockSpec((tm, d), lambda i, j: (i, 0)),
        pl.BlockSpec(mod.shape, lambda i, j: (0, 0)),
        pl.BlockSpec((1, d), lambda i, j: (0, 0)),
    ]
    return pl.pallas_call(
        _ffn_kernel,
        grid=(t // tm, nf),
        in_specs=in_specs,
        out_specs=pl.BlockSpec((tm, d), lambda i, j: (i, 0)),
        out_shape=jax.ShapeDtypeStruct((t, d), F32),
        scratch_shapes=[
            pltpu.VMEM((tm + 2 * HALO, d), BF16),
            pltpu.VMEM((tm + 2 * HALO, tf), F32),
            pltpu.VMEM((tm + 2 * HALO, tf), F32),
        ],
        compiler_params=_cparams(("arbitrary", "arbitrary")),
        name="conv_ffn",
    )(h2, h2, h2, w_up, w_up, conv_w, conv_w, conv_b, conv_b, w_down, x1, mod, norm_f)


def _rope_tables(n_ctx, n_tok):
    f32 = np.float32
    rows = n_tok // GRID_W
    row = np.repeat(np.arange(rows, dtype=f32), GRID_W)
    col = np.tile(np.arange(GRID_W, dtype=f32), rows)
    inv_freq = np.power(f32(ROPE_THETA), -np.arange(0, AXIS_DIM, 2, dtype=f32) / f32(AXIS_DIM)).astype(f32)
    ang_r = row[:, None] * inv_freq[None, :]
    ang_c = col[:, None] * inv_freq[None, :]
    cos = np.concatenate([np.cos(ang_r), np.cos(ang_r), np.cos(ang_c), np.cos(ang_c)], axis=1)
    sin = np.concatenate([-np.sin(ang_r), np.sin(ang_r), -np.sin(ang_c), np.sin(ang_c)], axis=1)
    cos = np.concatenate([np.ones((n_ctx, HEAD_DIM), f32), cos], axis=0).astype(f32)
    sin = np.concatenate([np.zeros((n_ctx, HEAD_DIM), f32), sin], axis=0).astype(f32)
    return jnp.asarray(cos), jnp.asarray(sin)


def kernel(x, c, ctx, c_ctx, w_mod, b_mod, norm1, w_in, q_norm, k_norm, b_igate, b_fgate, m_norm,
           w_out, norm2, w_up, conv_w, conv_b, w_down, norm_f):
    batch, n_tok, d = x.shape
    assert batch == 1 and w_mod.shape[0] == 1
    n_ctx = ctx.shape[1]
    x2 = x[0]
    ctx2 = ctx[0]

    cs = jnp.stack([c[0], c_ctx], axis=1)
    mod = _modulation(cs, w_mod[0], b_mod[0])

    w_in_t = w_in[0].T
    w_main = _cast_main_columns(w_in_t, OFF_G)
    w_gate = _gate_weights(w_in_t, OFF_G)
    cos_t, sin_t = _rope_tables(n_ctx, n_tok)
    q_gain = (q_norm[0] * (HEAD_DIM ** -0.5 * LOG2E)).reshape(1, HEAD_DIM)
    k_gain = k_norm[0].reshape(1, HEAD_DIM)
    q, k, vt, mq, mk, mv, mo, gates = _in_proj(
        x2, ctx2, mod, norm1[0].reshape(1, d), w_main, w_gate, q_gain, k_gain, cos_t, sin_t)

    att = _attention(q, k, vt)

    hf, hb = _mlstm(mq, mk, mv, gates, gates.T, b_igate[0], b_fgate[0], n_tok)

    x1, h2 = _out_proj(att, hf, hb, mo, m_norm[0].reshape(1, M_WIDTH), w_out[0].astype(BF16), x2, mod,
                       norm2[0].reshape(1, d))

    out = _conv_ffn(h2, w_up[0].astype(BF16), conv_w[0], conv_b[0].reshape(1, -1), w_down[0].astype(BF16),
                    x1, mod, norm_f.reshape(1, d))
    return out[None]
```

```python
import functools
import math

import jax
import jax.numpy as jnp
import numpy as np
from jax import lax
from jax.experimental import pallas as pl
from jax.experimental.pallas import tpu as pltpu

F32 = jnp.float32
BF16 = jnp.bfloat16

GRID_W = 64
HEAD_DIM = 128
N_Q_HEADS = 8
N_KV_HEADS = 2
Q_PER_KV = N_Q_HEADS // N_KV_HEADS
AXIS_DIM = HEAD_DIM // 2
ROPE_THETA = 10000.0
M_HEADS = 4
M_V = 256
M_QK = 128
N_DIR = 2
NORM_EPS = 1e-6
ATT_WIDTH = N_Q_HEADS * HEAD_DIM
KV_WIDTH = N_KV_HEADS * HEAD_DIM
M_WIDTH = M_HEADS * M_V
MQK_WIDTH = M_HEADS * M_QK
N_GATES = N_DIR * 2 * M_HEADS
N_UNITS = N_DIR * M_HEADS

OFF_AQ = 0
OFF_AK = OFF_AQ + ATT_WIDTH
OFF_AV = OFF_AK + KV_WIDTH
OFF_MQ = OFF_AV + KV_WIDTH
OFF_MK = OFF_MQ + MQK_WIDTH
OFF_MV = OFF_MK + MQK_WIDTH
OFF_MO = OFF_MV + M_WIDTH
OFF_G = OFF_MO + M_WIDTH

LANES = 128
MXU_WIDTH = 256
BF16_SUBLANES = 16
VMEM_LIMIT = 50 * 1024 * 1024

ROW_TILE = 256
MLSTM_CHUNK = 256
ATTN_TQ = 256
ATTN_TK_CAP = 768
ATTN_LOOKAHEAD = 4
OUT_TM = 512
FFN_TM = 512
FFN_TF = 512
HALO = BF16_SUBLANES
MOD_TN = 512
WCAST_TN = 512
LOG2E = 1.4426950408889634


def _largest_divisor(n, cap, mult):
    best = None
    for d in range(mult, cap + 1, mult):
        if n % d == 0:
            best = d
    assert best is not None, (n, cap, mult)
    return best


def _cparams(sem, vmem=VMEM_LIMIT):
    return pltpu.CompilerParams(dimension_semantics=sem, vmem_limit_bytes=vmem)


def _rms(x, eps=NORM_EPS):
    return x * lax.rsqrt(jnp.mean(x * x, axis=-1, keepdims=True) + eps)


def _mod_kernel(cs_ref, w_ref, b_ref, o_ref):
    cs = cs_ref[...]
    s = cs * jax.nn.sigmoid(cs)
    w = w_ref[...]
    b = b_ref[...]
    o_ref[0:1, :] = jnp.sum(s[:, 0:1] * w, axis=0, keepdims=True) + b
    o_ref[1:2, :] = jnp.sum(s[:, 1:2] * w, axis=0, keepdims=True) + b


def _modulation(cs, w_mod, b_mod):
    d, n = w_mod.shape
    tn = MOD_TN
    return pl.pallas_call(
        _mod_kernel,
        grid=(n // tn,),
        in_specs=[
            pl.BlockSpec((d, 2), lambda j: (0, 0)),
            pl.BlockSpec((d, tn), lambda j: (0, j)),
            pl.BlockSpec((1, tn), lambda j: (0, j)),
        ],
        out_specs=pl.BlockSpec((2, tn), lambda j: (0, j)),
        out_shape=jax.ShapeDtypeStruct((2, n), F32),
        compiler_params=_cparams(("arbitrary",)),
        name="modulation",
    )(cs, w_mod, b_mod.reshape(1, n))


def _gate_weight_kernel(wt_ref, o_ref):
    g = wt_ref[...]
    hi = g.astype(BF16)
    lo = (g - hi.astype(F32)).astype(BF16)
    pad = jnp.zeros((o_ref.shape[0] - 2 * N_GATES, g.shape[1]), BF16)
    o_ref[...] = jnp.concatenate([hi, lo, pad], axis=0)


def _gate_weights(w_t, row0):
    n, d = w_t.shape
    assert row0 % N_GATES == 0 and row0 + N_GATES == n
    return pl.pallas_call(
        _gate_weight_kernel,
        grid=(1,),
        in_specs=[pl.BlockSpec((N_GATES, d), lambda i: (row0 // N_GATES, 0))],
        out_specs=pl.BlockSpec((LANES, d), lambda i: (0, 0)),
        out_shape=jax.ShapeDtypeStruct((LANES, d), BF16),
        compiler_params=_cparams(("arbitrary",)),
        name="w_gate_split",
    )(w_t)


def _wcast_kernel(wt_ref, o_ref):
    o_ref[...] = wt_ref[...].T.astype(BF16)


def _cast_main_columns(w_t, n_main):
    n, d = w_t.shape
    tn = WCAST_TN
    assert n_main % tn == 0 and n_main <= n
    return pl.pallas_call(
        _wcast_kernel,
        grid=(n_main // tn,),
        in_specs=[pl.BlockSpec((tn, d), lambda j: (j, 0))],
        out_specs=pl.BlockSpec((d, tn), lambda j: (0, j)),
        out_shape=jax.ShapeDtypeStruct((d, n_main), BF16),
        compiler_params=_cparams(("arbitrary",)),
        name="w_in_cast",
    )(w_t)


def _swap_rope_halves(y):
    lane = lax.broadcasted_iota(jnp.int32, y.shape, 1)
    fwd = pltpu.roll(y, LANES - AXIS_DIM // 2, axis=1)
    bwd = pltpu.roll(y, AXIS_DIM // 2, axis=1)
    return jnp.where((lane & (AXIS_DIM // 2)) == 0, fwd, bwd)


def _inproj_kernel(x_ref, ctx_ref, mod_ref, n1_ref, w_ref, wg_ref, qg_ref, kg_ref, cos_ref, sin_ref,
                   q_ref, k_ref, vt_ref, mqt_ref, mk_ref, mvt_ref, mo_ref, g_ref):
    d = x_ref.shape[1]
    is_ctx = pl.program_id(0) == 0
    xin = jnp.where(is_ctx, ctx_ref[...], x_ref[...])
    shift = jnp.where(is_ctx, mod_ref[1:2, 0:d], mod_ref[0:1, 0:d])
    scale = jnp.where(is_ctx, mod_ref[1:2, d:2 * d], mod_ref[0:1, d:2 * d])
    hf = _rms(xin) * n1_ref[...] * (1.0 + scale) + shift
    h_hi = hf.astype(BF16)
    h_lo = (hf - h_hi.astype(F32)).astype(BF16)

    def proj(c0, width):
        return jnp.dot(h_hi, w_ref[:, c0:c0 + width], preferred_element_type=F32)

    cos = cos_ref[...]
    sin = sin_ref[...]

    def norm_rope(y, gain):
        yn = _rms(y) * gain
        return yn * cos + _swap_rope_halves(yn) * sin

    for j in range(ATT_WIDTH // MXU_WIDTH):
        y = proj(OFF_AQ + j * MXU_WIDTH, MXU_WIDTH)
        for t in range(MXU_WIDTH // HEAD_DIM):
            c = j * MXU_WIDTH + t * HEAD_DIM
            q_ref[:, c:c + HEAD_DIM] = norm_rope(y[:, t * HEAD_DIM:(t + 1) * HEAD_DIM], qg_ref[...]).astype(BF16)
    y = proj(OFF_AK, KV_WIDTH)
    for t in range(N_KV_HEADS):
        k_ref[:, t * HEAD_DIM:(t + 1) * HEAD_DIM] = norm_rope(
            y[:, t * HEAD_DIM:(t + 1) * HEAD_DIM], kg_ref[...]).astype(BF16)
    vt_ref[...] = proj(OFF_AV, KV_WIDTH).T.astype(BF16)
    for j in range(MQK_WIDTH // MXU_WIDTH):
        c = j * MXU_WIDTH
        mqt_ref[c:c + MXU_WIDTH, :] = proj(OFF_MQ + c, MXU_WIDTH).T.astype(BF16)
        mk_ref[:, c:c + MXU_WIDTH] = (proj(OFF_MK + c, MXU_WIDTH) * (M_QK ** -0.5)).astype(BF16)
    for j in range(M_WIDTH // MXU_WIDTH):
        c = j * MXU_WIDTH
        mvt_ref[c:c + MXU_WIDTH, :] = proj(OFF_MV + c, MXU_WIDTH).T.astype(BF16)
        mo_ref[:, c:c + MXU_WIDTH] = jax.nn.sigmoid(proj(OFF_MO + c, MXU_WIDTH)).astype(BF16)
    nt = (((1,), (1,)), ((), ()))
    gg = (lax.dot_general(h_hi, wg_ref[...], nt, preferred_element_type=F32)
          + lax.dot_general(h_lo, wg_ref[...], nt, preferred_element_type=F32))
    gg = gg + pltpu.roll(gg, LANES - N_GATES, axis=1)
    g_ref[...] = gg[:, 0:N_GATES]


def _in_proj(x, ctx, mod, norm1, w_main, w_gate, q_gain, k_gain, cos_t, sin_t):
    t, d = x.shape
    n_ctx = ctx.shape[0]
    assert n_ctx == ROW_TILE and t % ROW_TILE == 0
    rows = n_ctx + t
    steps = rows // ROW_TILE
    full = lambda i: (0, 0)
    row_all = lambda i: (i, 0)
    row_x = lambda i: (jnp.maximum(i - 1, 0), 0)
    out_shape = [
        jax.ShapeDtypeStruct((t, ATT_WIDTH), BF16),
        jax.ShapeDtypeStruct((rows, KV_WIDTH), BF16),
        jax.ShapeDtypeStruct((KV_WIDTH, rows), BF16),
        jax.ShapeDtypeStruct((MQK_WIDTH, rows), BF16),
        jax.ShapeDtypeStruct((rows, MQK_WIDTH), BF16),
        jax.ShapeDtypeStruct((M_WIDTH, rows), BF16),
        jax.ShapeDtypeStruct((t, M_WIDTH), BF16),
        jax.ShapeDtypeStruct((rows, N_GATES), F32),
    ]
    out_specs = [
        pl.BlockSpec((ROW_TILE, ATT_WIDTH), row_x),
        pl.BlockSpec((ROW_TILE, KV_WIDTH), row_all),
        pl.BlockSpec((KV_WIDTH, ROW_TILE), lambda i: (0, i)),
        pl.BlockSpec((MQK_WIDTH, ROW_TILE), lambda i: (0, i)),
        pl.BlockSpec((ROW_TILE, MQK_WIDTH), row_all),
        pl.BlockSpec((M_WIDTH, ROW_TILE), lambda i: (0, i)),
        pl.BlockSpec((ROW_TILE, M_WIDTH), row_x),
        pl.BlockSpec((ROW_TILE, N_GATES), row_all),
    ]
    in_specs = [
        pl.BlockSpec((ROW_TILE, d), row_x),
        pl.BlockSpec((ROW_TILE, d), full),
        pl.BlockSpec(mod.shape, full),
        pl.BlockSpec((1, d), full),
        pl.BlockSpec(w_main.shape, full),
        pl.BlockSpec(w_gate.shape, full),
        pl.BlockSpec((1, HEAD_DIM), full),
        pl.BlockSpec((1, HEAD_DIM), full),
        pl.BlockSpec((ROW_TILE, HEAD_DIM), row_all),
        pl.BlockSpec((ROW_TILE, HEAD_DIM), row_all),
    ]
    return pl.pallas_call(
        _inproj_kernel,
        grid=(steps,),
        in_specs=in_specs,
        out_specs=out_specs,
        out_shape=out_shape,
        compiler_params=_cparams(("arbitrary",)),
        name="in_proj",
    )(x, ctx, mod, norm1, w_main, w_gate, q_gain, k_gain, cos_t, sin_t)


def _attn_kernel(q_ref, k_ref, vt_ref, o_ref, qt_scr, acc_scr, *, tk):
    tq = q_ref.shape[0]
    n_chunks = k_ref.shape[0] // tk
    for g in range(Q_PER_KV):
        qt_scr[g] = q_ref[:, g * HEAD_DIM:(g + 1) * HEAD_DIM].astype(F32).T.astype(BF16)
    m = [jnp.full((1, tq), -jnp.inf, F32) for _ in range(Q_PER_KV)]
    l = [jnp.zeros((1, tq), F32) for _ in range(Q_PER_KV)]
    units = [(c, g) for c in range(n_chunks) for g in range(Q_PER_KV)]

    def scores(c, g):
        return jnp.dot(k_ref[c * tk:(c + 1) * tk, :], qt_scr[g], preferred_element_type=F32)

    pending = [scores(*u) for u in units[:ATTN_LOOKAHEAD]]
    for n, (c, g) in enumerate(units):
        st = pending.pop(0)
        if n + ATTN_LOOKAHEAD < len(units):
            pending.append(scores(*units[n + ATTN_LOOKAHEAD]))
        m_new = jnp.maximum(m[g], jnp.max(st, axis=0, keepdims=True))
        p = jnp.exp2(st - m_new)
        alpha = jnp.exp2(m[g] - m_new)
        l[g] = alpha * l[g] + jnp.sum(p, axis=0, keepdims=True)
        pv = jnp.dot(vt_ref[:, c * tk:(c + 1) * tk], p.astype(BF16), preferred_element_type=F32)
        if c == 0:
            acc_scr[g] = pv
        else:
            acc_scr[g] = alpha * acc_scr[g] + pv
        m[g] = m_new
    for g in range(Q_PER_KV):
        out = (acc_scr[g] / l[g]).T
        o_ref[:, g * HEAD_DIM:(g + 1) * HEAD_DIM] = out.astype(o_ref.dtype)


def _attention(q, k, vt):
    t = q.shape[0]
    s_len = k.shape[0]
    tq = ATTN_TQ
    tk = _largest_divisor(s_len, ATTN_TK_CAP, MXU_WIDTH)
    group_w = Q_PER_KV * HEAD_DIM
    return pl.pallas_call(
        functools.partial(_attn_kernel, tk=tk),
        grid=(N_KV_HEADS, t // tq),
        in_specs=[
            pl.BlockSpec((tq, group_w), lambda h, i: (i, h)),
            pl.BlockSpec((s_len, HEAD_DIM), lambda h, i: (0, h), pipeline_mode=pl.Buffered(1)),
            pl.BlockSpec((HEAD_DIM, s_len), lambda h, i: (h, 0), pipeline_mode=pl.Buffered(1)),
        ],
        out_specs=pl.BlockSpec((tq, group_w), lambda h, i: (i, h)),
        out_shape=jax.ShapeDtypeStruct((t, ATT_WIDTH), BF16),
        scratch_shapes=[
            pltpu.VMEM((Q_PER_KV, HEAD_DIM, tq), BF16),
            pltpu.VMEM((Q_PER_KV, HEAD_DIM, tq), F32),
        ],
        compiler_params=_cparams(("arbitrary", "arbitrary")),
        name="attention",
    )(q, k, vt)


def _log_sigmoid(x):
    return jnp.minimum(x, 0.0) - jnp.log1p(jnp.exp(-jnp.abs(x)))


def _scan(x, axis, op, ident, reverse):
    n = x.shape[axis]
    idx = lax.broadcasted_iota(jnp.int32, x.shape, axis)
    shift = 1
    while shift < n:
        if reverse:
            moved = pltpu.roll(x, n - shift, axis=axis)
            ok = idx < n - shift
        else:
            moved = pltpu.roll(x, shift, axis=axis)
            ok = idx >= shift
        x = op(x, jnp.where(ok, moved, ident))
        shift *= 2
    return x


def _mlstm_direction(reverse, k_ref, qt_ref, vt_ref, gi_ref, gf_ref, git_ref, gft_ref, brow_ref, bcol_ref,
                     h_ref, ct_scr, mrow_scr, mcol_scr, visible):
    L = k_ref.shape[0]
    add = lambda a, b: a + b
    i_c = gi_ref[...] + brow_ref[0:1, :]
    ls_c = _log_sigmoid(gf_ref[...] + brow_ref[1:2, :])
    a_c = i_c - _scan(ls_c, 0, add, 0.0, reverse)
    m_prev_r = mrow_scr[0:1, 0:N_UNITS]
    mm_r = jnp.maximum(m_prev_r, jnp.max(a_c, axis=0, keepdims=True))
    kscale_c = jnp.exp(a_c - mm_r)
    w_prev_r = jnp.exp(m_prev_r - mm_r)
    mrow_scr[0:1, 0:N_UNITS] = jnp.sum(ls_c, axis=0, keepdims=True) + mm_r
    i_r = git_ref[...] + bcol_ref[:, 0:1]
    ls_r = _log_sigmoid(gft_ref[...] + bcol_ref[:, 1:2])
    cum_r = _scan(ls_r, 1, add, 0.0, reverse)
    a_r = i_r - cum_r
    m_prev_c = mcol_scr[0:N_UNITS, 0:1]
    m_t = cum_r + jnp.maximum(_scan(a_r, 1, jnp.maximum, -jnp.inf, reverse), m_prev_c)
    r_r = cum_r - m_t
    w_inter_r = jnp.exp(cum_r + m_prev_c - m_t)
    floor_r = jnp.exp(-m_t)
    mcol_scr[0:N_UNITS, 0:1] = (jnp.sum(ls_r, axis=1, keepdims=True)
                                + jnp.maximum(m_prev_c, jnp.max(a_r, axis=1, keepdims=True)))

    d0 = M_HEADS if reverse else 0
    early = []
    for hd in range(M_HEADS):
        u = d0 + hd
        k = k_ref[:, hd * M_QK:(hd + 1) * M_QK]
        qt = qt_ref[hd * M_QK:(hd + 1) * M_QK, :]
        vt = vt_ref[hd * M_V:(hd + 1) * M_V, :]
        state = ct_scr[u]
        qk_t = jnp.dot(k, qt, preferred_element_type=F32)
        inter_t = jnp.dot(state.astype(BF16), qt, preferred_element_type=F32)
        kw = k.astype(F32) * kscale_c[:, u:u + 1]
        w_prev = w_prev_r[0:1, u:u + 1]
        ct_scr[u, 0:M_V, :] = w_prev * state[0:M_V] + jnp.dot(vt, kw.astype(BF16), preferred_element_type=F32)
        ct_scr[u, M_V:M_V + 1, :] = w_prev * state[M_V:M_V + 1] + jnp.sum(kw, axis=0, keepdims=True)
        early.append((qk_t, inter_t))
    for hd in range(M_HEADS):
        u = d0 + hd
        qk_t, inter_t = early[hd]
        vt = vt_ref[hd * M_V:(hd + 1) * M_V, :]
        decay_t = jnp.exp(jnp.where(visible, a_c[:, u:u + 1], -jnp.inf) + r_r[u:u + 1, :])
        s_t = qk_t * decay_t
        w_inter = w_inter_r[u:u + 1, :]
        num_t = inter_t[0:M_V] * w_inter + jnp.dot(vt, s_t.astype(BF16), preferred_element_type=F32)
        den = inter_t[M_V:M_V + 1] * w_inter + jnp.sum(s_t, axis=0, keepdims=True)
        h_ref[hd * M_V:(hd + 1) * M_V, :] = num_t * (1.0 / jnp.maximum(jnp.abs(den), floor_r[u:u + 1, :]))


def _mlstm_kernel(kf_ref, qtf_ref, vtf_ref, gif_ref, gff_ref, gitf_ref, gftf_ref,
                  kb_ref, qtb_ref, vtb_ref, gib_ref, gfb_ref, gitb_ref, gftb_ref,
                  brow_ref, bcol_ref, hf_ref, hb_ref, ct_scr, mrow_scr, mcol_scr):
    L = kf_ref.shape[0]

    @pl.when(pl.program_id(0) == 0)
    def _():
        ct_scr[...] = jnp.zeros(ct_scr.shape, F32)
        mrow_scr[...] = jnp.zeros(mrow_scr.shape, F32)
        mcol_scr[...] = jnp.zeros(mcol_scr.shape, F32)

    src = lax.broadcasted_iota(jnp.int32, (L, L), 0)
    tgt = lax.broadcasted_iota(jnp.int32, (L, L), 1)
    _mlstm_direction(False, kf_ref, qtf_ref, vtf_ref, gif_ref, gff_ref, gitf_ref, gftf_ref, brow_ref, bcol_ref,
                     hf_ref, ct_scr, mrow_scr.at[0], mcol_scr.at[0], src <= tgt)
    _mlstm_direction(True, kb_ref, qtb_ref, vtb_ref, gib_ref, gfb_ref, gitb_ref, gftb_ref, brow_ref, bcol_ref,
                     hb_ref, ct_scr, mrow_scr.at[1], mcol_scr.at[1], src >= tgt)


def _mlstm(mk, mq_t, mv_t, gates, b_i, b_f, n_x):
    L = MLSTM_CHUNK
    rows = mk.shape[0]
    assert rows % L == 0 and n_x % L == 0 and rows - n_x == L
    nx = n_x // L
    steps = nx + 1
    g4 = gates.reshape(rows, N_DIR, 2, M_HEADS)
    g_in = g4[:, :, 0, :].reshape(rows, N_UNITS)
    g_fg = g4[:, :, 1, :].reshape(rows, N_UNITS)
    bias_row = jnp.stack([b_i.reshape(N_UNITS), b_f.reshape(N_UNITS)], axis=0)
    f_chunk = lambda g: g
    b_chunk = lambda g: jnp.where(g == 0, 0, nx + 1 - g)
    f_out = lambda g: (0, jnp.maximum(g - 1, 0))
    b_out = lambda g: (0, jnp.where(g == 0, nx - 1, nx - g))
    small = lambda g: (0, 0)

    def stream(chunk):
        return [
            pl.BlockSpec((L, MQK_WIDTH), lambda g: (chunk(g), 0)),
            pl.BlockSpec((MQK_WIDTH, L), lambda g: (0, chunk(g))),
            pl.BlockSpec((M_WIDTH, L), lambda g: (0, chunk(g))),
            pl.BlockSpec((L, N_UNITS), lambda g: (chunk(g), 0)),
            pl.BlockSpec((L, N_UNITS), lambda g: (chunk(g), 0)),
            pl.BlockSpec((N_UNITS, L), lambda g: (0, chunk(g))),
            pl.BlockSpec((N_UNITS, L), lambda g: (0, chunk(g))),
        ]

    operands = (mk, mq_t, mv_t, g_in, g_fg, g_in.T, g_fg.T)
    return pl.pallas_call(
        _mlstm_kernel,
        grid=(steps,),
        in_specs=stream(f_chunk) + stream(b_chunk) + [pl.BlockSpec((2, N_UNITS), small),
                                                       pl.BlockSpec((N_UNITS, 2), small)],
        out_specs=[pl.BlockSpec((M_WIDTH, L), f_out), pl.BlockSpec((M_WIDTH, L), b_out)],
        out_shape=[jax.ShapeDtypeStruct((M_WIDTH, n_x), F32)] * 2,
        scratch_shapes=[
            pltpu.VMEM((N_UNITS, M_V + BF16_SUBLANES, M_QK), F32),
            pltpu.VMEM((N_DIR, 8, LANES), F32),
            pltpu.VMEM((N_DIR, 8, LANES), F32),
        ],
        compiler_params=_cparams(("arbitrary",)),
        name="mlstm",
    )(*operands, *operands, bias_row, bias_row.T)


def _outproj_kernel(att_ref, hft_ref, hbt_ref, mo_ref, mg_ref, w_ref, x_ref, mod_ref, n2_ref,
                    x1_ref, h2_ref):
    d = x_ref.shape[1]
    y = jnp.dot(att_ref[...], w_ref[0:ATT_WIDTH, :], preferred_element_type=F32)
    for hd in range(M_HEADS):
        c = hd * M_V
        ht = hft_ref[c:c + M_V, :] + hbt_ref[c:c + M_V, :]
        ht = ht * lax.rsqrt(jnp.mean(ht * ht, axis=0, keepdims=True) + NORM_EPS) * mg_ref[c:c + M_V, :]
        r = (ht.T * mo_ref[:, c:c + M_V].astype(F32)).astype(BF16)
        y = y + jnp.dot(r, w_ref[ATT_WIDTH + c:ATT_WIDTH + c + M_V, :], preferred_element_type=F32)
    x1 = x_ref[...] + mod_ref[0:1, 2 * d:3 * d] * y
    x1_ref[...] = x1
    h2 = _rms(x1) * n2_ref[...] * (1.0 + mod_ref[0:1, 4 * d:5 * d]) + mod_ref[0:1, 3 * d:4 * d]
    h2_ref[...] = h2.astype(BF16)


def _out_proj(att, hf_t, hb_t, mo, m_gain, w_out, x, mod, norm2):
    t, d = x.shape
    tm = OUT_TM
    row = lambda i: (i, 0)
    full = lambda i: (0, 0)
    return pl.pallas_call(
        _outproj_kernel,
        grid=(t // tm,),
        in_specs=[
            pl.BlockSpec((tm, ATT_WIDTH), row),
            pl.BlockSpec((M_WIDTH, tm), lambda i: (0, i)),
            pl.BlockSpec((M_WIDTH, tm), lambda i: (0, i)),
            pl.BlockSpec((tm, M_WIDTH), row),
            pl.BlockSpec((M_WIDTH, 1), full),
            pl.BlockSpec(w_out.shape, full),
            pl.BlockSpec((tm, d), row),
            pl.BlockSpec(mod.shape, full),
            pl.BlockSpec((1, d), full),
        ],
        out_specs=[pl.BlockSpec((tm, d), row), pl.BlockSpec((tm, d), row)],
        out_shape=[jax.ShapeDtypeStruct((t, d), F32), jax.ShapeDtypeStruct((t, d), BF16)],
        compiler_params=_cparams(("arbitrary",)),
        name="out_proj",
    )(att, hf_t, hb_t, mo, m_gain, w_out, x, mod, norm2)


def _ffn_kernel(h_ref, hp_ref, hn_ref, wg_ref, wv_ref, cwg_ref, cwv_ref, cbg_ref, cbv_ref, wd_ref,
                x1_ref, mod_ref, nf_ref, o_ref, hext_ref, ug_ref, uv_ref):
    i = pl.program_id(0)
    j = pl.program_id(1)
    tm = h_ref.shape[0]
    d = x1_ref.shape[1]

    @pl.when(j == 0)
    def _():
        hext_ref[0:HALO, :] = jnp.where(i == 0, jnp.zeros_like(hp_ref[...]), hp_ref[...])
        hext_ref[HALO:HALO + tm, :] = h_ref[...]
        hext_ref[HALO + tm:, :] = jnp.where(i == pl.num_programs(0) - 1, jnp.zeros_like(hn_ref[...]), hn_ref[...])
        o_ref[...] = jnp.zeros(o_ref.shape, F32)

    hext = hext_ref[...]
    ug_ref[...] = jnp.dot(hext, wg_ref[...], preferred_element_type=F32)
    uv_ref[...] = jnp.dot(hext, wv_ref[...], preferred_element_type=F32)

    def conv(u_ref, cw_ref, cb_ref):
        return (cw_ref[0:1, :] * u_ref[HALO - 1:HALO - 1 + tm, :]
                + cw_ref[1:2, :] * u_ref[HALO:HALO + tm, :]
                + cw_ref[2:3, :] * u_ref[HALO + 1:HALO + 1 + tm, :]
                + cb_ref[...])

    g = conv(ug_ref, cwg_ref, cbg_ref)
    val = conv(uv_ref, cwv_ref, cbv_ref)
    a = (g * jax.nn.sigmoid(g) * val).astype(BF16)
    o_ref[...] += jnp.dot(a, wd_ref[...], preferred_element_type=F32)

    @pl.when(j == pl.num_programs(1) - 1)
    def _():
        y = x1_ref[...] + mod_ref[0:1, 5 * d:6 * d] * o_ref[...]
        o_ref[...] = _rms(y) * nf_ref[...]


def _conv_ffn(h2, w_up, conv_w, conv_b, w_down, x1, mod, norm_f):
    t, d = h2.shape
    d_ff = w_down.shape[0]
    tm, tf = FFN_TM, FFN_TF
    nf = d_ff // tf
    hb = tm // HALO
    last_halo = t // HALO - 1
    in_specs = [
        pl.BlockSpec((tm, d), lambda i, j: (i, 0)),
        pl.BlockSpec((HALO, d), lambda i, j: (jnp.maximum(i * hb - 1, 0), 0)),
        pl.BlockSpec((HALO, d), lambda i, j: (jnp.minimum((i + 1) * hb, last_halo), 0)),
        pl.BlockSpec((d, tf), lambda i, j: (0, j)),
        pl.BlockSpec((d, tf), lambda i, j: (0, j + nf)),
        pl.BlockSpec((3, tf), lambda i, j: (0, j)),
        pl.BlockSpec((3, tf), lambda i, j: (0, j + nf)),
        pl.BlockSpec((1, tf), lambda i, j: (0, j)),
        pl.BlockSpec((1, tf), lambda i, j: (0, j + nf)),
        pl.BlockSpec((tf, d), lambda i, j: (j, 0)),
        pl.BlockSpec((tm, d), lambda i, j: (i, 0)),
        pl.BlockSpec(mod.shape, lambda i, j: (0, 0)),
        pl.BlockSpec((1, d), lambda i, j: (0, 0)),
    ]
    return pl.pallas_call(
        _ffn_kernel,
        grid=(t // tm, nf),
        in_specs=in_specs,
        out_specs=pl.BlockSpec((tm, d), lambda i, j: (i, 0)),
        out_shape=jax.ShapeDtypeStruct((t, d), F32),
        scratch_shapes=[
            pltpu.VMEM((tm + 2 * HALO, d), BF16),
            pltpu.VMEM((tm + 2 * HALO, tf), F32),
            pltpu.VMEM((tm + 2 * HALO, tf), F32),
        ],
        compiler_params=_cparams(("arbitrary", "arbitrary")),
        name="conv_ffn",
    )(h2, h2, h2, w_up, w_up, conv_w, conv_w, conv_b, conv_b, w_down, x1, mod, norm_f)


def _rope_tables(n_ctx, n_tok):
    f32 = np.float32
    rows = n_tok // GRID_W
    row = np.repeat(np.arange(rows, dtype=f32), GRID_W)
    col = np.tile(np.arange(GRID_W, dtype=f32), rows)
    inv_freq = np.power(f32(ROPE_THETA), -np.arange(0, AXIS_DIM, 2, dtype=f32) / f32(AXIS_DIM)).astype(f32)
    ang_r = row[:, None] * inv_freq[None, :]
    ang_c = col[:, None] * inv_freq[None, :]
    cos = np.concatenate([np.cos(ang_r), np.cos(ang_r), np.cos(ang_c), np.cos(ang_c)], axis=1)
    sin = np.concatenate([-np.sin(ang_r), np.sin(ang_r), -np.sin(ang_c), np.sin(ang_c)], axis=1)
    cos = np.concatenate([np.ones((n_ctx, HEAD_DIM), f32), cos], axis=0).astype(f32)
    sin = np.concatenate([np.zeros((n_ctx, HEAD_DIM), f32), sin], axis=0).astype(f32)
    return jnp.asarray(cos), jnp.asarray(sin)


def kernel(x, c, ctx, c_ctx, w_mod, b_mod, norm1, w_in, q_norm, k_norm, b_igate, b_fgate, m_norm,
           w_out, norm2, w_up, conv_w, conv_b, w_down, norm_f):
    batch, n_tok, d = x.shape
    assert batch == 1 and w_mod.shape[0] == 1
    n_ctx = ctx.shape[1]
    x2 = x[0]
    ctx2 = ctx[0]

    cs = jnp.stack([c[0], c_ctx], axis=1)
    mod = _modulation(cs, w_mod[0], b_mod[0])

    w_in_t = w_in[0].T
    w_main = _cast_main_columns(w_in_t, OFF_G)
    w_gate = _gate_weights(w_in_t, OFF_G)
    cos_t, sin_t = _rope_tables(n_ctx, n_tok)
    q_gain = (q_norm[0] * (HEAD_DIM ** -0.5 * LOG2E)).reshape(1, HEAD_DIM)
    k_gain = k_norm[0].reshape(1, HEAD_DIM)
    q, k, vt, mq_t, mk, mv_t, mo, gates = _in_proj(
        x2, ctx2, mod, norm1[0].reshape(1, d), w_main, w_gate, q_gain, k_gain, cos_t, sin_t)

    att = _attention(q, k, vt)

    hf_t, hb_t = _mlstm(mk, mq_t, mv_t, gates, b_igate[0], b_fgate[0], n_tok)

    x1, h2 = _out_proj(att, hf_t, hb_t, mo, m_norm[0].reshape(M_WIDTH, 1), w_out[0].astype(BF16), x2, mod,
                       norm2[0].reshape(1, d))

    out = _conv_ffn(h2, w_up[0].astype(BF16), conv_w[0], conv_b[0].reshape(1, -1), w_down[0].astype(BF16),
                    x1, mod, norm_f.reshape(1, d))
    return out[None]
```

```python
import functools
import math

import jax
import jax.numpy as jnp
import numpy as np
from jax import lax
from jax.experimental import pallas as pl
from jax.experimental.pallas import tpu as pltpu

F32 = jnp.float32
BF16 = jnp.bfloat16

GRID_W = 64
HEAD_DIM = 128
N_Q_HEADS = 8
N_KV_HEADS = 2
Q_PER_KV = N_Q_HEADS // N_KV_HEADS
AXIS_DIM = HEAD_DIM // 2
ROPE_THETA = 10000.0
M_HEADS = 4
M_V = 256
M_QK = 128
N_DIR = 2
NORM_EPS = 1e-6
ATT_WIDTH = N_Q_HEADS * HEAD_DIM
KV_WIDTH = N_KV_HEADS * HEAD_DIM
M_WIDTH = M_HEADS * M_V
MQK_WIDTH = M_HEADS * M_QK
N_GATES = N_DIR * 2 * M_HEADS
N_UNITS = N_DIR * M_HEADS

OFF_AQ = 0
OFF_AK = OFF_AQ + ATT_WIDTH
OFF_AV = OFF_AK + KV_WIDTH
OFF_MQ = OFF_AV + KV_WIDTH
OFF_MK = OFF_MQ + MQK_WIDTH
OFF_MV = OFF_MK + MQK_WIDTH
OFF_MO = OFF_MV + M_WIDTH
OFF_G = OFF_MO + M_WIDTH

LANES = 128
MXU_WIDTH = 256
BF16_SUBLANES = 16
VMEM_LIMIT = 50 * 1024 * 1024

ROW_TILE = 256
MLSTM_CHUNK = 256
ATTN_TQ = 512
ATTN_UNIT_Q = 256
ATTN_TK_CAP = 768
ATTN_LOOKAHEAD = 4
OUT_TM = 512
FFN_TM = 512
FFN_TF = 512
F32_SUBLANES = 8
HALO = BF16_SUBLANES
PREV_ROW = F32_SUBLANES - 1
NEXT_ROW = F32_SUBLANES
MOD_TK = 128
WCAST_TN = 512
LOG2E = 1.4426950408889634


def _largest_divisor(n, cap, mult):
    best = None
    for d in range(mult, cap + 1, mult):
        if n % d == 0:
            best = d
    assert best is not None, (n, cap, mult)
    return best


def _cparams(sem, vmem=VMEM_LIMIT):
    return pltpu.CompilerParams(dimension_semantics=sem, vmem_limit_bytes=vmem)


def _rms(x, eps=NORM_EPS):
    return x * lax.rsqrt(jnp.mean(x * x, axis=-1, keepdims=True) + eps)


def _mod_kernel(cs_ref, w_ref, b_ref, o_ref):
    @pl.when(pl.program_id(0) == 0)
    def _():
        o_ref[0:1, :] = b_ref[...]
        o_ref[1:2, :] = b_ref[...]

    cs = cs_ref[...]
    s = cs * jax.nn.sigmoid(cs)
    w = w_ref[...]
    o_ref[0:1, :] += jnp.sum(s[:, 0:1] * w, axis=0, keepdims=True)
    o_ref[1:2, :] += jnp.sum(s[:, 1:2] * w, axis=0, keepdims=True)


def _modulation(cs, w_mod, b_mod):
    d, n = w_mod.shape
    tk = MOD_TK
    return pl.pallas_call(
        _mod_kernel,
        grid=(d // tk,),
        in_specs=[
            pl.BlockSpec((tk, 2), lambda j: (j, 0)),
            pl.BlockSpec((tk, n), lambda j: (j, 0)),
            pl.BlockSpec((1, n), lambda j: (0, 0)),
        ],
        out_specs=pl.BlockSpec((2, n), lambda j: (0, 0)),
        out_shape=jax.ShapeDtypeStruct((2, n), F32),
        compiler_params=_cparams(("arbitrary",)),
        name="modulation",
    )(cs, w_mod, b_mod.reshape(1, n))


def _gate_weight_kernel(wt_ref, o_ref):
    g = wt_ref[...]
    hi = g.astype(BF16)
    lo = (g - hi.astype(F32)).astype(BF16)
    pad = jnp.zeros((o_ref.shape[0] - 2 * N_GATES, g.shape[1]), BF16)
    o_ref[...] = jnp.concatenate([hi, lo, pad], axis=0)


def _gate_weights(w_t, row0):
    n, d = w_t.shape
    assert row0 % N_GATES == 0 and row0 + N_GATES == n
    return pl.pallas_call(
        _gate_weight_kernel,
        grid=(1,),
        in_specs=[pl.BlockSpec((N_GATES, d), lambda i: (row0 // N_GATES, 0))],
        out_specs=pl.BlockSpec((LANES, d), lambda i: (0, 0)),
        out_shape=jax.ShapeDtypeStruct((LANES, d), BF16),
        compiler_params=_cparams(("arbitrary",)),
        name="w_gate_split",
    )(w_t)


def _wcast_kernel(wt_ref, o_ref):
    o_ref[...] = wt_ref[...].T.astype(BF16)


def _cast_main_columns(w_t, n_main):
    n, d = w_t.shape
    tn = WCAST_TN
    assert n_main % tn == 0 and n_main <= n
    return pl.pallas_call(
        _wcast_kernel,
        grid=(n_main // tn,),
        in_specs=[pl.BlockSpec((tn, d), lambda j: (j, 0))],
        out_specs=pl.BlockSpec((d, tn), lambda j: (0, j)),
        out_shape=jax.ShapeDtypeStruct((d, n_main), BF16),
        compiler_params=_cparams(("arbitrary",)),
        name="w_in_cast",
    )(w_t)


def _swap_rope_halves(y):
    lane = lax.broadcasted_iota(jnp.int32, y.shape, 1)
    fwd = pltpu.roll(y, LANES - AXIS_DIM // 2, axis=1)
    bwd = pltpu.roll(y, AXIS_DIM // 2, axis=1)
    return jnp.where((lane & (AXIS_DIM // 2)) == 0, fwd, bwd)


def _inproj_kernel(x_ref, ctx_ref, mod_ref, n1_ref, w_ref, wg_ref, qg_ref, kg_ref, cos_ref, sin_ref,
                   q_ref, k_ref, vt_ref, mqt_ref, mk_ref, mvt_ref, mo_ref, g_ref):
    d = x_ref.shape[1]
    is_ctx = pl.program_id(0) == 0
    xin = jnp.where(is_ctx, ctx_ref[...], x_ref[...])
    shift = jnp.where(is_ctx, mod_ref[1:2, 0:d], mod_ref[0:1, 0:d])
    scale = jnp.where(is_ctx, mod_ref[1:2, d:2 * d], mod_ref[0:1, d:2 * d])
    hf = _rms(xin) * n1_ref[...] * (1.0 + scale) + shift
    h_hi = hf.astype(BF16)
    h_lo = (hf - h_hi.astype(F32)).astype(BF16)

    def proj(c0, width):
        return jnp.dot(h_hi, w_ref[:, c0:c0 + width], preferred_element_type=F32)

    cos = cos_ref[...]
    sin = sin_ref[...]

    def norm_rope(y, gain):
        yn = _rms(y) * gain
        return yn * cos + _swap_rope_halves(yn) * sin

    for j in range(ATT_WIDTH // MXU_WIDTH):
        y = proj(OFF_AQ + j * MXU_WIDTH, MXU_WIDTH)
        for t in range(MXU_WIDTH // HEAD_DIM):
            c = j * MXU_WIDTH + t * HEAD_DIM
            q_ref[:, c:c + HEAD_DIM] = norm_rope(y[:, t * HEAD_DIM:(t + 1) * HEAD_DIM], qg_ref[...]).astype(BF16)
    y = proj(OFF_AK, KV_WIDTH)
    for t in range(N_KV_HEADS):
        k_ref[:, t * HEAD_DIM:(t + 1) * HEAD_DIM] = norm_rope(
            y[:, t * HEAD_DIM:(t + 1) * HEAD_DIM], kg_ref[...]).astype(BF16)
    vt_ref[...] = proj(OFF_AV, KV_WIDTH).T.astype(BF16)
    for j in range(MQK_WIDTH // MXU_WIDTH):
        c = j * MXU_WIDTH
        mqt_ref[c:c + MXU_WIDTH, :] = proj(OFF_MQ + c, MXU_WIDTH).T.astype(BF16)
        mk_ref[:, c:c + MXU_WIDTH] = (proj(OFF_MK + c, MXU_WIDTH) * (M_QK ** -0.5)).astype(BF16)
    for j in range(M_WIDTH // MXU_WIDTH):
        c = j * MXU_WIDTH
        mvt_ref[c:c + MXU_WIDTH, :] = proj(OFF_MV + c, MXU_WIDTH).T.astype(BF16)
        mo_ref[:, c:c + MXU_WIDTH] = jax.nn.sigmoid(proj(OFF_MO + c, MXU_WIDTH)).astype(BF16)
    nt = (((1,), (1,)), ((), ()))
    gg = (lax.dot_general(h_hi, wg_ref[...], nt, preferred_element_type=F32)
          + lax.dot_general(h_lo, wg_ref[...], nt, preferred_element_type=F32))
    gg = gg + pltpu.roll(gg, LANES - N_GATES, axis=1)
    g_ref[...] = gg[:, 0:N_GATES]


def _in_proj(x, ctx, mod, norm1, w_main, w_gate, q_gain, k_gain, cos_t, sin_t):
    t, d = x.shape
    n_ctx = ctx.shape[0]
    assert n_ctx == ROW_TILE and t % ROW_TILE == 0
    rows = n_ctx + t
    steps = rows // ROW_TILE
    full = lambda i: (0, 0)
    row_all = lambda i: (i, 0)
    row_x = lambda i: (jnp.maximum(i - 1, 0), 0)
    out_shape = [
        jax.ShapeDtypeStruct((t, ATT_WIDTH), BF16),
        jax.ShapeDtypeStruct((rows, KV_WIDTH), BF16),
        jax.ShapeDtypeStruct((KV_WIDTH, rows), BF16),
        jax.ShapeDtypeStruct((MQK_WIDTH, rows), BF16),
        jax.ShapeDtypeStruct((rows, MQK_WIDTH), BF16),
        jax.ShapeDtypeStruct((M_WIDTH, rows), BF16),
        jax.ShapeDtypeStruct((t, M_WIDTH), BF16),
        jax.ShapeDtypeStruct((rows, N_GATES), F32),
    ]
    out_specs = [
        pl.BlockSpec((ROW_TILE, ATT_WIDTH), row_x),
        pl.BlockSpec((ROW_TILE, KV_WIDTH), row_all),
        pl.BlockSpec((KV_WIDTH, ROW_TILE), lambda i: (0, i)),
        pl.BlockSpec((MQK_WIDTH, ROW_TILE), lambda i: (0, i)),
        pl.BlockSpec((ROW_TILE, MQK_WIDTH), row_all),
        pl.BlockSpec((M_WIDTH, ROW_TILE), lambda i: (0, i)),
        pl.BlockSpec((ROW_TILE, M_WIDTH), row_x),
        pl.BlockSpec((ROW_TILE, N_GATES), row_all),
    ]
    in_specs = [
        pl.BlockSpec((ROW_TILE, d), row_x),
        pl.BlockSpec((ROW_TILE, d), full),
        pl.BlockSpec(mod.shape, full),
        pl.BlockSpec((1, d), full),
        pl.BlockSpec(w_main.shape, full),
        pl.BlockSpec(w_gate.shape, full),
        pl.BlockSpec((1, HEAD_DIM), full),
        pl.BlockSpec((1, HEAD_DIM), full),
        pl.BlockSpec((ROW_TILE, HEAD_DIM), row_all),
        pl.BlockSpec((ROW_TILE, HEAD_DIM), row_all),
    ]
    return pl.pallas_call(
        _inproj_kernel,
        grid=(steps,),
        in_specs=in_specs,
        out_specs=out_specs,
        out_shape=out_shape,
        compiler_params=_cparams(("arbitrary",)),
        name="in_proj",
    )(x, ctx, mod, norm1, w_main, w_gate, q_gain, k_gain, cos_t, sin_t)


def _attn_kernel(q_ref, k_ref, vt_ref, o_ref, qt_scr, acc_scr, *, tk):
    uq = ATTN_UNIT_Q
    n_sub = q_ref.shape[0] // uq
    n_chunks = k_ref.shape[0] // tk
    streams = [(r, h) for r in range(n_sub) for h in range(Q_PER_KV)]
    for g, (r, h) in enumerate(streams):
        qt_scr[g] = q_ref[r * uq:(r + 1) * uq, h * HEAD_DIM:(h + 1) * HEAD_DIM].astype(F32).T.astype(BF16)
    m = [jnp.full((1, uq), -jnp.inf, F32) for _ in streams]
    l = [jnp.zeros((1, uq), F32) for _ in streams]
    units = [(c, g) for c in range(n_chunks) for g in range(len(streams))]

    def scores(c, g):
        return jnp.dot(k_ref[c * tk:(c + 1) * tk, :], qt_scr[g], preferred_element_type=F32)

    pending = [scores(*u) for u in units[:ATTN_LOOKAHEAD]]
    for n, (c, g) in enumerate(units):
        st = pending.pop(0)
        if n + ATTN_LOOKAHEAD < len(units):
            pending.append(scores(*units[n + ATTN_LOOKAHEAD]))
        m_new = jnp.maximum(m[g], jnp.max(st, axis=0, keepdims=True))
        p = jnp.exp2(st - m_new)
        alpha = jnp.exp2(m[g] - m_new)
        l[g] = alpha * l[g] + jnp.sum(p, axis=0, keepdims=True)
        pv = jnp.dot(vt_ref[:, c * tk:(c + 1) * tk], p.astype(BF16), preferred_element_type=F32)
        if c == 0:
            acc_scr[g] = pv
        else:
            acc_scr[g] = alpha * acc_scr[g] + pv
        m[g] = m_new
    for g, (r, h) in enumerate(streams):
        out = (acc_scr[g] / l[g]).T
        o_ref[r * uq:(r + 1) * uq, h * HEAD_DIM:(h + 1) * HEAD_DIM] = out.astype(o_ref.dtype)


def _attention(q, k, vt):
    t = q.shape[0]
    s_len = k.shape[0]
    tq = ATTN_TQ
    tk = _largest_divisor(s_len, ATTN_TK_CAP, MXU_WIDTH)
    group_w = Q_PER_KV * HEAD_DIM
    return pl.pallas_call(
        functools.partial(_attn_kernel, tk=tk),
        grid=(N_KV_HEADS, t // tq),
        in_specs=[
            pl.BlockSpec((tq, group_w), lambda h, i: (i, h)),
            pl.BlockSpec((s_len, HEAD_DIM), lambda h, i: (0, h), pipeline_mode=pl.Buffered(1)),
            pl.BlockSpec((HEAD_DIM, s_len), lambda h, i: (h, 0), pipeline_mode=pl.Buffered(1)),
        ],
        out_specs=pl.BlockSpec((tq, group_w), lambda h, i: (i, h)),
        out_shape=jax.ShapeDtypeStruct((t, ATT_WIDTH), BF16),
        scratch_shapes=[
            pltpu.VMEM((Q_PER_KV * tq // ATTN_UNIT_Q, HEAD_DIM, ATTN_UNIT_Q), BF16),
            pltpu.VMEM((Q_PER_KV * tq // ATTN_UNIT_Q, HEAD_DIM, ATTN_UNIT_Q), F32),
        ],
        compiler_params=_cparams(("arbitrary", "arbitrary")),
        name="attention",
    )(q, k, vt)


def _log_sigmoid(x):
    return jnp.minimum(x, 0.0) - jnp.log1p(jnp.exp(-jnp.abs(x)))


def _scan(x, axis, op, ident, reverse):
    n = x.shape[axis]
    idx = lax.broadcasted_iota(jnp.int32, x.shape, axis)
    shift = 1
    while shift < n:
        if reverse:
            moved = pltpu.roll(x, n - shift, axis=axis)
            ok = idx < n - shift
        else:
            moved = pltpu.roll(x, shift, axis=axis)
            ok = idx >= shift
        x = op(x, jnp.where(ok, moved, ident))
        shift *= 2
    return x


def _mlstm_direction(reverse, k_ref, qt_ref, vt_ref, gi_ref, gf_ref, git_ref, gft_ref, brow_ref, bcol_ref,
                     h_ref, ct_scr, mrow_scr, mcol_scr, visible):
    L = k_ref.shape[0]
    add = lambda a, b: a + b
    i_c = gi_ref[...] + brow_ref[0:1, :]
    ls_c = _log_sigmoid(gf_ref[...] + brow_ref[1:2, :])
    a_c = i_c - _scan(ls_c, 0, add, 0.0, reverse)
    m_prev_r = mrow_scr[0:1, 0:N_UNITS]
    mm_r = jnp.maximum(m_prev_r, jnp.max(a_c, axis=0, keepdims=True))
    kscale_c = jnp.exp(a_c - mm_r)
    w_prev_r = jnp.exp(m_prev_r - mm_r)
    mrow_scr[0:1, 0:N_UNITS] = jnp.sum(ls_c, axis=0, keepdims=True) + mm_r
    i_r = git_ref[...] + bcol_ref[:, 0:1]
    ls_r = _log_sigmoid(gft_ref[...] + bcol_ref[:, 1:2])
    cum_r = _scan(ls_r, 1, add, 0.0, reverse)
    a_r = i_r - cum_r
    m_prev_c = mcol_scr[0:N_UNITS, 0:1]
    m_t = cum_r + jnp.maximum(_scan(a_r, 1, jnp.maximum, -jnp.inf, reverse), m_prev_c)
    r_r = cum_r - m_t
    w_inter_r = jnp.exp(cum_r + m_prev_c - m_t)
    floor_r = jnp.exp(-m_t)
    mcol_scr[0:N_UNITS, 0:1] = (jnp.sum(ls_r, axis=1, keepdims=True)
                                + jnp.maximum(m_prev_c, jnp.max(a_r, axis=1, keepdims=True)))

    d0 = M_HEADS if reverse else 0
    early = []
    for hd in range(M_HEADS):
        u = d0 + hd
        k = k_ref[:, hd * M_QK:(hd + 1) * M_QK]
        qt = qt_ref[hd * M_QK:(hd + 1) * M_QK, :]
        vt = vt_ref[hd * M_V:(hd + 1) * M_V, :]
        state = ct_scr[u]
        qk_t = jnp.dot(k, qt, preferred_element_type=F32)
        inter_t = jnp.dot(state.astype(BF16), qt, preferred_element_type=F32)
        kw = k.astype(F32) * kscale_c[:, u:u + 1]
        w_prev = w_prev_r[0:1, u:u + 1]
        ct_scr[u, 0:M_V, :] = w_prev * state[0:M_V] + jnp.dot(vt, kw.astype(BF16), preferred_element_type=F32)
        ct_scr[u, M_V:M_V + 1, :] = w_prev * state[M_V:M_V + 1] + jnp.sum(kw, axis=0, keepdims=True)
        early.append((qk_t, inter_t))
    for hd in range(M_HEADS):
        u = d0 + hd
        qk_t, inter_t = early[hd]
        vt = vt_ref[hd * M_V:(hd + 1) * M_V, :]
        decay_t = jnp.exp(jnp.where(visible, a_c[:, u:u + 1], -jnp.inf) + r_r[u:u + 1, :])
        s_t = qk_t * decay_t
        w_inter = w_inter_r[u:u + 1, :]
        num_t = inter_t[0:M_V] * w_inter + jnp.dot(vt, s_t.astype(BF16), preferred_element_type=F32)
        den = inter_t[M_V:M_V + 1] * w_inter + jnp.sum(s_t, axis=0, keepdims=True)
        h_ref[hd * M_V:(hd + 1) * M_V, :] = num_t * (1.0 / jnp.maximum(jnp.abs(den), floor_r[u:u + 1, :]))


def _mlstm_kernel(kf_ref, qtf_ref, vtf_ref, gif_ref, gff_ref, gitf_ref, gftf_ref,
                  kb_ref, qtb_ref, vtb_ref, gib_ref, gfb_ref, gitb_ref, gftb_ref,
                  brow_ref, bcol_ref, *rest, n_cast):
    cast_in, (hf_ref, hb_ref), rest = rest[:n_cast], rest[n_cast:n_cast + 2], rest[n_cast + 2:]
    cast_out, (ct_scr, mrow_scr, mcol_scr) = rest[:n_cast], rest[n_cast:]
    L = kf_ref.shape[0]
    for w_ref, o_ref in zip(cast_in, cast_out):
        o_ref[...] = w_ref[...].astype(BF16)

    @pl.when(pl.program_id(0) == 0)
    def _():
        ct_scr[...] = jnp.zeros(ct_scr.shape, F32)
        mrow_scr[...] = jnp.zeros(mrow_scr.shape, F32)
        mcol_scr[...] = jnp.zeros(mcol_scr.shape, F32)

    src = lax.broadcasted_iota(jnp.int32, (L, L), 0)
    tgt = lax.broadcasted_iota(jnp.int32, (L, L), 1)
    _mlstm_direction(False, kf_ref, qtf_ref, vtf_ref, gif_ref, gff_ref, gitf_ref, gftf_ref, brow_ref, bcol_ref,
                     hf_ref, ct_scr, mrow_scr.at[0], mcol_scr.at[0], src <= tgt)
    _mlstm_direction(True, kb_ref, qtb_ref, vtb_ref, gib_ref, gfb_ref, gitb_ref, gftb_ref, brow_ref, bcol_ref,
                     hb_ref, ct_scr, mrow_scr.at[1], mcol_scr.at[1], src >= tgt)


def _mlstm(mk, mq_t, mv_t, gates, b_i, b_f, n_x, cast_weights):
    L = MLSTM_CHUNK
    rows = mk.shape[0]
    assert rows % L == 0 and n_x % L == 0 and rows - n_x == L
    nx = n_x // L
    steps = nx + 1
    g4 = gates.reshape(rows, N_DIR, 2, M_HEADS)
    g_in = g4[:, :, 0, :].reshape(rows, N_UNITS)
    g_fg = g4[:, :, 1, :].reshape(rows, N_UNITS)
    bias_row = jnp.stack([b_i.reshape(N_UNITS), b_f.reshape(N_UNITS)], axis=0)
    f_chunk = lambda g: g
    b_chunk = lambda g: jnp.where(g == 0, 0, nx + 1 - g)
    f_out = lambda g: (0, jnp.maximum(g - 1, 0))
    b_out = lambda g: (0, jnp.where(g == 0, nx - 1, nx - g))
    f_out_rows = lambda g: (jnp.maximum(g - 1, 0), 0)
    small = lambda g: (0, 0)

    def stream(chunk):
        return [
            pl.BlockSpec((L, MQK_WIDTH), lambda g: (chunk(g), 0)),
            pl.BlockSpec((MQK_WIDTH, L), lambda g: (0, chunk(g))),
            pl.BlockSpec((M_WIDTH, L), lambda g: (0, chunk(g))),
            pl.BlockSpec((L, N_UNITS), lambda g: (chunk(g), 0)),
            pl.BlockSpec((L, N_UNITS), lambda g: (chunk(g), 0)),
            pl.BlockSpec((N_UNITS, L), lambda g: (0, chunk(g))),
            pl.BlockSpec((N_UNITS, L), lambda g: (0, chunk(g))),
        ]

    operands = (mk, mq_t, mv_t, g_in, g_fg, g_in.T, g_fg.T)
    cast_specs, cast_shapes = [], []
    for w in cast_weights:
        assert w.shape[0] % (nx * BF16_SUBLANES) == 0, w.shape
        blk = (w.shape[0] // nx, w.shape[1])
        cast_specs.append(pl.BlockSpec(blk, f_out_rows))
        cast_shapes.append(jax.ShapeDtypeStruct(w.shape, BF16))
    outs = pl.pallas_call(
        functools.partial(_mlstm_kernel, n_cast=len(cast_weights)),
        grid=(steps,),
        in_specs=stream(f_chunk) + stream(b_chunk) + [pl.BlockSpec((2, N_UNITS), small),
                                                       pl.BlockSpec((N_UNITS, 2), small)] + cast_specs,
        out_specs=[pl.BlockSpec((M_WIDTH, L), f_out), pl.BlockSpec((M_WIDTH, L), b_out)] + cast_specs,
        out_shape=[jax.ShapeDtypeStruct((M_WIDTH, n_x), F32)] * 2 + cast_shapes,
        scratch_shapes=[
            pltpu.VMEM((N_UNITS, M_V + BF16_SUBLANES, M_QK), F32),
            pltpu.VMEM((N_DIR, 8, LANES), F32),
            pltpu.VMEM((N_DIR, 8, LANES), F32),
        ],
        compiler_params=_cparams(("arbitrary",)),
        name="mlstm",
    )(*operands, *operands, bias_row, bias_row.T, *cast_weights)
    return outs[0], outs[1], tuple(outs[2:])


def _outproj_kernel(att_ref, hft_ref, hbt_ref, mo_ref, mg_ref, w_ref, x_ref, mod_ref, n2_ref,
                    x1_ref, h2_ref):
    d = x_ref.shape[1]
    y = jnp.dot(att_ref[...], w_ref[0:ATT_WIDTH, :], preferred_element_type=F32)
    for hd in range(M_HEADS):
        c = hd * M_V
        ht = hft_ref[c:c + M_V, :] + hbt_ref[c:c + M_V, :]
        ht = ht * lax.rsqrt(jnp.mean(ht * ht, axis=0, keepdims=True) + NORM_EPS) * mg_ref[c:c + M_V, :]
        r = (ht.T * mo_ref[:, c:c + M_V].astype(F32)).astype(BF16)
        y = y + jnp.dot(r, w_ref[ATT_WIDTH + c:ATT_WIDTH + c + M_V, :], preferred_element_type=F32)
    x1 = x_ref[...] + mod_ref[0:1, 2 * d:3 * d] * y
    x1_ref[...] = x1
    h2 = _rms(x1) * n2_ref[...] * (1.0 + mod_ref[0:1, 4 * d:5 * d]) + mod_ref[0:1, 3 * d:4 * d]
    h2_ref[...] = h2.astype(BF16)


def _out_proj(att, hf_t, hb_t, mo, m_gain, w_out, x, mod, norm2):
    t, d = x.shape
    tm = OUT_TM
    row = lambda i: (i, 0)
    full = lambda i: (0, 0)
    return pl.pallas_call(
        _outproj_kernel,
        grid=(t // tm,),
        in_specs=[
            pl.BlockSpec((tm, ATT_WIDTH), row),
            pl.BlockSpec((M_WIDTH, tm), lambda i: (0, i)),
            pl.BlockSpec((M_WIDTH, tm), lambda i: (0, i)),
            pl.BlockSpec((tm, M_WIDTH), row),
            pl.BlockSpec((M_WIDTH, 1), full),
            pl.BlockSpec(w_out.shape, full),
            pl.BlockSpec((tm, d), row),
            pl.BlockSpec(mod.shape, full),
            pl.BlockSpec((1, d), full),
        ],
        out_specs=[pl.BlockSpec((tm, d), row), pl.BlockSpec((tm, d), row)],
        out_shape=[jax.ShapeDtypeStruct((t, d), F32), jax.ShapeDtypeStruct((t, d), BF16)],
        compiler_params=_cparams(("arbitrary",)),
        name="out_proj",
    )(att, hf_t, hb_t, mo, m_gain, w_out, x, mod, norm2)


def _ffn_kernel(h_ref, hp_ref, hn_ref, wg_ref, wv_ref, cwg_ref, cwv_ref, cbg_ref, cbv_ref, wd_ref,
                x1_ref, mod_ref, nf_ref, o_ref, hext_ref, ug_ref, uv_ref):
    i = pl.program_id(0)
    j = pl.program_id(1)
    tm = h_ref.shape[0]
    d = x1_ref.shape[1]

    @pl.when(j == 0)
    def _():
        prev = jnp.where(i == 0, 0.0, hp_ref[HALO - 1:HALO, :].astype(F32))
        nxt = jnp.where(i == pl.num_programs(0) - 1, 0.0, hn_ref[0:1, :].astype(F32))
        sub = lax.broadcasted_iota(jnp.int32, (HALO, d), 0)
        halo = jnp.where(sub == PREV_ROW, prev, jnp.where(sub == NEXT_ROW, nxt, 0.0))
        hext_ref[0:tm, :] = h_ref[...]
        hext_ref[tm:, :] = halo.astype(BF16)
        o_ref[...] = jnp.zeros(o_ref.shape, F32)

    hext = hext_ref[...]

    def up(w_ref, u_ref):
        u = jnp.dot(hext, w_ref[...], preferred_element_type=F32)
        u_ref[F32_SUBLANES:F32_SUBLANES + tm, :] = u[0:tm]
        u_ref[PREV_ROW:PREV_ROW + 1, :] = u[tm + PREV_ROW:tm + PREV_ROW + 1]
        u_ref[tm + NEXT_ROW:tm + NEXT_ROW + 1, :] = u[tm + NEXT_ROW:tm + NEXT_ROW + 1]

    up(wg_ref, ug_ref)
    up(wv_ref, uv_ref)

    def conv(u_ref, cw_ref, cb_ref):
        return (cw_ref[0:1, :] * u_ref[F32_SUBLANES - 1:F32_SUBLANES - 1 + tm, :]
                + cw_ref[1:2, :] * u_ref[F32_SUBLANES:F32_SUBLANES + tm, :]
                + cw_ref[2:3, :] * u_ref[F32_SUBLANES + 1:F32_SUBLANES + 1 + tm, :]
                + cb_ref[...])

    g = conv(ug_ref, cwg_ref, cbg_ref)
    val = conv(uv_ref, cwv_ref, cbv_ref)
    a = (g * jax.nn.sigmoid(g) * val).astype(BF16)
    o_ref[...] += jnp.dot(a, wd_ref[...], preferred_element_type=F32)

    @pl.when(j == pl.num_programs(1) - 1)
    def _():
        y = x1_ref[...] + mod_ref[0:1, 5 * d:6 * d] * o_ref[...]
        o_ref[...] = _rms(y) * nf_ref[...]


def _conv_ffn(h2, w_up, conv_w, conv_b, w_down, x1, mod, norm_f):
    t, d = h2.shape
    d_ff = w_down.shape[0]
    tm, tf = FFN_TM, FFN_TF
    nf = d_ff // tf
    hb = tm // HALO
    last_halo = t // HALO - 1
    in_specs = [
        pl.BlockSpec((tm, d), lambda i, j: (i, 0)),
        pl.BlockSpec((HALO, d), lambda i, j: (jnp.maximum(i * hb - 1, 0), 0)),
        pl.BlockSpec((HALO, d), lambda i, j: (jnp.minimum((i + 1) * hb, last_halo), 0)),
        pl.BlockSpec((d, tf), lambda i, j: (0, j)),
        pl.BlockSpec((d, tf), lambda i, j: (0, j + nf)),
        pl.BlockSpec((3, tf), lambda i, j: (0, j)),
        pl.BlockSpec((3, tf), lambda i, j: (0, j + nf)),
        pl.BlockSpec((1, tf), lambda i, j: (0, j)),
        pl.BlockSpec((1, tf), lambda i, j: (0, j + nf)),
        pl.BlockSpec((tf, d), lambda i, j: (j, 0)),
        pl.BlockSpec((tm, d), lambda i, j: (i, 0)),
        pl.BlockSpec(mod.shape, lambda i, j: (0, 0)),
        pl.BlockSpec((1, d), lambda i, j: (0, 0)),
    ]
    return pl.pallas_call(
        _ffn_kernel,
        grid=(t // tm, nf),
        in_specs=in_specs,
        out_specs=pl.BlockSpec((tm, d), lambda i, j: (i, 0)),
        out_shape=jax.ShapeDtypeStruct((t, d), F32),
        scratch_shapes=[
            pltpu.VMEM((tm + HALO, d), BF16),
            pltpu.VMEM((tm + 2 * F32_SUBLANES, tf), F32),
            pltpu.VMEM((tm + 2 * F32_SUBLANES, tf), F32),
        ],
        compiler_params=_cparams(("arbitrary", "arbitrary")),
        name="conv_ffn",
    )(h2, h2, h2, w_up, w_up, conv_w, conv_w, conv_b, conv_b, w_down, x1, mod, norm_f)


def _rope_tables(n_ctx, n_tok):
    f32 = np.float32
    rows = n_tok // GRID_W
    row = np.repeat(np.arange(rows, dtype=f32), GRID_W)
    col = np.tile(np.arange(GRID_W, dtype=f32), rows)
    inv_freq = np.power(f32(ROPE_THETA), -np.arange(0, AXIS_DIM, 2, dtype=f32) / f32(AXIS_DIM)).astype(f32)
    ang_r = row[:, None] * inv_freq[None, :]
    ang_c = col[:, None] * inv_freq[None, :]
    cos = np.concatenate([np.cos(ang_r), np.cos(ang_r), np.cos(ang_c), np.cos(ang_c)], axis=1)
    sin = np.concatenate([-np.sin(ang_r), np.sin(ang_r), -np.sin(ang_c), np.sin(ang_c)], axis=1)
    cos = np.concatenate([np.ones((n_ctx, HEAD_DIM), f32), cos], axis=0).astype(f32)
    sin = np.concatenate([np.zeros((n_ctx, HEAD_DIM), f32), sin], axis=0).astype(f32)
    return jnp.asarray(cos), jnp.asarray(sin)


def kernel(x, c, ctx, c_ctx, w_mod, b_mod, norm1, w_in, q_norm, k_norm, b_igate, b_fgate, m_norm,
           w_out, norm2, w_up, conv_w, conv_b, w_down, norm_f):
    batch, n_tok, d = x.shape
    assert batch == 1 and w_mod.shape[0] == 1
    n_ctx = ctx.shape[1]
    x2 = x[0]
    ctx2 = ctx[0]

    cs = jnp.stack([c[0], c_ctx], axis=1)
    mod = _modulation(cs, w_mod[0], b_mod[0])

    w_in_t = w_in[0].T
    w_main = _cast_main_columns(w_in_t, OFF_G)
    w_gate = _gate_weights(w_in_t, OFF_G)
    cos_t, sin_t = _rope_tables(n_ctx, n_tok)
    q_gain = (q_norm[0] * (HEAD_DIM ** -0.5 * LOG2E)).reshape(1, HEAD_DIM)
    k_gain = k_norm[0].reshape(1, HEAD_DIM)
    q, k, vt, mq_t, mk, mv_t, mo, gates = _in_proj(
        x2, ctx2, mod, norm1[0].reshape(1, d), w_main, w_gate, q_gain, k_gain, cos_t, sin_t)

    att = _attention(q, k, vt)

    hf_t, hb_t, (w_out_b, w_up_b, w_down_b) = _mlstm(
        mk, mq_t, mv_t, gates, b_igate[0], b_fgate[0], n_tok, (w_out[0], w_up[0], w_down[0]))

    x1, h2 = _out_proj(att, hf_t, hb_t, mo, m_norm[0].reshape(M_WIDTH, 1), w_out_b, x2, mod,
                       norm2[0].reshape(1, d))

    out = _conv_ffn(h2, w_up_b, conv_w[0], conv_b[0].reshape(1, -1), w_down_b, x1, mod, norm_f.reshape(1, d))
    return out[None]
```

```python
import functools
import math

import jax
import jax.numpy as jnp
import numpy as np
from jax import lax
from jax.experimental import pallas as pl
from jax.experimental.pallas import tpu as pltpu

F32 = jnp.float32
BF16 = jnp.bfloat16

GRID_W = 64
HEAD_DIM = 128
N_Q_HEADS = 8
N_KV_HEADS = 2
Q_PER_KV = N_Q_HEADS // N_KV_HEADS
AXIS_DIM = HEAD_DIM // 2
ROPE_THETA = 10000.0
M_HEADS = 4
M_V = 256
M_QK = 128
N_DIR = 2
NORM_EPS = 1e-6
ATT_WIDTH = N_Q_HEADS * HEAD_DIM
KV_WIDTH = N_KV_HEADS * HEAD_DIM
M_WIDTH = M_HEADS * M_V
MQK_WIDTH = M_HEADS * M_QK
N_GATES = N_DIR * 2 * M_HEADS
N_UNITS = N_DIR * M_HEADS

OFF_AQ = 0
OFF_AK = OFF_AQ + ATT_WIDTH
OFF_AV = OFF_AK + KV_WIDTH
OFF_MQ = OFF_AV + KV_WIDTH
OFF_MK = OFF_MQ + MQK_WIDTH
OFF_MV = OFF_MK + MQK_WIDTH
OFF_MO = OFF_MV + M_WIDTH
OFF_G = OFF_MO + M_WIDTH

LANES = 128
MXU_WIDTH = 256
BF16_SUBLANES = 16
VMEM_LIMIT = 50 * 1024 * 1024

ROW_TILE = 256
MLSTM_CHUNK = 256
ATTN_TQ = 512
ATTN_UNIT_Q = 256
ATTN_TK_CAP = 768
ATTN_LOOKAHEAD = 4
OUT_TM = 512
FFN_TM = 512
FFN_TF = 512
HALO = BF16_SUBLANES
MOD_TK = 128
WCAST_TN = 512
LOG2E = 1.4426950408889634


def _largest_divisor(n, cap, mult):
    best = None
    for d in range(mult, cap + 1, mult):
        if n % d == 0:
            best = d
    assert best is not None, (n, cap, mult)
    return best


def _cparams(sem, vmem=VMEM_LIMIT):
    return pltpu.CompilerParams(dimension_semantics=sem, vmem_limit_bytes=vmem)


def _rms(x, eps=NORM_EPS):
    return x * lax.rsqrt(jnp.mean(x * x, axis=-1, keepdims=True) + eps)


def _mod_kernel(cs_ref, w_ref, b_ref, o_ref):
    @pl.when(pl.program_id(0) == 0)
    def _():
        o_ref[0:1, :] = b_ref[...]
        o_ref[1:2, :] = b_ref[...]

    cs = cs_ref[...]
    s = cs * jax.nn.sigmoid(cs)
    w = w_ref[...]
    o_ref[0:1, :] += jnp.sum(s[:, 0:1] * w, axis=0, keepdims=True)
    o_ref[1:2, :] += jnp.sum(s[:, 1:2] * w, axis=0, keepdims=True)


def _modulation(cs, w_mod, b_mod):
    d, n = w_mod.shape
    tk = MOD_TK
    return pl.pallas_call(
        _mod_kernel,
        grid=(d // tk,),
        in_specs=[
            pl.BlockSpec((tk, 2), lambda j: (j, 0)),
            pl.BlockSpec((tk, n), lambda j: (j, 0)),
            pl.BlockSpec((1, n), lambda j: (0, 0)),
        ],
        out_specs=pl.BlockSpec((2, n), lambda j: (0, 0)),
        out_shape=jax.ShapeDtypeStruct((2, n), F32),
        compiler_params=_cparams(("arbitrary",)),
        name="modulation",
    )(cs, w_mod, b_mod.reshape(1, n))


def _gate_weight_kernel(wt_ref, o_ref):
    g = wt_ref[...]
    hi = g.astype(BF16)
    lo = (g - hi.astype(F32)).astype(BF16)
    pad = jnp.zeros((o_ref.shape[0] - 2 * N_GATES, g.shape[1]), BF16)
    o_ref[...] = jnp.concatenate([hi, lo, pad], axis=0)


def _gate_weights(w_t, row0):
    n, d = w_t.shape
    assert row0 % N_GATES == 0 and row0 + N_GATES == n
    return pl.pallas_call(
        _gate_weight_kernel,
        grid=(1,),
        in_specs=[pl.BlockSpec((N_GATES, d), lambda i: (row0 // N_GATES, 0))],
        out_specs=pl.BlockSpec((LANES, d), lambda i: (0, 0)),
        out_shape=jax.ShapeDtypeStruct((LANES, d), BF16),
        compiler_params=_cparams(("arbitrary",)),
        name="w_gate_split",
    )(w_t)


def _wcast_kernel(wt_ref, o_ref):
    o_ref[...] = wt_ref[...].T.astype(BF16)


def _cast_main_columns(w_t, n_main):
    n, d = w_t.shape
    tn = WCAST_TN
    assert n_main % tn == 0 and n_main <= n
    return pl.pallas_call(
        _wcast_kernel,
        grid=(n_main // tn,),
        in_specs=[pl.BlockSpec((tn, d), lambda j: (j, 0))],
        out_specs=pl.BlockSpec((d, tn), lambda j: (0, j)),
        out_shape=jax.ShapeDtypeStruct((d, n_main), BF16),
        compiler_params=_cparams(("arbitrary",)),
        name="w_in_cast",
    )(w_t)


def _swap_rope_halves(y):
    lane = lax.broadcasted_iota(jnp.int32, y.shape, 1)
    fwd = pltpu.roll(y, LANES - AXIS_DIM // 2, axis=1)
    bwd = pltpu.roll(y, AXIS_DIM // 2, axis=1)
    return jnp.where((lane & (AXIS_DIM // 2)) == 0, fwd, bwd)


def _inproj_kernel(x_ref, ctx_ref, mod_ref, n1_ref, w_ref, wg_ref, qg_ref, kg_ref, cos_ref, sin_ref,
                   q_ref, k_ref, vt_ref, mqt_ref, mk_ref, mvt_ref, mo_ref, g_ref):
    d = x_ref.shape[1]
    is_ctx = pl.program_id(0) == 0
    xin = jnp.where(is_ctx, ctx_ref[...], x_ref[...])
    shift = jnp.where(is_ctx, mod_ref[1:2, 0:d], mod_ref[0:1, 0:d])
    scale = jnp.where(is_ctx, mod_ref[1:2, d:2 * d], mod_ref[0:1, d:2 * d])
    hf = _rms(xin) * n1_ref[...] * (1.0 + scale) + shift
    h_hi = hf.astype(BF16)
    h_lo = (hf - h_hi.astype(F32)).astype(BF16)

    def proj(c0, width):
        return jnp.dot(h_hi, w_ref[:, c0:c0 + width], preferred_element_type=F32)

    cos = cos_ref[...]
    sin = sin_ref[...]

    def norm_rope(y, gain):
        yn = _rms(y) * gain
        return yn * cos + _swap_rope_halves(yn) * sin

    for j in range(ATT_WIDTH // MXU_WIDTH):
        y = proj(OFF_AQ + j * MXU_WIDTH, MXU_WIDTH)
        for t in range(MXU_WIDTH // HEAD_DIM):
            c = j * MXU_WIDTH + t * HEAD_DIM
            q_ref[:, c:c + HEAD_DIM] = norm_rope(y[:, t * HEAD_DIM:(t + 1) * HEAD_DIM], qg_ref[...]).astype(BF16)
    y = proj(OFF_AK, KV_WIDTH)
    for t in range(N_KV_HEADS):
        k_ref[:, t * HEAD_DIM:(t + 1) * HEAD_DIM] = norm_rope(
            y[:, t * HEAD_DIM:(t + 1) * HEAD_DIM], kg_ref[...]).astype(BF16)
    vt_ref[...] = proj(OFF_AV, KV_WIDTH).T.astype(BF16)
    for j in range(MQK_WIDTH // MXU_WIDTH):
        c = j * MXU_WIDTH
        mqt_ref[c:c + MXU_WIDTH, :] = proj(OFF_MQ + c, MXU_WIDTH).T.astype(BF16)
        mk_ref[:, c:c + MXU_WIDTH] = (proj(OFF_MK + c, MXU_WIDTH) * (M_QK ** -0.5)).astype(BF16)
    for j in range(M_WIDTH // MXU_WIDTH):
        c = j * MXU_WIDTH
        mvt_ref[c:c + MXU_WIDTH, :] = proj(OFF_MV + c, MXU_WIDTH).T.astype(BF16)
        mo_ref[:, c:c + MXU_WIDTH] = jax.nn.sigmoid(proj(OFF_MO + c, MXU_WIDTH)).astype(BF16)
    nt = (((1,), (1,)), ((), ()))
    gg = (lax.dot_general(h_hi, wg_ref[...], nt, preferred_element_type=F32)
          + lax.dot_general(h_lo, wg_ref[...], nt, preferred_element_type=F32))
    gg = gg + pltpu.roll(gg, LANES - N_GATES, axis=1)
    g_ref[...] = gg[:, 0:N_GATES]


def _in_proj(x, ctx, mod, norm1, w_main, w_gate, q_gain, k_gain, cos_t, sin_t):
    t, d = x.shape
    n_ctx = ctx.shape[0]
    assert n_ctx == ROW_TILE and t % ROW_TILE == 0
    rows = n_ctx + t
    steps = rows // ROW_TILE
    full = lambda i: (0, 0)
    row_all = lambda i: (i, 0)
    row_x = lambda i: (jnp.maximum(i - 1, 0), 0)
    out_shape = [
        jax.ShapeDtypeStruct((t, ATT_WIDTH), BF16),
        jax.ShapeDtypeStruct((rows, KV_WIDTH), BF16),
        jax.ShapeDtypeStruct((KV_WIDTH, rows), BF16),
        jax.ShapeDtypeStruct((MQK_WIDTH, rows), BF16),
        jax.ShapeDtypeStruct((rows, MQK_WIDTH), BF16),
        jax.ShapeDtypeStruct((M_WIDTH, rows), BF16),
        jax.ShapeDtypeStruct((t, M_WIDTH), BF16),
        jax.ShapeDtypeStruct((rows, N_GATES), F32),
    ]
    out_specs = [
        pl.BlockSpec((ROW_TILE, ATT_WIDTH), row_x),
        pl.BlockSpec((ROW_TILE, KV_WIDTH), row_all),
        pl.BlockSpec((KV_WIDTH, ROW_TILE), lambda i: (0, i)),
        pl.BlockSpec((MQK_WIDTH, ROW_TILE), lambda i: (0, i)),
        pl.BlockSpec((ROW_TILE, MQK_WIDTH), row_all),
        pl.BlockSpec((M_WIDTH, ROW_TILE), lambda i: (0, i)),
        pl.BlockSpec((ROW_TILE, M_WIDTH), row_x),
        pl.BlockSpec((ROW_TILE, N_GATES), row_all),
    ]
    in_specs = [
        pl.BlockSpec((ROW_TILE, d), row_x),
        pl.BlockSpec((ROW_TILE, d), full),
        pl.BlockSpec(mod.shape, full),
        pl.BlockSpec((1, d), full),
        pl.BlockSpec(w_main.shape, full),
        pl.BlockSpec(w_gate.shape, full),
        pl.BlockSpec((1, HEAD_DIM), full),
        pl.BlockSpec((1, HEAD_DIM), full),
        pl.BlockSpec((ROW_TILE, HEAD_DIM), row_all),
        pl.BlockSpec((ROW_TILE, HEAD_DIM), row_all),
    ]
    return pl.pallas_call(
        _inproj_kernel,
        grid=(steps,),
        in_specs=in_specs,
        out_specs=out_specs,
        out_shape=out_shape,
        compiler_params=_cparams(("arbitrary",)),
        name="in_proj",
    )(x, ctx, mod, norm1, w_main, w_gate, q_gain, k_gain, cos_t, sin_t)


def _attn_kernel(q_ref, k_ref, vt_ref, o_ref, qt_scr, acc_scr, *, tk):
    uq = ATTN_UNIT_Q
    n_sub = q_ref.shape[0] // uq
    n_chunks = k_ref.shape[0] // tk
    streams = [(r, h) for r in range(n_sub) for h in range(Q_PER_KV)]
    for g, (r, h) in enumerate(streams):
        qt_scr[g] = q_ref[r * uq:(r + 1) * uq, h * HEAD_DIM:(h + 1) * HEAD_DIM].astype(F32).T.astype(BF16)
    m = [jnp.full((1, uq), -jnp.inf, F32) for _ in streams]
    l = [jnp.zeros((1, uq), F32) for _ in streams]
    units = [(c, g) for c in range(n_chunks) for g in range(len(streams))]

    def scores(c, g):
        return jnp.dot(k_ref[c * tk:(c + 1) * tk, :], qt_scr[g], preferred_element_type=F32)

    pending = [scores(*u) for u in units[:ATTN_LOOKAHEAD]]
    for n, (c, g) in enumerate(units):
        st = pending.pop(0)
        if n + ATTN_LOOKAHEAD < len(units):
            pending.append(scores(*units[n + ATTN_LOOKAHEAD]))
        m_new = jnp.maximum(m[g], jnp.max(st, axis=0, keepdims=True))
        p = jnp.exp2(st - m_new)
        alpha = jnp.exp2(m[g] - m_new)
        l[g] = alpha * l[g] + jnp.sum(p, axis=0, keepdims=True)
        pv = jnp.dot(vt_ref[:, c * tk:(c + 1) * tk], p.astype(BF16), preferred_element_type=F32)
        if c == 0:
            acc_scr[g] = pv
        else:
            acc_scr[g] = alpha * acc_scr[g] + pv
        m[g] = m_new
    for g, (r, h) in enumerate(streams):
        out = (acc_scr[g] / l[g]).T
        o_ref[r * uq:(r + 1) * uq, h * HEAD_DIM:(h + 1) * HEAD_DIM] = out.astype(o_ref.dtype)


def _attention(q, k, vt):
    t = q.shape[0]
    s_len = k.shape[0]
    tq = ATTN_TQ
    tk = _largest_divisor(s_len, ATTN_TK_CAP, MXU_WIDTH)
    group_w = Q_PER_KV * HEAD_DIM
    return pl.pallas_call(
        functools.partial(_attn_kernel, tk=tk),
        grid=(N_KV_HEADS, t // tq),
        in_specs=[
            pl.BlockSpec((tq, group_w), lambda h, i: (i, h)),
            pl.BlockSpec((s_len, HEAD_DIM), lambda h, i: (0, h), pipeline_mode=pl.Buffered(1)),
            pl.BlockSpec((HEAD_DIM, s_len), lambda h, i: (h, 0), pipeline_mode=pl.Buffered(1)),
        ],
        out_specs=pl.BlockSpec((tq, group_w), lambda h, i: (i, h)),
        out_shape=jax.ShapeDtypeStruct((t, ATT_WIDTH), BF16),
        scratch_shapes=[
            pltpu.VMEM((Q_PER_KV * tq // ATTN_UNIT_Q, HEAD_DIM, ATTN_UNIT_Q), BF16),
            pltpu.VMEM((Q_PER_KV * tq // ATTN_UNIT_Q, HEAD_DIM, ATTN_UNIT_Q), F32),
        ],
        compiler_params=_cparams(("arbitrary", "arbitrary")),
        name="attention",
    )(q, k, vt)


def _log_sigmoid(x):
    return jnp.minimum(x, 0.0) - jnp.log1p(jnp.exp(-jnp.abs(x)))


def _scan(x, axis, op, ident, reverse):
    n = x.shape[axis]
    idx = lax.broadcasted_iota(jnp.int32, x.shape, axis)
    shift = 1
    while shift < n:
        if reverse:
            moved = pltpu.roll(x, n - shift, axis=axis)
            ok = idx < n - shift
        else:
            moved = pltpu.roll(x, shift, axis=axis)
            ok = idx >= shift
        x = op(x, jnp.where(ok, moved, ident))
        shift *= 2
    return x


class _MlstmDirection:
    def __init__(self, reverse, k_ref, qt_ref, vt_ref, gi_ref, gf_ref, git_ref, gft_ref, h_ref,
                 ct_scr, mrow_scr, mcol_scr, visible):
        self.reverse, self.visible = reverse, visible
        self.k_ref, self.qt_ref, self.vt_ref, self.h_ref = k_ref, qt_ref, vt_ref, h_ref
        self.gi_ref, self.gf_ref, self.git_ref, self.gft_ref = gi_ref, gf_ref, git_ref, gft_ref
        self.ct_scr, self.mrow_scr, self.mcol_scr = ct_scr, mrow_scr, mcol_scr
        self.units = [((M_HEADS if reverse else 0) + hd, hd) for hd in range(M_HEADS)]

    def _k(self, hd):
        return self.k_ref[:, hd * M_QK:(hd + 1) * M_QK]

    def _vt(self, hd):
        return self.vt_ref[hd * M_V:(hd + 1) * M_V, :]

    def state_free_matmuls(self):
        self.early = []
        for u, hd in self.units:
            qt = self.qt_ref[hd * M_QK:(hd + 1) * M_QK, :]
            state = self.ct_scr[u]
            qk_t = jnp.dot(self._k(hd), qt, preferred_element_type=F32)
            inter_t = jnp.dot(state.astype(BF16), qt, preferred_element_type=F32)
            self.early.append((state, qk_t, inter_t))

    def gate_terms(self, brow_ref, bcol_ref):
        add = lambda a, b: a + b
        reverse = self.reverse
        i_c = self.gi_ref[...] + brow_ref[0:1, :]
        ls_c = _log_sigmoid(self.gf_ref[...] + brow_ref[1:2, :])
        self.a_c = i_c - _scan(ls_c, 0, add, 0.0, reverse)
        m_prev_r = self.mrow_scr[0:1, 0:N_UNITS]
        mm_r = jnp.maximum(m_prev_r, jnp.max(self.a_c, axis=0, keepdims=True))
        self.kscale_c = jnp.exp(self.a_c - mm_r)
        self.w_prev_r = jnp.exp(m_prev_r - mm_r)
        self.mrow_scr[0:1, 0:N_UNITS] = jnp.sum(ls_c, axis=0, keepdims=True) + mm_r
        i_r = self.git_ref[...] + bcol_ref[:, 0:1]
        ls_r = _log_sigmoid(self.gft_ref[...] + bcol_ref[:, 1:2])
        cum_r = _scan(ls_r, 1, add, 0.0, reverse)
        a_r = i_r - cum_r
        m_prev_c = self.mcol_scr[0:N_UNITS, 0:1]
        m_t = cum_r + jnp.maximum(_scan(a_r, 1, jnp.maximum, -jnp.inf, reverse), m_prev_c)
        self.r_r = cum_r - m_t
        self.w_inter_r = jnp.exp(cum_r + m_prev_c - m_t)
        self.floor_r = jnp.exp(-m_t)
        self.mcol_scr[0:N_UNITS, 0:1] = (jnp.sum(ls_r, axis=1, keepdims=True)
                                         + jnp.maximum(m_prev_c, jnp.max(a_r, axis=1, keepdims=True)))

    def state_update(self):
        for (u, hd), (state, _, _) in zip(self.units, self.early):
            kw = self._k(hd).astype(F32) * self.kscale_c[:, u:u + 1]
            w_prev = self.w_prev_r[0:1, u:u + 1]
            self.ct_scr[u, 0:M_V, :] = (w_prev * state[0:M_V]
                                        + jnp.dot(self._vt(hd), kw.astype(BF16), preferred_element_type=F32))
            self.ct_scr[u, M_V:M_V + 1, :] = w_prev * state[M_V:M_V + 1] + jnp.sum(kw, axis=0, keepdims=True)

    def outputs(self):
        for (u, hd), (_, qk_t, inter_t) in zip(self.units, self.early):
            decay_t = jnp.exp(jnp.where(self.visible, self.a_c[:, u:u + 1], -jnp.inf) + self.r_r[u:u + 1, :])
            s_t = qk_t * decay_t
            w_inter = self.w_inter_r[u:u + 1, :]
            num_t = (inter_t[0:M_V] * w_inter
                     + jnp.dot(self._vt(hd), s_t.astype(BF16), preferred_element_type=F32))
            den = inter_t[M_V:M_V + 1] * w_inter + jnp.sum(s_t, axis=0, keepdims=True)
            self.h_ref[hd * M_V:(hd + 1) * M_V, :] = num_t * (
                1.0 / jnp.maximum(jnp.abs(den), self.floor_r[u:u + 1, :]))


def _mlstm_kernel(kf_ref, qtf_ref, vtf_ref, gif_ref, gff_ref, gitf_ref, gftf_ref,
                  kb_ref, qtb_ref, vtb_ref, gib_ref, gfb_ref, gitb_ref, gftb_ref,
                  brow_ref, bcol_ref, *rest, n_cast):
    cast_in, (hf_ref, hb_ref), rest = rest[:n_cast], rest[n_cast:n_cast + 2], rest[n_cast + 2:]
    cast_out, (ct_scr, mrow_scr, mcol_scr) = rest[:n_cast], rest[n_cast:]
    L = kf_ref.shape[0]

    @pl.when(pl.program_id(0) == 0)
    def _():
        ct_scr[...] = jnp.zeros(ct_scr.shape, F32)
        mrow_scr[...] = jnp.zeros(mrow_scr.shape, F32)
        mcol_scr[...] = jnp.zeros(mcol_scr.shape, F32)

    src = lax.broadcasted_iota(jnp.int32, (L, L), 0)
    tgt = lax.broadcasted_iota(jnp.int32, (L, L), 1)
    dirs = (
        _MlstmDirection(False, kf_ref, qtf_ref, vtf_ref, gif_ref, gff_ref, gitf_ref, gftf_ref, hf_ref,
                        ct_scr, mrow_scr.at[0], mcol_scr.at[0], src <= tgt),
        _MlstmDirection(True, kb_ref, qtb_ref, vtb_ref, gib_ref, gfb_ref, gitb_ref, gftb_ref, hb_ref,
                        ct_scr, mrow_scr.at[1], mcol_scr.at[1], src >= tgt),
    )
    for d in dirs:
        d.state_free_matmuls()
    for d in dirs:
        d.gate_terms(brow_ref, bcol_ref)
    for d in dirs:
        d.state_update()
    for d in dirs:
        d.outputs()
    for w_ref, o_ref in zip(cast_in, cast_out):
        o_ref[...] = w_ref[...].astype(BF16)


def _mlstm(mk, mq_t, mv_t, gates, b_i, b_f, n_x, cast_weights):
    L = MLSTM_CHUNK
    rows = mk.shape[0]
    assert rows % L == 0 and n_x % L == 0 and rows - n_x == L
    nx = n_x // L
    steps = nx + 1
    g4 = gates.reshape(rows, N_DIR, 2, M_HEADS)
    g_in = g4[:, :, 0, :].reshape(rows, N_UNITS)
    g_fg = g4[:, :, 1, :].reshape(rows, N_UNITS)
    bias_row = jnp.stack([b_i.reshape(N_UNITS), b_f.reshape(N_UNITS)], axis=0)
    f_chunk = lambda g: g
    b_chunk = lambda g: jnp.where(g == 0, 0, nx + 1 - g)
    f_out = lambda g: (0, jnp.maximum(g - 1, 0))
    b_out = lambda g: (0, jnp.where(g == 0, nx - 1, nx - g))
    f_out_rows = lambda g: (jnp.maximum(g - 1, 0), 0)
    small = lambda g: (0, 0)

    def stream(chunk):
        return [
            pl.BlockSpec((L, MQK_WIDTH), lambda g: (chunk(g), 0)),
            pl.BlockSpec((MQK_WIDTH, L), lambda g: (0, chunk(g))),
            pl.BlockSpec((M_WIDTH, L), lambda g: (0, chunk(g))),
            pl.BlockSpec((L, N_UNITS), lambda g: (chunk(g), 0)),
            pl.BlockSpec((L, N_UNITS), lambda g: (chunk(g), 0)),
            pl.BlockSpec((N_UNITS, L), lambda g: (0, chunk(g))),
            pl.BlockSpec((N_UNITS, L), lambda g: (0, chunk(g))),
        ]

    operands = (mk, mq_t, mv_t, g_in, g_fg, g_in.T, g_fg.T)
    cast_specs, cast_shapes = [], []
    for w in cast_weights:
        assert w.shape[0] % (nx * BF16_SUBLANES) == 0, w.shape
        blk = (w.shape[0] // nx, w.shape[1])
        cast_specs.append(pl.BlockSpec(blk, f_out_rows))
        cast_shapes.append(jax.ShapeDtypeStruct(w.shape, BF16))
    outs = pl.pallas_call(
        functools.partial(_mlstm_kernel, n_cast=len(cast_weights)),
        grid=(steps,),
        in_specs=stream(f_chunk) + stream(b_chunk) + [pl.BlockSpec((2, N_UNITS), small),
                                                       pl.BlockSpec((N_UNITS, 2), small)] + cast_specs,
        out_specs=[pl.BlockSpec((M_WIDTH, L), f_out), pl.BlockSpec((M_WIDTH, L), b_out)] + cast_specs,
        out_shape=[jax.ShapeDtypeStruct((M_WIDTH, n_x), F32)] * 2 + cast_shapes,
        scratch_shapes=[
            pltpu.VMEM((N_UNITS, M_V + BF16_SUBLANES, M_QK), F32),
            pltpu.VMEM((N_DIR, 8, LANES), F32),
            pltpu.VMEM((N_DIR, 8, LANES), F32),
        ],
        compiler_params=_cparams(("arbitrary",)),
        name="mlstm",
    )(*operands, *operands, bias_row, bias_row.T, *cast_weights)
    return outs[0], outs[1], tuple(outs[2:])


def _outproj_kernel(att_ref, hft_ref, hbt_ref, mo_ref, mg_ref, w_ref, x_ref, mod_ref, n2_ref,
                    x1_ref, h2_ref):
    d = x_ref.shape[1]
    y = jnp.dot(att_ref[...], w_ref[0:ATT_WIDTH, :], preferred_element_type=F32)
    for hd in range(M_HEADS):
        c = hd * M_V
        ht = hft_ref[c:c + M_V, :] + hbt_ref[c:c + M_V, :]
        ht = ht * lax.rsqrt(jnp.mean(ht * ht, axis=0, keepdims=True) + NORM_EPS) * mg_ref[c:c + M_V, :]
        r = (ht.T * mo_ref[:, c:c + M_V].astype(F32)).astype(BF16)
        y = y + jnp.dot(r, w_ref[ATT_WIDTH + c:ATT_WIDTH + c + M_V, :], preferred_element_type=F32)
    x1 = x_ref[...] + mod_ref[0:1, 2 * d:3 * d] * y
    x1_ref[...] = x1
    h2 = _rms(x1) * n2_ref[...] * (1.0 + mod_ref[0:1, 4 * d:5 * d]) + mod_ref[0:1, 3 * d:4 * d]
    h2_ref[...] = h2.astype(BF16)


def _out_proj(att, hf_t, hb_t, mo, m_gain, w_out, x, mod, norm2):
    t, d = x.shape
    tm = OUT_TM
    row = lambda i: (i, 0)
    full = lambda i: (0, 0)
    return pl.pallas_call(
        _outproj_kernel,
        grid=(t // tm,),
        in_specs=[
            pl.BlockSpec((tm, ATT_WIDTH), row),
            pl.BlockSpec((M_WIDTH, tm), lambda i: (0, i)),
            pl.BlockSpec((M_WIDTH, tm), lambda i: (0, i)),
            pl.BlockSpec((tm, M_WIDTH), row),
            pl.BlockSpec((M_WIDTH, 1), full),
            pl.BlockSpec(w_out.shape, full),
            pl.BlockSpec((tm, d), row),
            pl.BlockSpec(mod.shape, full),
            pl.BlockSpec((1, d), full),
        ],
        out_specs=[pl.BlockSpec((tm, d), row), pl.BlockSpec((tm, d), row)],
        out_shape=[jax.ShapeDtypeStruct((t, d), F32), jax.ShapeDtypeStruct((t, d), BF16)],
        compiler_params=_cparams(("arbitrary",)),
        name="out_proj",
    )(att, hf_t, hb_t, mo, m_gain, w_out, x, mod, norm2)


def _ffn_kernel(h_ref, hp_ref, hn_ref, wg_ref, wv_ref, cwg_ref, cwv_ref, cbg_ref, cbv_ref, wd_ref,
                x1_ref, mod_ref, nf_ref, o_ref, hext_ref, ug_ref, uv_ref):
    i = pl.program_id(0)
    j = pl.program_id(1)
    tm = h_ref.shape[0]
    d = x1_ref.shape[1]

    @pl.when(j == 0)
    def _():
        hext_ref[0:HALO, :] = jnp.where(i == 0, jnp.zeros_like(hp_ref[...]), hp_ref[...])
        hext_ref[HALO:HALO + tm, :] = h_ref[...]
        hext_ref[HALO + tm:, :] = jnp.where(i == pl.num_programs(0) - 1, jnp.zeros_like(hn_ref[...]), hn_ref[...])
        o_ref[...] = jnp.zeros(o_ref.shape, F32)

    hext = hext_ref[...]
    ug_ref[...] = jnp.dot(hext, wg_ref[...], preferred_element_type=F32)
    uv_ref[...] = jnp.dot(hext, wv_ref[...], preferred_element_type=F32)

    def conv(u_ref, cw_ref, cb_ref):
        return (cw_ref[0:1, :] * u_ref[HALO - 1:HALO - 1 + tm, :]
                + cw_ref[1:2, :] * u_ref[HALO:HALO + tm, :]
                + cw_ref[2:3, :] * u_ref[HALO + 1:HALO + 1 + tm, :]
                + cb_ref[...])

    g = conv(ug_ref, cwg_ref, cbg_ref)
    val = conv(uv_ref, cwv_ref, cbv_ref)
    a = (g * jax.nn.sigmoid(g) * val).astype(BF16)
    o_ref[...] += jnp.dot(a, wd_ref[...], preferred_element_type=F32)

    @pl.when(j == pl.num_programs(1) - 1)
    def _():
        y = x1_ref[...] + mod_ref[0:1, 5 * d:6 * d] * o_ref[...]
        o_ref[...] = _rms(y) * nf_ref[...]


def _conv_ffn(h2, w_up, conv_w, conv_b, w_down, x1, mod, norm_f):
    t, d = h2.shape
    d_ff = w_down.shape[0]
    tm, tf = FFN_TM, FFN_TF
    nf = d_ff // tf
    hb = tm // HALO
    last_halo = t // HALO - 1
    in_specs = [
        pl.BlockSpec((tm, d), lambda i, j: (i, 0)),
        pl.BlockSpec((HALO, d), lambda i, j: (jnp.maximum(i * hb - 1, 0), 0)),
        pl.BlockSpec((HALO, d), lambda i, j: (jnp.minimum((i + 1) * hb, last_halo), 0)),
        pl.BlockSpec((d, tf), lambda i, j: (0, j)),
        pl.BlockSpec((d, tf), lambda i, j: (0, j + nf)),
        pl.BlockSpec((3, tf), lambda i, j: (0, j)),
        pl.BlockSpec((3, tf), lambda i, j: (0, j + nf)),
        pl.BlockSpec((1, tf), lambda i, j: (0, j)),
        pl.BlockSpec((1, tf), lambda i, j: (0, j + nf)),
        pl.BlockSpec((tf, d), lambda i, j: (j, 0)),
        pl.BlockSpec((tm, d), lambda i, j: (i, 0)),
        pl.BlockSpec(mod.shape, lambda i, j: (0, 0)),
        pl.BlockSpec((1, d), lambda i, j: (0, 0)),
    ]
    return pl.pallas_call(
        _ffn_kernel,
        grid=(t // tm, nf),
        in_specs=in_specs,
        out_specs=pl.BlockSpec((tm, d), lambda i, j: (i, 0)),
        out_shape=jax.ShapeDtypeStruct((t, d), F32),
        scratch_shapes=[
            pltpu.VMEM((tm + 2 * HALO, d), BF16),
            pltpu.VMEM((tm + 2 * HALO, tf), F32),
            pltpu.VMEM((tm + 2 * HALO, tf), F32),
        ],
        compiler_params=_cparams(("arbitrary", "arbitrary")),
        name="conv_ffn",
    )(h2, h2, h2, w_up, w_up, conv_w, conv_w, conv_b, conv_b, w_down, x1, mod, norm_f)


def _rope_tables(n_ctx, n_tok):
    f32 = np.float32
    rows = n_tok // GRID_W
    row = np.repeat(np.arange(rows, dtype=f32), GRID_W)
    col = np.tile(np.arange(GRID_W, dtype=f32), rows)
    inv_freq = np.power(f32(ROPE_THETA), -np.arange(0, AXIS_DIM, 2, dtype=f32) / f32(AXIS_DIM)).astype(f32)
    ang_r = row[:, None] * inv_freq[None, :]
    ang_c = col[:, None] * inv_freq[None, :]
    cos = np.concatenate([np.cos(ang_r), np.cos(ang_r), np.cos(ang_c), np.cos(ang_c)], axis=1)
    sin = np.concatenate([-np.sin(ang_r), np.sin(ang_r), -np.sin(ang_c), np.sin(ang_c)], axis=1)
    cos = np.concatenate([np.ones((n_ctx, HEAD_DIM), f32), cos], axis=0).astype(f32)
    sin = np.concatenate([np.zeros((n_ctx, HEAD_DIM), f32), sin], axis=0).astype(f32)
    return jnp.asarray(cos), jnp.asarray(sin)


def kernel(x, c, ctx, c_ctx, w_mod, b_mod, norm1, w_in, q_norm, k_norm, b_igate, b_fgate, m_norm,
           w_out, norm2, w_up, conv_w, conv_b, w_down, norm_f):
    batch, n_tok, d = x.shape
    assert batch == 1 and w_mod.shape[0] == 1
    n_ctx = ctx.shape[1]
    x2 = x[0]
    ctx2 = ctx[0]

    cs = jnp.stack([c[0], c_ctx], axis=1)
    mod = _modulation(cs, w_mod[0], b_mod[0])

    w_in_t = w_in[0].T
    w_main = _cast_main_columns(w_in_t, OFF_G)
    w_gate = _gate_weights(w_in_t, OFF_G)
    cos_t, sin_t = _rope_tables(n_ctx, n_tok)
    q_gain = (q_norm[0] * (HEAD_DIM ** -0.5 * LOG2E)).reshape(1, HEAD_DIM)
    k_gain = k_norm[0].reshape(1, HEAD_DIM)
    q, k, vt, mq_t, mk, mv_t, mo, gates = _in_proj(
        x2, ctx2, mod, norm1[0].reshape(1, d), w_main, w_gate, q_gain, k_gain, cos_t, sin_t)

    att = _attention(q, k, vt)

    hf_t, hb_t, (w_out_b, w_up_b, w_down_b) = _mlstm(
        mk, mq_t, mv_t, gates, b_igate[0], b_fgate[0], n_tok, (w_out[0], w_up[0], w_down[0]))

    x1, h2 = _out_proj(att, hf_t, hb_t, mo, m_norm[0].reshape(M_WIDTH, 1), w_out_b, x2, mod,
                       norm2[0].reshape(1, d))

    out = _conv_ffn(h2, w_up_b, conv_w[0], conv_b[0].reshape(1, -1), w_down_b, x1, mod, norm_f.reshape(1, d))
    return out[None]
```

```python
import functools
import math

import jax
import jax.numpy as jnp
import numpy as np
from jax import lax
from jax.experimental import pallas as pl
from jax.experimental.pallas import tpu as pltpu

F32 = jnp.float32
BF16 = jnp.bfloat16

GRID_W = 64
HEAD_DIM = 128
N_Q_HEADS = 8
N_KV_HEADS = 2
Q_PER_KV = N_Q_HEADS // N_KV_HEADS
AXIS_DIM = HEAD_DIM // 2
ROPE_THETA = 10000.0
M_HEADS = 4
M_V = 256
M_QK = 128
N_DIR = 2
NORM_EPS = 1e-6
ATT_WIDTH = N_Q_HEADS * HEAD_DIM
KV_WIDTH = N_KV_HEADS * HEAD_DIM
M_WIDTH = M_HEADS * M_V
MQK_WIDTH = M_HEADS * M_QK
N_GATES = N_DIR * 2 * M_HEADS
N_UNITS = N_DIR * M_HEADS

OFF_AQ = 0
OFF_AK = OFF_AQ + ATT_WIDTH
OFF_AV = OFF_AK + KV_WIDTH
OFF_MQ = OFF_AV + KV_WIDTH
OFF_MK = OFF_MQ + MQK_WIDTH
OFF_MV = OFF_MK + MQK_WIDTH
OFF_MO = OFF_MV + M_WIDTH
OFF_G = OFF_MO + M_WIDTH

LANES = 128
MXU_WIDTH = 256
BF16_SUBLANES = 16
VMEM_LIMIT = 50 * 1024 * 1024

ROW_TILE = 256
MLSTM_CHUNK = 256
ATTN_TQ = 512
ATTN_TQ_GENERAL = 512
ATTN_SCORE_BOUND = 64.0
ATTN_BOUND_HEADROOM = 1.02
ATTN_UNIT_Q = 256
ATTN_TK_CAP = 768
ATTN_LOOKAHEAD = 4
OUT_TM = 512
FFN_TM = 512
FFN_TF = 512
HALO = BF16_SUBLANES
MOD_TK = 128
WCAST_TN = 512
LOG2E = 1.4426950408889634


def _largest_divisor(n, cap, mult):
    best = None
    for d in range(mult, cap + 1, mult):
        if n % d == 0:
            best = d
    assert best is not None, (n, cap, mult)
    return best


def _cparams(sem, vmem=VMEM_LIMIT):
    return pltpu.CompilerParams(dimension_semantics=sem, vmem_limit_bytes=vmem)


def _rms(x, eps=NORM_EPS):
    return x * lax.rsqrt(jnp.mean(x * x, axis=-1, keepdims=True) + eps)


def _mod_kernel(cs_ref, w_ref, b_ref, o_ref):
    @pl.when(pl.program_id(0) == 0)
    def _():
        o_ref[0:1, :] = b_ref[...]
        o_ref[1:2, :] = b_ref[...]

    cs = cs_ref[...]
    s = cs * jax.nn.sigmoid(cs)
    w = w_ref[...]
    o_ref[0:1, :] += jnp.sum(s[:, 0:1] * w, axis=0, keepdims=True)
    o_ref[1:2, :] += jnp.sum(s[:, 1:2] * w, axis=0, keepdims=True)


def _modulation(cs, w_mod, b_mod):
    d, n = w_mod.shape
    tk = MOD_TK
    return pl.pallas_call(
        _mod_kernel,
        grid=(d // tk,),
        in_specs=[
            pl.BlockSpec((tk, 2), lambda j: (j, 0)),
            pl.BlockSpec((tk, n), lambda j: (j, 0)),
            pl.BlockSpec((1, n), lambda j: (0, 0)),
        ],
        out_specs=pl.BlockSpec((2, n), lambda j: (0, 0)),
        out_shape=jax.ShapeDtypeStruct((2, n), F32),
        compiler_params=_cparams(("arbitrary",)),
        name="modulation",
    )(cs, w_mod, b_mod.reshape(1, n))


def _gate_weight_kernel(wt_ref, o_ref):
    g = wt_ref[...]
    hi = g.astype(BF16)
    lo = (g - hi.astype(F32)).astype(BF16)
    pad = jnp.zeros((o_ref.shape[0] - 2 * N_GATES, g.shape[1]), BF16)
    o_ref[...] = jnp.concatenate([hi, lo, pad], axis=0)


def _gate_weights(w_t, row0):
    n, d = w_t.shape
    assert row0 % N_GATES == 0 and row0 + N_GATES == n
    return pl.pallas_call(
        _gate_weight_kernel,
        grid=(1,),
        in_specs=[pl.BlockSpec((N_GATES, d), lambda i: (row0 // N_GATES, 0))],
        out_specs=pl.BlockSpec((LANES, d), lambda i: (0, 0)),
        out_shape=jax.ShapeDtypeStruct((LANES, d), BF16),
        compiler_params=_cparams(("arbitrary",)),
        name="w_gate_split",
    )(w_t)


def _wcast_kernel(wt_ref, o_ref):
    o_ref[...] = wt_ref[...].T.astype(BF16)


def _cast_main_columns(w_t, n_main):
    n, d = w_t.shape
    tn = WCAST_TN
    assert n_main % tn == 0 and n_main <= n
    return pl.pallas_call(
        _wcast_kernel,
        grid=(n_main // tn,),
        in_specs=[pl.BlockSpec((tn, d), lambda j: (j, 0))],
        out_specs=pl.BlockSpec((d, tn), lambda j: (0, j)),
        out_shape=jax.ShapeDtypeStruct((d, n_main), BF16),
        compiler_params=_cparams(("arbitrary",)),
        name="w_in_cast",
    )(w_t)


def _swap_rope_halves(y):
    lane = lax.broadcasted_iota(jnp.int32, y.shape, 1)
    fwd = pltpu.roll(y, LANES - AXIS_DIM // 2, axis=1)
    bwd = pltpu.roll(y, AXIS_DIM // 2, axis=1)
    return jnp.where((lane & (AXIS_DIM // 2)) == 0, fwd, bwd)


def _inproj_kernel(x_ref, ctx_ref, mod_ref, n1_ref, w_ref, wg_ref, qg_ref, kg_ref, cos_ref, sin_ref,
                   q_ref, k_ref, vt_ref, mqt_ref, mk_ref, mvt_ref, mo_ref, g_ref):
    d = x_ref.shape[1]
    is_ctx = pl.program_id(0) == 0
    xin = jnp.where(is_ctx, ctx_ref[...], x_ref[...])
    shift = jnp.where(is_ctx, mod_ref[1:2, 0:d], mod_ref[0:1, 0:d])
    scale = jnp.where(is_ctx, mod_ref[1:2, d:2 * d], mod_ref[0:1, d:2 * d])
    hf = _rms(xin) * n1_ref[...] * (1.0 + scale) + shift
    h_hi = hf.astype(BF16)

    def proj(c0, width):
        return jnp.dot(h_hi, w_ref[:, c0:c0 + width], preferred_element_type=F32)

    cos = cos_ref[...]
    sin = sin_ref[...]

    def norm_rope(y, gain):
        yn = _rms(y) * gain
        return yn * cos + _swap_rope_halves(yn) * sin

    for j in range(ATT_WIDTH // MXU_WIDTH):
        y = proj(OFF_AQ + j * MXU_WIDTH, MXU_WIDTH)
        for t in range(MXU_WIDTH // HEAD_DIM):
            c = j * MXU_WIDTH + t * HEAD_DIM
            q_ref[:, c:c + HEAD_DIM] = norm_rope(y[:, t * HEAD_DIM:(t + 1) * HEAD_DIM], qg_ref[...]).astype(BF16)
    y = proj(OFF_AK, KV_WIDTH)
    for t in range(N_KV_HEADS):
        k_ref[:, t * HEAD_DIM:(t + 1) * HEAD_DIM] = norm_rope(
            y[:, t * HEAD_DIM:(t + 1) * HEAD_DIM], kg_ref[...]).astype(BF16)
    vt_ref[...] = proj(OFF_AV, KV_WIDTH).T.astype(BF16)
    for j in range(MQK_WIDTH // MXU_WIDTH):
        c = j * MXU_WIDTH
        mqt_ref[c:c + MXU_WIDTH, :] = proj(OFF_MQ + c, MXU_WIDTH).T.astype(BF16)
        mk_ref[:, c:c + MXU_WIDTH] = (proj(OFF_MK + c, MXU_WIDTH) * (M_QK ** -0.5)).astype(BF16)
    for j in range(M_WIDTH // MXU_WIDTH):
        c = j * MXU_WIDTH
        mvt_ref[c:c + MXU_WIDTH, :] = proj(OFF_MV + c, MXU_WIDTH).T.astype(BF16)
        mo_ref[:, c:c + MXU_WIDTH] = jax.nn.sigmoid(proj(OFF_MO + c, MXU_WIDTH)).astype(BF16)
    nt = (((1,), (1,)), ((), ()))
    gg = lax.dot_general(h_hi, wg_ref[...], nt, preferred_element_type=F32)
    gg = gg + pltpu.roll(gg, LANES - N_GATES, axis=1)
    g_ref[...] = gg[:, 0:N_GATES]


def _in_proj(x, ctx, mod, norm1, w_main, w_gate, q_gain, k_gain, cos_t, sin_t):
    t, d = x.shape
    n_ctx = ctx.shape[0]
    assert n_ctx == ROW_TILE and t % ROW_TILE == 0
    rows = n_ctx + t
    steps = rows // ROW_TILE
    full = lambda i: (0, 0)
    row_all = lambda i: (i, 0)
    row_x = lambda i: (jnp.maximum(i - 1, 0), 0)
    out_shape = [
        jax.ShapeDtypeStruct((t, ATT_WIDTH), BF16),
        jax.ShapeDtypeStruct((rows, KV_WIDTH), BF16),
        jax.ShapeDtypeStruct((KV_WIDTH, rows), BF16),
        jax.ShapeDtypeStruct((MQK_WIDTH, rows), BF16),
        jax.ShapeDtypeStruct((rows, MQK_WIDTH), BF16),
        jax.ShapeDtypeStruct((M_WIDTH, rows), BF16),
        jax.ShapeDtypeStruct((t, M_WIDTH), BF16),
        jax.ShapeDtypeStruct((rows, N_GATES), F32),
    ]
    out_specs = [
        pl.BlockSpec((ROW_TILE, ATT_WIDTH), row_x),
        pl.BlockSpec((ROW_TILE, KV_WIDTH), row_all),
        pl.BlockSpec((KV_WIDTH, ROW_TILE), lambda i: (0, i)),
        pl.BlockSpec((MQK_WIDTH, ROW_TILE), lambda i: (0, i)),
        pl.BlockSpec((ROW_TILE, MQK_WIDTH), row_all),
        pl.BlockSpec((M_WIDTH, ROW_TILE), lambda i: (0, i)),
        pl.BlockSpec((ROW_TILE, M_WIDTH), row_x),
        pl.BlockSpec((ROW_TILE, N_GATES), row_all),
    ]
    in_specs = [
        pl.BlockSpec((ROW_TILE, d), row_x),
        pl.BlockSpec((ROW_TILE, d), full),
        pl.BlockSpec(mod.shape, full),
        pl.BlockSpec((1, d), full),
        pl.BlockSpec(w_main.shape, full),
        pl.BlockSpec(w_gate.shape, full),
        pl.BlockSpec((1, HEAD_DIM), full),
        pl.BlockSpec((1, HEAD_DIM), full),
        pl.BlockSpec((ROW_TILE, HEAD_DIM), row_all),
        pl.BlockSpec((ROW_TILE, HEAD_DIM), row_all),
    ]
    return pl.pallas_call(
        _inproj_kernel,
        grid=(steps,),
        in_specs=in_specs,
        out_specs=out_specs,
        out_shape=out_shape,
        compiler_params=_cparams(("arbitrary",)),
        name="in_proj",
    )(x, ctx, mod, norm1, w_main, w_gate, q_gain, k_gain, cos_t, sin_t)


def _attn_kernel(q_ref, k_ref, vt_ref, *rest, tk, n_cast, bounded):
    cast_in, o_ref, cast_out = rest[:n_cast], rest[n_cast], rest[n_cast + 1:2 * n_cast + 1]
    qt_scr, acc_scr = rest[2 * n_cast + 1:]
    uq = ATTN_UNIT_Q
    n_sub = q_ref.shape[0] // uq
    n_chunks = k_ref.shape[0] // tk
    streams = [(r, h) for r in range(n_sub) for h in range(Q_PER_KV)]
    for g, (r, h) in enumerate(streams):
        qt_scr[g] = q_ref[r * uq:(r + 1) * uq, h * HEAD_DIM:(h + 1) * HEAD_DIM].astype(F32).T.astype(BF16)
    zero_row = jnp.zeros((1, uq), F32)
    if bounded:
        units = [(c, g) for c in range(n_chunks) for g in range(len(streams))]

        def scores(c, g):
            return jnp.dot(k_ref[c * tk:(c + 1) * tk, :], qt_scr[g], preferred_element_type=F32)

        l = [zero_row for _ in streams]
        pending = [scores(*u) for u in units[:ATTN_LOOKAHEAD]]
        for n, (c, g) in enumerate(units):
            st = pending.pop(0)
            if n + ATTN_LOOKAHEAD < len(units):
                pending.append(scores(*units[n + ATTN_LOOKAHEAD]))
            p = jnp.exp2(st)
            l[g] = l[g] + jnp.sum(p, axis=0, keepdims=True)
            pv = jnp.dot(vt_ref[:, c * tk:(c + 1) * tk], p.astype(BF16), preferred_element_type=F32)
            if c == 0:
                acc_scr[g] = pv
            else:
                acc_scr[g] += pv
    else:
        acc_scr[...] = jnp.zeros(acc_scr.shape, F32)

        def chunk(c, carry):
            m, l = carry
            r0 = pl.multiple_of(c * tk, tk)
            kc = k_ref[pl.ds(r0, tk), :]
            vtc = vt_ref[:, pl.ds(r0, tk)]
            m_out, l_out = [], []
            for g in range(len(streams)):
                st = jnp.dot(kc, qt_scr[g], preferred_element_type=F32)
                m_new = jnp.maximum(m[g], jnp.max(st, axis=0, keepdims=True))
                p = jnp.exp2(st - m_new)
                alpha = jnp.exp2(m[g] - m_new)
                l_out.append(alpha * l[g] + jnp.sum(p, axis=0, keepdims=True))
                acc_scr[g] = alpha * acc_scr[g] + jnp.dot(vtc, p.astype(BF16), preferred_element_type=F32)
                m_out.append(m_new)
            return tuple(m_out), tuple(l_out)

        init = (tuple(jnp.full((1, uq), -jnp.inf, F32) for _ in streams), tuple(zero_row for _ in streams))
        _, l = lax.fori_loop(0, n_chunks, chunk, init)
    for g, (r, h) in enumerate(streams):
        out = (acc_scr[g] / l[g]).T
        o_ref[r * uq:(r + 1) * uq, h * HEAD_DIM:(h + 1) * HEAD_DIM] = out.astype(o_ref.dtype)
    for w_ref, wo_ref in zip(cast_in, cast_out):
        wo_ref[...] = w_ref[...].astype(BF16)


def _attention(q, k, vt, cast_weights, *, bounded, tq):
    t = q.shape[0]
    s_len = k.shape[0]
    tk = _largest_divisor(s_len, ATTN_TK_CAP, MXU_WIDTH)
    group_w = Q_PER_KV * HEAD_DIM
    n_i = t // tq
    steps = N_KV_HEADS * n_i
    cast_specs, cast_shapes = [], []
    for w in cast_weights:
        assert w.shape[0] % (steps * BF16_SUBLANES) == 0, w.shape
        cast_specs.append(pl.BlockSpec((w.shape[0] // steps, w.shape[1]), lambda h, i: (h * n_i + i, 0)))
        cast_shapes.append(jax.ShapeDtypeStruct(w.shape, BF16))
    outs = pl.pallas_call(
        functools.partial(_attn_kernel, tk=tk, n_cast=len(cast_weights), bounded=bounded),
        grid=(N_KV_HEADS, n_i),
        in_specs=[
            pl.BlockSpec((tq, group_w), lambda h, i: (i, h)),
            pl.BlockSpec((s_len, HEAD_DIM), lambda h, i: (0, h), pipeline_mode=pl.Buffered(1)),
            pl.BlockSpec((HEAD_DIM, s_len), lambda h, i: (h, 0), pipeline_mode=pl.Buffered(1)),
        ] + cast_specs,
        out_specs=[pl.BlockSpec((tq, group_w), lambda h, i: (i, h))] + cast_specs,
        out_shape=[jax.ShapeDtypeStruct((t, ATT_WIDTH), BF16)] + cast_shapes,
        scratch_shapes=[
            pltpu.VMEM((Q_PER_KV * tq // ATTN_UNIT_Q, HEAD_DIM, ATTN_UNIT_Q), BF16),
            pltpu.VMEM((Q_PER_KV * tq // ATTN_UNIT_Q, HEAD_DIM, ATTN_UNIT_Q), F32),
        ],
        compiler_params=_cparams(("arbitrary", "arbitrary")),
        name="attention_bounded" if bounded else "attention_general",
    )(q, k, vt, *cast_weights)
    return outs[0], tuple(outs[1:])


def _log_sigmoid(x):
    return jnp.minimum(x, 0.0) - jnp.log1p(jnp.exp(-jnp.abs(x)))


def _scan(x, axis, op, ident, reverse):
    n = x.shape[axis]
    idx = lax.broadcasted_iota(jnp.int32, x.shape, axis)
    shift = 1
    while shift < n:
        if reverse:
            moved = pltpu.roll(x, n - shift, axis=axis)
            ok = idx < n - shift
        else:
            moved = pltpu.roll(x, shift, axis=axis)
            ok = idx >= shift
        x = op(x, jnp.where(ok, moved, ident))
        shift *= 2
    return x


class _MlstmDirection:
    def __init__(self, reverse, k_ref, qt_ref, vt_ref, gi_ref, gf_ref, git_ref, gft_ref, h_ref,
                 ct_scr, mrow_scr, mcol_scr, visible):
        self.reverse, self.visible = reverse, visible
        self.k_ref, self.qt_ref, self.vt_ref, self.h_ref = k_ref, qt_ref, vt_ref, h_ref
        self.gi_ref, self.gf_ref, self.git_ref, self.gft_ref = gi_ref, gf_ref, git_ref, gft_ref
        self.ct_scr, self.mrow_scr, self.mcol_scr = ct_scr, mrow_scr, mcol_scr
        self.units = [((M_HEADS if reverse else 0) + hd, hd) for hd in range(M_HEADS)]

    def _k(self, hd):
        return self.k_ref[:, hd * M_QK:(hd + 1) * M_QK]

    def _vt(self, hd):
        return self.vt_ref[hd * M_V:(hd + 1) * M_V, :]

    def state_free_matmuls(self):
        self.early = []
        for u, hd in self.units:
            qt = self.qt_ref[hd * M_QK:(hd + 1) * M_QK, :]
            state = self.ct_scr[u]
            qk_t = jnp.dot(self._k(hd), qt, preferred_element_type=F32)
            inter_t = jnp.dot(state.astype(BF16), qt, preferred_element_type=F32)
            self.early.append((state, qk_t, inter_t))

    def gate_terms(self, brow_ref, bcol_ref):
        add = lambda a, b: a + b
        reverse = self.reverse
        i_c = self.gi_ref[...] + brow_ref[0:1, :]
        ls_c = _log_sigmoid(self.gf_ref[...] + brow_ref[1:2, :])
        self.a_c = i_c - _scan(ls_c, 0, add, 0.0, reverse)
        m_prev_r = self.mrow_scr[0:1, 0:N_UNITS]
        mm_r = jnp.maximum(m_prev_r, jnp.max(self.a_c, axis=0, keepdims=True))
        self.kscale_c = jnp.exp(self.a_c - mm_r)
        self.w_prev_r = jnp.exp(m_prev_r - mm_r)
        self.mrow_scr[0:1, 0:N_UNITS] = jnp.sum(ls_c, axis=0, keepdims=True) + mm_r
        i_r = self.git_ref[...] + bcol_ref[:, 0:1]
        ls_r = _log_sigmoid(self.gft_ref[...] + bcol_ref[:, 1:2])
        cum_r = _scan(ls_r, 1, add, 0.0, reverse)
        a_r = i_r - cum_r
        m_prev_c = self.mcol_scr[0:N_UNITS, 0:1]
        m_t = cum_r + jnp.maximum(_scan(a_r, 1, jnp.maximum, -jnp.inf, reverse), m_prev_c)
        self.r_r = cum_r - m_t
        self.w_inter_r = jnp.exp(cum_r + m_prev_c - m_t)
        self.floor_r = jnp.exp(-m_t)
        self.mcol_scr[0:N_UNITS, 0:1] = (jnp.sum(ls_r, axis=1, keepdims=True)
                                         + jnp.maximum(m_prev_c, jnp.max(a_r, axis=1, keepdims=True)))

    def state_update(self):
        for (u, hd), (state, _, _) in zip(self.units, self.early):
            kw = self._k(hd).astype(F32) * self.kscale_c[:, u:u + 1]
            w_prev = self.w_prev_r[0:1, u:u + 1]
            self.ct_scr[u, 0:M_V, :] = (w_prev * state[0:M_V]
                                        + jnp.dot(self._vt(hd), kw.astype(BF16), preferred_element_type=F32))
            self.ct_scr[u, M_V:M_V + 1, :] = w_prev * state[M_V:M_V + 1] + jnp.sum(kw, axis=0, keepdims=True)

    def outputs(self):
        for (u, hd), (_, qk_t, inter_t) in zip(self.units, self.early):
            decay_t = jnp.exp(jnp.where(self.visible, self.a_c[:, u:u + 1], -jnp.inf) + self.r_r[u:u + 1, :])
            s_t = qk_t * decay_t
            w_inter = self.w_inter_r[u:u + 1, :]
            num_t = (inter_t[0:M_V] * w_inter
                     + jnp.dot(self._vt(hd), s_t.astype(BF16), preferred_element_type=F32))
            den = inter_t[M_V:M_V + 1] * w_inter + jnp.sum(s_t, axis=0, keepdims=True)
            self.h_ref[hd * M_V:(hd + 1) * M_V, :] = num_t * (
                1.0 / jnp.maximum(jnp.abs(den), self.floor_r[u:u + 1, :]))


def _mlstm_kernel(kf_ref, qtf_ref, vtf_ref, gif_ref, gff_ref, gitf_ref, gftf_ref,
                  kb_ref, qtb_ref, vtb_ref, gib_ref, gfb_ref, gitb_ref, gftb_ref,
                  brow_ref, bcol_ref, *rest, n_cast):
    cast_in, (hf_ref, hb_ref), rest = rest[:n_cast], rest[n_cast:n_cast + 2], rest[n_cast + 2:]
    cast_out, (ct_scr, mrow_scr, mcol_scr) = rest[:n_cast], rest[n_cast:]
    L = kf_ref.shape[0]

    @pl.when(pl.program_id(0) == 0)
    def _():
        ct_scr[...] = jnp.zeros(ct_scr.shape, F32)
        mrow_scr[...] = jnp.zeros(mrow_scr.shape, F32)
        mcol_scr[...] = jnp.zeros(mcol_scr.shape, F32)

    src = lax.broadcasted_iota(jnp.int32, (L, L), 0)
    tgt = lax.broadcasted_iota(jnp.int32, (L, L), 1)
    dirs = (
        _MlstmDirection(False, kf_ref, qtf_ref, vtf_ref, gif_ref, gff_ref, gitf_ref, gftf_ref, hf_ref,
                        ct_scr, mrow_scr.at[0], mcol_scr.at[0], src <= tgt),
        _MlstmDirection(True, kb_ref, qtb_ref, vtb_ref, gib_ref, gfb_ref, gitb_ref, gftb_ref, hb_ref,
                        ct_scr, mrow_scr.at[1], mcol_scr.at[1], src >= tgt),
    )
    for d in dirs:
        d.state_free_matmuls()
    for d in dirs:
        d.gate_terms(brow_ref, bcol_ref)
    for d in dirs:
        d.state_update()
    for d in dirs:
        d.outputs()
    for w_ref, o_ref in zip(cast_in, cast_out):
        o_ref[...] = w_ref[...].astype(BF16)


def _mlstm(mk, mq_t, mv_t, gates, b_i, b_f, n_x, cast_weights):
    L = MLSTM_CHUNK
    rows = mk.shape[0]
    assert rows % L == 0 and n_x % L == 0 and rows - n_x == L
    nx = n_x // L
    steps = nx + 1
    g4 = gates.reshape(rows, N_DIR, 2, M_HEADS)
    g_in = g4[:, :, 0, :].reshape(rows, N_UNITS)
    g_fg = g4[:, :, 1, :].reshape(rows, N_UNITS)
    bias_row = jnp.stack([b_i.reshape(N_UNITS), b_f.reshape(N_UNITS)], axis=0)
    f_chunk = lambda g: g
    b_chunk = lambda g: jnp.where(g == 0, 0, nx + 1 - g)
    f_out = lambda g: (0, jnp.maximum(g - 1, 0))
    b_out = lambda g: (0, jnp.where(g == 0, nx - 1, nx - g))
    f_out_rows = lambda g: (jnp.maximum(g - 1, 0), 0)
    small = lambda g: (0, 0)

    def stream(chunk):
        return [
            pl.BlockSpec((L, MQK_WIDTH), lambda g: (chunk(g), 0)),
            pl.BlockSpec((MQK_WIDTH, L), lambda g: (0, chunk(g))),
            pl.BlockSpec((M_WIDTH, L), lambda g: (0, chunk(g))),
            pl.BlockSpec((L, N_UNITS), lambda g: (chunk(g), 0)),
            pl.BlockSpec((L, N_UNITS), lambda g: (chunk(g), 0)),
            pl.BlockSpec((N_UNITS, L), lambda g: (0, chunk(g))),
            pl.BlockSpec((N_UNITS, L), lambda g: (0, chunk(g))),
        ]

    operands = (mk, mq_t, mv_t, g_in, g_fg, g_in.T, g_fg.T)
    cast_specs, cast_shapes = [], []
    for w in cast_weights:
        assert w.shape[0] % (nx * BF16_SUBLANES) == 0, w.shape
        blk = (w.shape[0] // nx, w.shape[1])
        cast_specs.append(pl.BlockSpec(blk, f_out_rows))
        cast_shapes.append(jax.ShapeDtypeStruct(w.shape, BF16))
    outs = pl.pallas_call(
        functools.partial(_mlstm_kernel, n_cast=len(cast_weights)),
        grid=(steps,),
        in_specs=stream(f_chunk) + stream(b_chunk) + [pl.BlockSpec((2, N_UNITS), small),
                                                       pl.BlockSpec((N_UNITS, 2), small)] + cast_specs,
        out_specs=[pl.BlockSpec((M_WIDTH, L), f_out), pl.BlockSpec((M_WIDTH, L), b_out)] + cast_specs,
        out_shape=[jax.ShapeDtypeStruct((M_WIDTH, n_x), F32)] * 2 + cast_shapes,
        scratch_shapes=[
            pltpu.VMEM((N_UNITS, M_V + BF16_SUBLANES, M_QK), F32),
            pltpu.VMEM((N_DIR, 8, LANES), F32),
            pltpu.VMEM((N_DIR, 8, LANES), F32),
        ],
        compiler_params=_cparams(("arbitrary",)),
        name="mlstm",
    )(*operands, *operands, bias_row, bias_row.T, *cast_weights)
    return outs[0], outs[1], tuple(outs[2:])


def _outproj_kernel(att_ref, hft_ref, hbt_ref, mo_ref, mg_ref, w_ref, x_ref, mod_ref, n2_ref,
                    x1_ref, h2_ref):
    d = x_ref.shape[1]
    y = jnp.dot(att_ref[...], w_ref[0:ATT_WIDTH, :], preferred_element_type=F32)
    for hd in range(M_HEADS):
        c = hd * M_V
        ht = hft_ref[c:c + M_V, :] + hbt_ref[c:c + M_V, :]
        ht = ht * lax.rsqrt(jnp.mean(ht * ht, axis=0, keepdims=True) + NORM_EPS) * mg_ref[c:c + M_V, :]
        r = (ht.T * mo_ref[:, c:c + M_V].astype(F32)).astype(BF16)
        y = y + jnp.dot(r, w_ref[ATT_WIDTH + c:ATT_WIDTH + c + M_V, :], preferred_element_type=F32)
    x1 = x_ref[...] + mod_ref[0:1, 2 * d:3 * d] * y
    x1_ref[...] = x1
    h2 = _rms(x1) * n2_ref[...] * (1.0 + mod_ref[0:1, 4 * d:5 * d]) + mod_ref[0:1, 3 * d:4 * d]
    h2_ref[...] = h2.astype(BF16)


def _out_proj(att, hf_t, hb_t, mo, m_gain, w_out, x, mod, norm2):
    t, d = x.shape
    tm = OUT_TM
    row = lambda i: (i, 0)
    full = lambda i: (0, 0)
    return pl.pallas_call(
        _outproj_kernel,
        grid=(t // tm,),
        in_specs=[
            pl.BlockSpec((tm, ATT_WIDTH), row),
            pl.BlockSpec((M_WIDTH, tm), lambda i: (0, i)),
            pl.BlockSpec((M_WIDTH, tm), lambda i: (0, i)),
            pl.BlockSpec((tm, M_WIDTH), row),
            pl.BlockSpec((M_WIDTH, 1), full),
            pl.BlockSpec(w_out.shape, full),
            pl.BlockSpec((tm, d), row),
            pl.BlockSpec(mod.shape, full),
            pl.BlockSpec((1, d), full),
        ],
        out_specs=[pl.BlockSpec((tm, d), row), pl.BlockSpec((tm, d), row)],
        out_shape=[jax.ShapeDtypeStruct((t, d), F32), jax.ShapeDtypeStruct((t, d), BF16)],
        compiler_params=_cparams(("arbitrary",)),
        name="out_proj",
    )(att, hf_t, hb_t, mo, m_gain, w_out, x, mod, norm2)


def _ffn_kernel(h_ref, hp_ref, hn_ref, wg_ref, wv_ref, cwg_ref, cwv_ref, cbg_ref, cbv_ref, wd_ref,
                x1_ref, mod_ref, nf_ref, o_ref, hext_ref, ug_ref, uv_ref):
    i = pl.program_id(0)
    j = pl.program_id(1)
    tm = h_ref.shape[0]
    d = x1_ref.shape[1]

    @pl.when(j == 0)
    def _():
        hext_ref[0:HALO, :] = jnp.where(i == 0, jnp.zeros_like(hp_ref[...]), hp_ref[...])
        hext_ref[HALO:HALO + tm, :] = h_ref[...]
        hext_ref[HALO + tm:, :] = jnp.where(i == pl.num_programs(0) - 1, jnp.zeros_like(hn_ref[...]), hn_ref[...])
        o_ref[...] = jnp.zeros(o_ref.shape, F32)

    hext = hext_ref[...]
    ug_ref[...] = jnp.dot(hext, wg_ref[...], preferred_element_type=F32)
    uv_ref[...] = jnp.dot(hext, wv_ref[...], preferred_element_type=F32)

    def conv(u_ref, cw_ref, cb_ref):
        return (cw_ref[0:1, :] * u_ref[HALO - 1:HALO - 1 + tm, :]
                + cw_ref[1:2, :] * u_ref[HALO:HALO + tm, :]
                + cw_ref[2:3, :] * u_ref[HALO + 1:HALO + 1 + tm, :]
                + cb_ref[...])

    g = conv(ug_ref, cwg_ref, cbg_ref)
    val = conv(uv_ref, cwv_ref, cbv_ref)
    a = (g * jax.nn.sigmoid(g) * val).astype(BF16)
    o_ref[...] += jnp.dot(a, wd_ref[...], preferred_element_type=F32)

    @pl.when(j == pl.num_programs(1) - 1)
    def _():
        y = x1_ref[...] + mod_ref[0:1, 5 * d:6 * d] * o_ref[...]
        o_ref[...] = _rms(y) * nf_ref[...]


def _conv_ffn(h2, w_up, conv_w, conv_b, w_down, x1, mod, norm_f):
    t, d = h2.shape
    d_ff = w_down.shape[0]
    tm, tf = FFN_TM, FFN_TF
    nf = d_ff // tf
    hb = tm // HALO
    last_halo = t // HALO - 1
    in_specs = [
        pl.BlockSpec((tm, d), lambda i, j: (i, 0)),
        pl.BlockSpec((HALO, d), lambda i, j: (jnp.maximum(i * hb - 1, 0), 0)),
        pl.BlockSpec((HALO, d), lambda i, j: (jnp.minimum((i + 1) * hb, last_halo), 0)),
        pl.BlockSpec((d, tf), lambda i, j: (0, j)),
        pl.BlockSpec((d, tf), lambda i, j: (0, j + nf)),
        pl.BlockSpec((3, tf), lambda i, j: (0, j)),
        pl.BlockSpec((3, tf), lambda i, j: (0, j + nf)),
        pl.BlockSpec((1, tf), lambda i, j: (0, j)),
        pl.BlockSpec((1, tf), lambda i, j: (0, j + nf)),
        pl.BlockSpec((tf, d), lambda i, j: (j, 0)),
        pl.BlockSpec((tm, d), lambda i, j: (i, 0)),
        pl.BlockSpec(mod.shape, lambda i, j: (0, 0)),
        pl.BlockSpec((1, d), lambda i, j: (0, 0)),
    ]
    return pl.pallas_call(
        _ffn_kernel,
        grid=(t // tm, nf),
        in_specs=in_specs,
        out_specs=pl.BlockSpec((tm, d), lambda i, j: (i, 0)),
        out_shape=jax.ShapeDtypeStruct((t, d), F32),
        scratch_shapes=[
            pltpu.VMEM((tm + 2 * HALO, d), BF16),
            pltpu.VMEM((tm + 2 * HALO, tf), F32),
            pltpu.VMEM((tm + 2 * HALO, tf), F32),
        ],
        compiler_params=_cparams(("arbitrary", "arbitrary")),
        name="conv_ffn",
    )(h2, h2, h2, w_up, w_up, conv_w, conv_w, conv_b, conv_b, w_down, x1, mod, norm_f)


def _rope_tables(n_ctx, n_tok):
    f32 = np.float32
    rows = n_tok // GRID_W
    row = np.repeat(np.arange(rows, dtype=f32), GRID_W)
    col = np.tile(np.arange(GRID_W, dtype=f32), rows)
    inv_freq = np.power(f32(ROPE_THETA), -np.arange(0, AXIS_DIM, 2, dtype=f32) / f32(AXIS_DIM)).astype(f32)
    ang_r = row[:, None] * inv_freq[None, :]
    ang_c = col[:, None] * inv_freq[None, :]
    cos = np.concatenate([np.cos(ang_r), np.cos(ang_r), np.cos(ang_c), np.cos(ang_c)], axis=1)
    sin = np.concatenate([-np.sin(ang_r), np.sin(ang_r), -np.sin(ang_c), np.sin(ang_c)], axis=1)
    cos = np.concatenate([np.ones((n_ctx, HEAD_DIM), f32), cos], axis=0).astype(f32)
    sin = np.concatenate([np.zeros((n_ctx, HEAD_DIM), f32), sin], axis=0).astype(f32)
    return jnp.asarray(cos), jnp.asarray(sin)


def kernel(x, c, ctx, c_ctx, w_mod, b_mod, norm1, w_in, q_norm, k_norm, b_igate, b_fgate, m_norm,
           w_out, norm2, w_up, conv_w, conv_b, w_down, norm_f):
    batch, n_tok, d = x.shape
    assert batch == 1 and w_mod.shape[0] == 1
    n_ctx = ctx.shape[1]
    x2 = x[0]
    ctx2 = ctx[0]

    cs = jnp.stack([c[0], c_ctx], axis=1)
    mod = _modulation(cs, w_mod[0], b_mod[0])

    w_in_t = w_in[0].T
    w_main = _cast_main_columns(w_in_t, OFF_G)
    w_gate = _gate_weights(w_in_t, OFF_G)
    cos_t, sin_t = _rope_tables(n_ctx, n_tok)
    q_gain = (q_norm[0] * (HEAD_DIM ** -0.5 * LOG2E)).reshape(1, HEAD_DIM)
    k_gain = k_norm[0].reshape(1, HEAD_DIM)
    q, k, vt, mq_t, mk, mv_t, mo, gates = _in_proj(
        x2, ctx2, mod, norm1[0].reshape(1, d), w_main, w_gate, q_gain, k_gain, cos_t, sin_t)

    score_bound = HEAD_DIM * jnp.max(jnp.abs(q_gain)) * jnp.max(jnp.abs(k_gain)) * ATTN_BOUND_HEADROOM
    cast_weights = (w_out[0], w_up[0], w_down[0])
    att, (w_out_b, w_up_b, w_down_b) = lax.cond(
        score_bound <= ATTN_SCORE_BOUND,
        functools.partial(_attention, bounded=True, tq=ATTN_TQ),
        functools.partial(_attention, bounded=False, tq=ATTN_TQ_GENERAL),
        q, k, vt, cast_weights)

    hf_t, hb_t, _ = _mlstm(mk, mq_t, mv_t, gates, b_igate[0], b_fgate[0], n_tok, ())

    x1, h2 = _out_proj(att, hf_t, hb_t, mo, m_norm[0].reshape(M_WIDTH, 1), w_out_b, x2, mod,
                       norm2[0].reshape(1, d))

    out = _conv_ffn(h2, w_up_b, conv_w[0], conv_b[0].reshape(1, -1), w_down_b, x1, mod, norm_f.reshape(1, d))
    return out[None]
```

```python
import functools
import math

import jax
import jax.numpy as jnp
import numpy as np
from jax import lax
from jax.experimental import pallas as pl
from jax.experimental.pallas import tpu as pltpu

F32 = jnp.float32
BF16 = jnp.bfloat16

GRID_W = 64
HEAD_DIM = 128
N_Q_HEADS = 8
N_KV_HEADS = 2
Q_PER_KV = N_Q_HEADS // N_KV_HEADS
AXIS_DIM = HEAD_DIM // 2
ROPE_THETA = 10000.0
M_HEADS = 4
M_V = 256
M_QK = 128
N_DIR = 2
NORM_EPS = 1e-6
ATT_WIDTH = N_Q_HEADS * HEAD_DIM
KV_WIDTH = N_KV_HEADS * HEAD_DIM
M_WIDTH = M_HEADS * M_V
MQK_WIDTH = M_HEADS * M_QK
N_GATES = N_DIR * 2 * M_HEADS
N_UNITS = N_DIR * M_HEADS

OFF_AQ = 0
OFF_AK = OFF_AQ + ATT_WIDTH
OFF_AV = OFF_AK + KV_WIDTH
OFF_MQ = OFF_AV + KV_WIDTH
OFF_MK = OFF_MQ + MQK_WIDTH
OFF_MV = OFF_MK + MQK_WIDTH
OFF_MO = OFF_MV + M_WIDTH
OFF_G = OFF_MO + M_WIDTH

LANES = 128
MXU_WIDTH = 256
BF16_SUBLANES = 16
VMEM_LIMIT = 50 * 1024 * 1024

ROW_TILE = 256
MLSTM_CHUNK = 256
ATTN_TQ = 512
ATTN_SCORE_BOUND = 64.0
ATTN_BOUND_HEADROOM = 1.02
ATTN_UNIT_Q = 256
ATTN_TK_CAP = 768
ATTN_LOOKAHEAD = 4
OUT_TM = 512
FFN_TM = 512
FFN_TF = 512
HALO = BF16_SUBLANES
MOD_TK = 128
WCAST_TN = 512
LOG2E = 1.4426950408889634


def _largest_divisor(n, cap, mult):
    best = None
    for d in range(mult, cap + 1, mult):
        if n % d == 0:
            best = d
    assert best is not None, (n, cap, mult)
    return best


def _cparams(sem, vmem=VMEM_LIMIT):
    return pltpu.CompilerParams(dimension_semantics=sem, vmem_limit_bytes=vmem)


def _rms(x, eps=NORM_EPS):
    return x * lax.rsqrt(jnp.mean(x * x, axis=-1, keepdims=True) + eps)


def _mod_kernel(cs_ref, w_ref, b_ref, o_ref):
    @pl.when(pl.program_id(0) == 0)
    def _():
        o_ref[0:1, :] = b_ref[...]
        o_ref[1:2, :] = b_ref[...]

    cs = cs_ref[...]
    s = cs * jax.nn.sigmoid(cs)
    w = w_ref[...]
    o_ref[0:1, :] += jnp.sum(s[:, 0:1] * w, axis=0, keepdims=True)
    o_ref[1:2, :] += jnp.sum(s[:, 1:2] * w, axis=0, keepdims=True)


def _modulation(cs, w_mod, b_mod, n):
    d = w_mod.shape[0]
    tk = MOD_TK
    return pl.pallas_call(
        _mod_kernel,
        grid=(d // tk,),
        in_specs=[
            pl.BlockSpec((tk, 2), lambda j: (j, 0)),
            pl.BlockSpec((tk, n), lambda j: (j, 0)),
            pl.BlockSpec((1, n), lambda j: (0, 0)),
        ],
        out_specs=pl.BlockSpec((2, n), lambda j: (0, 0)),
        out_shape=jax.ShapeDtypeStruct((2, n), F32),
        compiler_params=_cparams(("arbitrary",)),
        name="modulation",
    )(cs, w_mod, b_mod.reshape(1, -1))


def _gate_weight_kernel(wt_ref, o_ref):
    g = wt_ref[...]
    hi = g.astype(BF16)
    lo = (g - hi.astype(F32)).astype(BF16)
    pad = jnp.zeros((o_ref.shape[0] - 2 * N_GATES, g.shape[1]), BF16)
    o_ref[...] = jnp.concatenate([hi, lo, pad], axis=0)


def _gate_weights(w_t, row0):
    n, d = w_t.shape
    assert row0 % N_GATES == 0 and row0 + N_GATES == n
    return pl.pallas_call(
        _gate_weight_kernel,
        grid=(1,),
        in_specs=[pl.BlockSpec((N_GATES, d), lambda i: (row0 // N_GATES, 0))],
        out_specs=pl.BlockSpec((LANES, d), lambda i: (0, 0)),
        out_shape=jax.ShapeDtypeStruct((LANES, d), BF16),
        compiler_params=_cparams(("arbitrary",)),
        name="w_gate_split",
    )(w_t)


def _wcast_kernel(wt_ref, o_ref):
    o_ref[...] = wt_ref[...].T.astype(BF16)


def _cast_main_columns(w_t, n_main):
    n, d = w_t.shape
    tn = WCAST_TN
    assert n_main % tn == 0 and n_main <= n
    return pl.pallas_call(
        _wcast_kernel,
        grid=(n_main // tn,),
        in_specs=[pl.BlockSpec((tn, d), lambda j: (j, 0))],
        out_specs=pl.BlockSpec((d, tn), lambda j: (0, j)),
        out_shape=jax.ShapeDtypeStruct((d, n_main), BF16),
        compiler_params=_cparams(("arbitrary",)),
        name="w_in_cast",
    )(w_t)


def _swap_rope_halves(y):
    lane = lax.broadcasted_iota(jnp.int32, y.shape, 1)
    fwd = pltpu.roll(y, LANES - AXIS_DIM // 2, axis=1)
    bwd = pltpu.roll(y, AXIS_DIM // 2, axis=1)
    return jnp.where((lane & (AXIS_DIM // 2)) == 0, fwd, bwd)


def _inproj_kernel(x_ref, ctx_ref, mod_ref, n1_ref, w_ref, wg_ref, qg_ref, kg_ref, cos_ref, sin_ref,
                   q_ref, k_ref, vt_ref, mqt_ref, mk_ref, mvt_ref, mo_ref, g_ref):
    d = x_ref.shape[1]
    is_ctx = pl.program_id(0) == 0
    xin = jnp.where(is_ctx, ctx_ref[...], x_ref[...])
    shift = jnp.where(is_ctx, mod_ref[1:2, 0:d], mod_ref[0:1, 0:d])
    scale = jnp.where(is_ctx, mod_ref[1:2, d:2 * d], mod_ref[0:1, d:2 * d])
    hf = _rms(xin) * n1_ref[...] * (1.0 + scale) + shift
    h_hi = hf.astype(BF16)

    def proj(c0, width):
        return jnp.dot(h_hi, w_ref[:, c0:c0 + width], preferred_element_type=F32)

    cos = cos_ref[...]
    sin = sin_ref[...]

    def norm_rope(y, gain):
        yn = _rms(y) * gain
        return yn * cos + _swap_rope_halves(yn) * sin

    for j in range(ATT_WIDTH // MXU_WIDTH):
        y = proj(OFF_AQ + j * MXU_WIDTH, MXU_WIDTH)
        for t in range(MXU_WIDTH // HEAD_DIM):
            c = j * MXU_WIDTH + t * HEAD_DIM
            q_ref[:, c:c + HEAD_DIM] = norm_rope(y[:, t * HEAD_DIM:(t + 1) * HEAD_DIM], qg_ref[...]).astype(BF16)
    y = proj(OFF_AK, KV_WIDTH)
    for t in range(N_KV_HEADS):
        k_ref[:, t * HEAD_DIM:(t + 1) * HEAD_DIM] = norm_rope(
            y[:, t * HEAD_DIM:(t + 1) * HEAD_DIM], kg_ref[...]).astype(BF16)
    vt_ref[...] = proj(OFF_AV, KV_WIDTH).T.astype(BF16)
    for j in range(MQK_WIDTH // MXU_WIDTH):
        c = j * MXU_WIDTH
        mqt_ref[c:c + MXU_WIDTH, :] = proj(OFF_MQ + c, MXU_WIDTH).T.astype(BF16)
        mk_ref[:, c:c + MXU_WIDTH] = (proj(OFF_MK + c, MXU_WIDTH) * (M_QK ** -0.5)).astype(BF16)
    for j in range(M_WIDTH // MXU_WIDTH):
        c = j * MXU_WIDTH
        mvt_ref[c:c + MXU_WIDTH, :] = proj(OFF_MV + c, MXU_WIDTH).T.astype(BF16)
        mo_ref[:, c:c + MXU_WIDTH] = jax.nn.sigmoid(proj(OFF_MO + c, MXU_WIDTH)).astype(BF16)
    nt = (((1,), (1,)), ((), ()))
    gg = lax.dot_general(h_hi, wg_ref[...], nt, preferred_element_type=F32)
    gg = gg + pltpu.roll(gg, LANES - N_GATES, axis=1)
    g_ref[...] = gg[:, 0:N_GATES]


def _in_proj(x, ctx, mod, norm1, w_main, w_gate, q_gain, k_gain, cos_t, sin_t):
    t, d = x.shape
    n_ctx = ctx.shape[0]
    assert n_ctx == ROW_TILE and t % ROW_TILE == 0
    rows = n_ctx + t
    steps = rows // ROW_TILE
    full = lambda i: (0, 0)
    row_all = lambda i: (i, 0)
    row_x = lambda i: (jnp.maximum(i - 1, 0), 0)
    out_shape = [
        jax.ShapeDtypeStruct((t, ATT_WIDTH), BF16),
        jax.ShapeDtypeStruct((rows, KV_WIDTH), BF16),
        jax.ShapeDtypeStruct((KV_WIDTH, rows), BF16),
        jax.ShapeDtypeStruct((MQK_WIDTH, rows), BF16),
        jax.ShapeDtypeStruct((rows, MQK_WIDTH), BF16),
        jax.ShapeDtypeStruct((M_WIDTH, rows), BF16),
        jax.ShapeDtypeStruct((t, M_WIDTH), BF16),
        jax.ShapeDtypeStruct((rows, N_GATES), F32),
    ]
    out_specs = [
        pl.BlockSpec((ROW_TILE, ATT_WIDTH), row_x),
        pl.BlockSpec((ROW_TILE, KV_WIDTH), row_all),
        pl.BlockSpec((KV_WIDTH, ROW_TILE), lambda i: (0, i)),
        pl.BlockSpec((MQK_WIDTH, ROW_TILE), lambda i: (0, i)),
        pl.BlockSpec((ROW_TILE, MQK_WIDTH), row_all),
        pl.BlockSpec((M_WIDTH, ROW_TILE), lambda i: (0, i)),
        pl.BlockSpec((ROW_TILE, M_WIDTH), row_x),
        pl.BlockSpec((ROW_TILE, N_GATES), row_all),
    ]
    in_specs = [
        pl.BlockSpec((ROW_TILE, d), row_x),
        pl.BlockSpec((ROW_TILE, d), full),
        pl.BlockSpec(mod.shape, full),
        pl.BlockSpec((1, d), full),
        pl.BlockSpec(w_main.shape, full),
        pl.BlockSpec(w_gate.shape, full),
        pl.BlockSpec((1, HEAD_DIM), full),
        pl.BlockSpec((1, HEAD_DIM), full),
        pl.BlockSpec((ROW_TILE, HEAD_DIM), row_all),
        pl.BlockSpec((ROW_TILE, HEAD_DIM), row_all),
    ]
    return pl.pallas_call(
        _inproj_kernel,
        grid=(steps,),
        in_specs=in_specs,
        out_specs=out_specs,
        out_shape=out_shape,
        compiler_params=_cparams(("arbitrary",)),
        name="in_proj",
    )(x, ctx, mod, norm1, w_main, w_gate, q_gain, k_gain, cos_t, sin_t)


def _attn_kernel(bounded_ref, q_ref, k_ref, vt_ref, cs_ref, wm_a_ref, wm_b_ref, bm_ref, *rest, tk, n_cast):
    cast_in, (o_ref, modb_ref), rest = rest[:n_cast], rest[n_cast:n_cast + 2], rest[n_cast + 2:]
    cast_out, (qt_scr, acc_scr) = rest[:n_cast], rest[n_cast:]
    uq = ATTN_UNIT_Q

    @pl.when(jnp.logical_and(pl.program_id(0) == 0, pl.program_id(1) == 0))
    def _():
        modb_ref[0:1, :] = bm_ref[...]
        modb_ref[1:2, :] = bm_ref[...]

    n_sub = q_ref.shape[0] // uq
    n_chunks = k_ref.shape[0] // tk
    streams = [(r, h) for r in range(n_sub) for h in range(Q_PER_KV)]
    for g, (r, h) in enumerate(streams):
        qt_scr[g] = q_ref[r * uq:(r + 1) * uq, h * HEAD_DIM:(h + 1) * HEAD_DIM].astype(F32).T.astype(BF16)
    zero_row = jnp.zeros((1, uq), F32)

    def finish(l):
        for g, (r, h) in enumerate(streams):
            out = (acc_scr[g] / l[g]).T
            o_ref[r * uq:(r + 1) * uq, h * HEAD_DIM:(h + 1) * HEAD_DIM] = out.astype(o_ref.dtype)
        for w_ref, wo_ref in zip(cast_in, cast_out):
            wo_ref[...] = w_ref[...].astype(BF16)
        cs = cs_ref[...]
        s = cs * jax.nn.sigmoid(cs)
        half = wm_a_ref.shape[1]
        for n, wm_ref in enumerate((wm_a_ref, wm_b_ref)):
            wm = wm_ref[...]
            modb_ref[0:1, n * half:(n + 1) * half] += jnp.sum(s[:, 0:1] * wm, axis=0, keepdims=True)
            modb_ref[1:2, n * half:(n + 1) * half] += jnp.sum(s[:, 1:2] * wm, axis=0, keepdims=True)

    @pl.when(bounded_ref[0] == 1)
    def _():
        units = [(c, g) for c in range(n_chunks) for g in range(len(streams))]

        def scores(c, g):
            return jnp.dot(k_ref[c * tk:(c + 1) * tk, :], qt_scr[g], preferred_element_type=F32)

        l = [zero_row for _ in streams]
        pending = [scores(*u) for u in units[:ATTN_LOOKAHEAD]]
        for n, (c, g) in enumerate(units):
            st = pending.pop(0)
            if n + ATTN_LOOKAHEAD < len(units):
                pending.append(scores(*units[n + ATTN_LOOKAHEAD]))
            p = jnp.exp2(st)
            l[g] = l[g] + jnp.sum(p, axis=0, keepdims=True)
            pv = jnp.dot(vt_ref[:, c * tk:(c + 1) * tk], p.astype(BF16), preferred_element_type=F32)
            if c == 0:
                acc_scr[g] = pv
            else:
                acc_scr[g] += pv
        finish(l)

    @pl.when(bounded_ref[0] == 0)
    def _():
        acc_scr[...] = jnp.zeros(acc_scr.shape, F32)

        def chunk(c, carry):
            m, l = carry
            r0 = pl.multiple_of(c * tk, tk)
            kc = k_ref[pl.ds(r0, tk), :]
            vtc = vt_ref[:, pl.ds(r0, tk)]
            m_out, l_out = [], []
            for g in range(len(streams)):
                st = jnp.dot(kc, qt_scr[g], preferred_element_type=F32)
                m_new = jnp.maximum(m[g], jnp.max(st, axis=0, keepdims=True))
                p = jnp.exp2(st - m_new)
                alpha = jnp.exp2(m[g] - m_new)
                l_out.append(alpha * l[g] + jnp.sum(p, axis=0, keepdims=True))
                acc_scr[g] = alpha * acc_scr[g] + jnp.dot(vtc, p.astype(BF16), preferred_element_type=F32)
                m_out.append(m_new)
            return tuple(m_out), tuple(l_out)

        init = (tuple(jnp.full((1, uq), -jnp.inf, F32) for _ in streams), tuple(zero_row for _ in streams))
        _, l = lax.fori_loop(0, n_chunks, chunk, init)
        finish(l)


def _attention(q, k, vt, bounded, cast_weights, cs, w_mod, b_mod, n_done):
    t = q.shape[0]
    s_len = k.shape[0]
    tq = ATTN_TQ
    tk = _largest_divisor(s_len, ATTN_TK_CAP, MXU_WIDTH)
    group_w = Q_PER_KV * HEAD_DIM
    n_i = t // tq
    steps = N_KV_HEADS * n_i
    step = lambda h, i: h * n_i + i
    n_rest = w_mod.shape[1] - n_done
    half = n_rest // 2
    assert n_done % half == 0 and w_mod.shape[0] % steps == 0
    mod_rows = w_mod.shape[0] // steps
    mod_specs = [
        pl.BlockSpec((mod_rows, 2), lambda h, i: (step(h, i), 0)),
        pl.BlockSpec((mod_rows, half), lambda h, i: (step(h, i), n_done // half)),
        pl.BlockSpec((mod_rows, half), lambda h, i: (step(h, i), n_done // half + 1)),
        pl.BlockSpec((1, n_rest), lambda h, i: (0, 0)),
    ]
    cast_specs, cast_shapes = [], []
    for w in cast_weights:
        assert w.shape[0] % (steps * BF16_SUBLANES) == 0, w.shape
        cast_specs.append(pl.BlockSpec((w.shape[0] // steps, w.shape[1]), lambda h, i: (step(h, i), 0)))
        cast_shapes.append(jax.ShapeDtypeStruct(w.shape, BF16))
    outs = pl.pallas_call(
        functools.partial(_attn_kernel, tk=tk, n_cast=len(cast_weights)),
        grid=(N_KV_HEADS, n_i),
        in_specs=[
            pl.BlockSpec(memory_space=pltpu.SMEM),
            pl.BlockSpec((tq, group_w), lambda h, i: (i, h)),
            pl.BlockSpec((s_len, HEAD_DIM), lambda h, i: (0, h), pipeline_mode=pl.Buffered(1)),
            pl.BlockSpec((HEAD_DIM, s_len), lambda h, i: (h, 0), pipeline_mode=pl.Buffered(1)),
        ] + mod_specs + cast_specs,
        out_specs=[pl.BlockSpec((tq, group_w), lambda h, i: (i, h)),
                   pl.BlockSpec((2, n_rest), lambda h, i: (0, 0))] + cast_specs,
        out_shape=[jax.ShapeDtypeStruct((t, ATT_WIDTH), BF16),
                   jax.ShapeDtypeStruct((2, n_rest), F32)] + cast_shapes,
        scratch_shapes=[
            pltpu.VMEM((Q_PER_KV * tq // ATTN_UNIT_Q, HEAD_DIM, ATTN_UNIT_Q), BF16),
            pltpu.VMEM((Q_PER_KV * tq // ATTN_UNIT_Q, HEAD_DIM, ATTN_UNIT_Q), F32),
        ],
        compiler_params=_cparams(("arbitrary", "arbitrary")),
        name="attention",
    )(bounded, q, k, vt, cs, w_mod, w_mod, b_mod[n_done:].reshape(1, n_rest), *cast_weights)
    return outs[0], outs[1], tuple(outs[2:])


def _log_sigmoid(x):
    return jnp.minimum(x, 0.0) - jnp.log1p(jnp.exp(-jnp.abs(x)))


def _scan(x, axis, op, ident, reverse):
    n = x.shape[axis]
    idx = lax.broadcasted_iota(jnp.int32, x.shape, axis)
    shift = 1
    while shift < n:
        if reverse:
            moved = pltpu.roll(x, n - shift, axis=axis)
            ok = idx < n - shift
        else:
            moved = pltpu.roll(x, shift, axis=axis)
            ok = idx >= shift
        x = op(x, jnp.where(ok, moved, ident))
        shift *= 2
    return x


class _MlstmDirection:
    def __init__(self, reverse, k_ref, qt_ref, vt_ref, gi_ref, gf_ref, git_ref, gft_ref, h_ref,
                 ct_scr, mrow_scr, mcol_scr, visible):
        self.reverse, self.visible = reverse, visible
        self.k_ref, self.qt_ref, self.vt_ref, self.h_ref = k_ref, qt_ref, vt_ref, h_ref
        self.gi_ref, self.gf_ref, self.git_ref, self.gft_ref = gi_ref, gf_ref, git_ref, gft_ref
        self.ct_scr, self.mrow_scr, self.mcol_scr = ct_scr, mrow_scr, mcol_scr
        self.units = [((M_HEADS if reverse else 0) + hd, hd) for hd in range(M_HEADS)]

    def _k(self, hd):
        return self.k_ref[:, hd * M_QK:(hd + 1) * M_QK]

    def _vt(self, hd):
        return self.vt_ref[hd * M_V:(hd + 1) * M_V, :]

    def state_free_matmuls(self):
        self.early = []
        for u, hd in self.units:
            qt = self.qt_ref[hd * M_QK:(hd + 1) * M_QK, :]
            state = self.ct_scr[u]
            qk_t = jnp.dot(self._k(hd), qt, preferred_element_type=F32)
            inter_t = jnp.dot(state.astype(BF16), qt, preferred_element_type=F32)
            self.early.append((state, qk_t, inter_t))

    def gate_terms(self, brow_ref, bcol_ref):
        add = lambda a, b: a + b
        reverse = self.reverse
        i_c = self.gi_ref[...] + brow_ref[0:1, :]
        ls_c = _log_sigmoid(self.gf_ref[...] + brow_ref[1:2, :])
        self.a_c = i_c - _scan(ls_c, 0, add, 0.0, reverse)
        m_prev_r = self.mrow_scr[0:1, 0:N_UNITS]
        mm_r = jnp.maximum(m_prev_r, jnp.max(self.a_c, axis=0, keepdims=True))
        self.kscale_c = jnp.exp(self.a_c - mm_r)
        self.w_prev_r = jnp.exp(m_prev_r - mm_r)
        self.mrow_scr[0:1, 0:N_UNITS] = jnp.sum(ls_c, axis=0, keepdims=True) + mm_r
        i_r = self.git_ref[...] + bcol_ref[:, 0:1]
        ls_r = _log_sigmoid(self.gft_ref[...] + bcol_ref[:, 1:2])
        cum_r = _scan(ls_r, 1, add, 0.0, reverse)
        a_r = i_r - cum_r
        m_prev_c = self.mcol_scr[0:N_UNITS, 0:1]
        m_t = cum_r + jnp.maximum(_scan(a_r, 1, jnp.maximum, -jnp.inf, reverse), m_prev_c)
        self.r_r = cum_r - m_t
        self.w_inter_r = jnp.exp(cum_r + m_prev_c - m_t)
        self.floor_r = jnp.exp(-m_t)
        self.mcol_scr[0:N_UNITS, 0:1] = (jnp.sum(ls_r, axis=1, keepdims=True)
                                         + jnp.maximum(m_prev_c, jnp.max(a_r, axis=1, keepdims=True)))

    def state_update(self):
        for (u, hd), (state, _, _) in zip(self.units, self.early):
            kw = self._k(hd).astype(F32) * self.kscale_c[:, u:u + 1]
            w_prev = self.w_prev_r[0:1, u:u + 1]
            self.ct_scr[u, 0:M_V, :] = (w_prev * state[0:M_V]
                                        + jnp.dot(self._vt(hd), kw.astype(BF16), preferred_element_type=F32))
            self.ct_scr[u, M_V:M_V + 1, :] = w_prev * state[M_V:M_V + 1] + jnp.sum(kw, axis=0, keepdims=True)

    def outputs(self):
        for (u, hd), (_, qk_t, inter_t) in zip(self.units, self.early):
            decay_t = jnp.exp(jnp.where(self.visible, self.a_c[:, u:u + 1], -jnp.inf) + self.r_r[u:u + 1, :])
            s_t = qk_t * decay_t
            w_inter = self.w_inter_r[u:u + 1, :]
            num_t = (inter_t[0:M_V] * w_inter
                     + jnp.dot(self._vt(hd), s_t.astype(BF16), preferred_element_type=F32))
            den = inter_t[M_V:M_V + 1] * w_inter + jnp.sum(s_t, axis=0, keepdims=True)
            self.h_ref[hd * M_V:(hd + 1) * M_V, :] = num_t * (
                1.0 / jnp.maximum(jnp.abs(den), self.floor_r[u:u + 1, :]))


def _mlstm_kernel(kf_ref, qtf_ref, vtf_ref, gif_ref, gff_ref, gitf_ref, gftf_ref,
                  kb_ref, qtb_ref, vtb_ref, gib_ref, gfb_ref, gitb_ref, gftb_ref,
                  brow_ref, bcol_ref, *rest, n_cast):
    cast_in, (hf_ref, hb_ref), rest = rest[:n_cast], rest[n_cast:n_cast + 2], rest[n_cast + 2:]
    cast_out, (ct_scr, mrow_scr, mcol_scr) = rest[:n_cast], rest[n_cast:]
    L = kf_ref.shape[0]

    @pl.when(pl.program_id(0) == 0)
    def _():
        ct_scr[...] = jnp.zeros(ct_scr.shape, F32)
        mrow_scr[...] = jnp.zeros(mrow_scr.shape, F32)
        mcol_scr[...] = jnp.zeros(mcol_scr.shape, F32)

    src = lax.broadcasted_iota(jnp.int32, (L, L), 0)
    tgt = lax.broadcasted_iota(jnp.int32, (L, L), 1)
    dirs = (
        _MlstmDirection(False, kf_ref, qtf_ref, vtf_ref, gif_ref, gff_ref, gitf_ref, gftf_ref, hf_ref,
                        ct_scr, mrow_scr.at[0], mcol_scr.at[0], src <= tgt),
        _MlstmDirection(True, kb_ref, qtb_ref, vtb_ref, gib_ref, gfb_ref, gitb_ref, gftb_ref, hb_ref,
                        ct_scr, mrow_scr.at[1], mcol_scr.at[1], src >= tgt),
    )
    for d in dirs:
        d.state_free_matmuls()
    for d in dirs:
        d.gate_terms(brow_ref, bcol_ref)
    for d in dirs:
        d.state_update()
    for d in dirs:
        d.outputs()
    for w_ref, o_ref in zip(cast_in, cast_out):
        o_ref[...] = w_ref[...].astype(BF16)


def _mlstm(mk, mq_t, mv_t, gates, b_i, b_f, n_x, cast_weights):
    L = MLSTM_CHUNK
    rows = mk.shape[0]
    assert rows % L == 0 and n_x % L == 0 and rows - n_x == L
    nx = n_x // L
    steps = nx + 1
    g4 = gates.reshape(rows, N_DIR, 2, M_HEADS)
    g_in = g4[:, :, 0, :].reshape(rows, N_UNITS)
    g_fg = g4[:, :, 1, :].reshape(rows, N_UNITS)
    bias_row = jnp.stack([b_i.reshape(N_UNITS), b_f.reshape(N_UNITS)], axis=0)
    f_chunk = lambda g: g
    b_chunk = lambda g: jnp.where(g == 0, 0, nx + 1 - g)
    f_out = lambda g: (0, jnp.maximum(g - 1, 0))
    b_out = lambda g: (0, jnp.where(g == 0, nx - 1, nx - g))
    f_out_rows = lambda g: (jnp.maximum(g - 1, 0), 0)
    small = lambda g: (0, 0)

    def stream(chunk):
        return [
            pl.BlockSpec((L, MQK_WIDTH), lambda g: (chunk(g), 0)),
            pl.BlockSpec((MQK_WIDTH, L), lambda g: (0, chunk(g))),
            pl.BlockSpec((M_WIDTH, L), lambda g: (0, chunk(g))),
            pl.BlockSpec((L, N_UNITS), lambda g: (chunk(g), 0)),
            pl.BlockSpec((L, N_UNITS), lambda g: (chunk(g), 0)),
            pl.BlockSpec((N_UNITS, L), lambda g: (0, chunk(g))),
            pl.BlockSpec((N_UNITS, L), lambda g: (0, chunk(g))),
        ]

    operands = (mk, mq_t, mv_t, g_in, g_fg, g_in.T, g_fg.T)
    cast_specs, cast_shapes = [], []
    for w in cast_weights:
        assert w.shape[0] % (nx * BF16_SUBLANES) == 0, w.shape
        blk = (w.shape[0] // nx, w.shape[1])
        cast_specs.append(pl.BlockSpec(blk, f_out_rows))
        cast_shapes.append(jax.ShapeDtypeStruct(w.shape, BF16))
    outs = pl.pallas_call(
        functools.partial(_mlstm_kernel, n_cast=len(cast_weights)),
        grid=(steps,),
        in_specs=stream(f_chunk) + stream(b_chunk) + [pl.BlockSpec((2, N_UNITS), small),
                                                       pl.BlockSpec((N_UNITS, 2), small)] + cast_specs,
        out_specs=[pl.BlockSpec((M_WIDTH, L), f_out), pl.BlockSpec((M_WIDTH, L), b_out)] + cast_specs,
        out_shape=[jax.ShapeDtypeStruct((M_WIDTH, n_x), F32)] * 2 + cast_shapes,
        scratch_shapes=[
            pltpu.VMEM((N_UNITS, M_V + BF16_SUBLANES, M_QK), F32),
            pltpu.VMEM((N_DIR, 8, LANES), F32),
            pltpu.VMEM((N_DIR, 8, LANES), F32),
        ],
        compiler_params=_cparams(("arbitrary",)),
        name="mlstm",
    )(*operands, *operands, bias_row, bias_row.T, *cast_weights)
    return outs[0], outs[1], tuple(outs[2:])


def _outproj_kernel(att_ref, hft_ref, hbt_ref, mo_ref, mg_ref, w_ref, x_ref, mod_ref, n2_ref,
                    x1_ref, h2_ref):
    d = x_ref.shape[1]
    y = jnp.dot(att_ref[...], w_ref[0:ATT_WIDTH, :], preferred_element_type=F32)
    for hd in range(M_HEADS):
        c = hd * M_V
        ht = hft_ref[c:c + M_V, :] + hbt_ref[c:c + M_V, :]
        ht = ht * lax.rsqrt(jnp.mean(ht * ht, axis=0, keepdims=True) + NORM_EPS) * mg_ref[c:c + M_V, :]
        r = (ht.T * mo_ref[:, c:c + M_V].astype(F32)).astype(BF16)
        y = y + jnp.dot(r, w_ref[ATT_WIDTH + c:ATT_WIDTH + c + M_V, :], preferred_element_type=F32)
    x1 = x_ref[...] + mod_ref[0:1, 0:d] * y
    x1_ref[...] = x1
    h2 = _rms(x1) * n2_ref[...] * (1.0 + mod_ref[0:1, 2 * d:3 * d]) + mod_ref[0:1, d:2 * d]
    h2_ref[...] = h2.astype(BF16)


def _out_proj(att, hf_t, hb_t, mo, m_gain, w_out, x, mod, norm2):
    t, d = x.shape
    tm = OUT_TM
    row = lambda i: (i, 0)
    full = lambda i: (0, 0)
    return pl.pallas_call(
        _outproj_kernel,
        grid=(t // tm,),
        in_specs=[
            pl.BlockSpec((tm, ATT_WIDTH), row),
            pl.BlockSpec((M_WIDTH, tm), lambda i: (0, i)),
            pl.BlockSpec((M_WIDTH, tm), lambda i: (0, i)),
            pl.BlockSpec((tm, M_WIDTH), row),
            pl.BlockSpec((M_WIDTH, 1), full),
            pl.BlockSpec(w_out.shape, full),
            pl.BlockSpec((tm, d), row),
            pl.BlockSpec(mod.shape, full),
            pl.BlockSpec((1, d), full),
        ],
        out_specs=[pl.BlockSpec((tm, d), row), pl.BlockSpec((tm, d), row)],
        out_shape=[jax.ShapeDtypeStruct((t, d), F32), jax.ShapeDtypeStruct((t, d), BF16)],
        compiler_params=_cparams(("arbitrary",)),
        name="out_proj",
    )(att, hf_t, hb_t, mo, m_gain, w_out, x, mod, norm2)


def _ffn_kernel(h_ref, hp_ref, hn_ref, wg_ref, wv_ref, cwg_ref, cwv_ref, cbg_ref, cbv_ref, wd_ref,
                x1_ref, mod_ref, nf_ref, o_ref, hext_ref, ug_ref, uv_ref):
    i = pl.program_id(0)
    j = pl.program_id(1)
    tm = h_ref.shape[0]
    d = x1_ref.shape[1]

    @pl.when(j == 0)
    def _():
        hext_ref[0:HALO, :] = jnp.where(i == 0, jnp.zeros_like(hp_ref[...]), hp_ref[...])
        hext_ref[HALO:HALO + tm, :] = h_ref[...]
        hext_ref[HALO + tm:, :] = jnp.where(i == pl.num_programs(0) - 1, jnp.zeros_like(hn_ref[...]), hn_ref[...])
        o_ref[...] = jnp.zeros(o_ref.shape, F32)

    hext = hext_ref[...]
    ug_ref[...] = jnp.dot(hext, wg_ref[...], preferred_element_type=F32)
    uv_ref[...] = jnp.dot(hext, wv_ref[...], preferred_element_type=F32)

    def conv(u_ref, cw_ref, cb_ref):
        return (cw_ref[0:1, :] * u_ref[HALO - 1:HALO - 1 + tm, :]
                + cw_ref[1:2, :] * u_ref[HALO:HALO + tm, :]
                + cw_ref[2:3, :] * u_ref[HALO + 1:HALO + 1 + tm, :]
                + cb_ref[...])

    g = conv(ug_ref, cwg_ref, cbg_ref)
    val = conv(uv_ref, cwv_ref, cbv_ref)
    a = (g * jax.nn.sigmoid(g) * val).astype(BF16)
    o_ref[...] += jnp.dot(a, wd_ref[...], preferred_element_type=F32)

    @pl.when(j == pl.num_programs(1) - 1)
    def _():
        y = x1_ref[...] + mod_ref[0:1, 3 * d:4 * d] * o_ref[...]
        o_ref[...] = _rms(y) * nf_ref[...]


def _conv_ffn(h2, w_up, conv_w, conv_b, w_down, x1, mod, norm_f):
    t, d = h2.shape
    d_ff = w_down.shape[0]
    tm, tf = FFN_TM, FFN_TF
    nf = d_ff // tf
    hb = tm // HALO
    last_halo = t // HALO - 1
    in_specs = [
        pl.BlockSpec((tm, d), lambda i, j: (i, 0)),
        pl.BlockSpec((HALO, d), lambda i, j: (jnp.maximum(i * hb - 1, 0), 0)),
        pl.BlockSpec((HALO, d), lambda i, j: (jnp.minimum((i + 1) * hb, last_halo), 0)),
        pl.BlockSpec((d, tf), lambda i, j: (0, j)),
        pl.BlockSpec((d, tf), lambda i, j: (0, j + nf)),
        pl.BlockSpec((3, tf), lambda i, j: (0, j)),
        pl.BlockSpec((3, tf), lambda i, j: (0, j + nf)),
        pl.BlockSpec((1, tf), lambda i, j: (0, j)),
        pl.BlockSpec((1, tf), lambda i, j: (0, j + nf)),
        pl.BlockSpec((tf, d), lambda i, j: (j, 0)),
        pl.BlockSpec((tm, d), lambda i, j: (i, 0)),
        pl.BlockSpec(mod.shape, lambda i, j: (0, 0)),
        pl.BlockSpec((1, d), lambda i, j: (0, 0)),
    ]
    return pl.pallas_call(
        _ffn_kernel,
        grid=(t // tm, nf),
        in_specs=in_specs,
        out_specs=pl.BlockSpec((tm, d), lambda i, j: (i, 0)),
        out_shape=jax.ShapeDtypeStruct((t, d), F32),
        scratch_shapes=[
            pltpu.VMEM((tm + 2 * HALO, d), BF16),
            pltpu.VMEM((tm + 2 * HALO, tf), F32),
            pltpu.VMEM((tm + 2 * HALO, tf), F32),
        ],
        compiler_params=_cparams(("arbitrary", "arbitrary")),
        name="conv_ffn",
    )(h2, h2, h2, w_up, w_up, conv_w, conv_w, conv_b, conv_b, w_down, x1, mod, norm_f)


def _rope_tables(n_ctx, n_tok):
    f32 = np.float32
    rows = n_tok // GRID_W
    row = np.repeat(np.arange(rows, dtype=f32), GRID_W)
    col = np.tile(np.arange(GRID_W, dtype=f32), rows)
    inv_freq = np.power(f32(ROPE_THETA), -np.arange(0, AXIS_DIM, 2, dtype=f32) / f32(AXIS_DIM)).astype(f32)
    ang_r = row[:, None] * inv_freq[None, :]
    ang_c = col[:, None] * inv_freq[None, :]
    cos = np.concatenate([np.cos(ang_r), np.cos(ang_r), np.cos(ang_c), np.cos(ang_c)], axis=1)
    sin = np.concatenate([-np.sin(ang_r), np.sin(ang_r), -np.sin(ang_c), np.sin(ang_c)], axis=1)
    cos = np.concatenate([np.ones((n_ctx, HEAD_DIM), f32), cos], axis=0).astype(f32)
    sin = np.concatenate([np.zeros((n_ctx, HEAD_DIM), f32), sin], axis=0).astype(f32)
    return jnp.asarray(cos), jnp.asarray(sin)


def kernel(x, c, ctx, c_ctx, w_mod, b_mod, norm1, w_in, q_norm, k_norm, b_igate, b_fgate, m_norm,
           w_out, norm2, w_up, conv_w, conv_b, w_down, norm_f):
    batch, n_tok, d = x.shape
    assert batch == 1 and w_mod.shape[0] == 1
    n_ctx = ctx.shape[1]
    x2 = x[0]
    ctx2 = ctx[0]

    cs = jnp.stack([c[0], c_ctx], axis=1)
    mod = _modulation(cs, w_mod[0], b_mod[0], 2 * d)

    w_in_t = w_in[0].T
    w_main = _cast_main_columns(w_in_t, OFF_G)
    w_gate = _gate_weights(w_in_t, OFF_G)
    cos_t, sin_t = _rope_tables(n_ctx, n_tok)
    q_gain = (q_norm[0] * (HEAD_DIM ** -0.5 * LOG2E)).reshape(1, HEAD_DIM)
    k_gain = k_norm[0].reshape(1, HEAD_DIM)
    q, k, vt, mq_t, mk, mv_t, mo, gates = _in_proj(
        x2, ctx2, mod, norm1[0].reshape(1, d), w_main, w_gate, q_gain, k_gain, cos_t, sin_t)

    score_bound = HEAD_DIM * jnp.max(jnp.abs(q_gain)) * jnp.max(jnp.abs(k_gain)) * ATTN_BOUND_HEADROOM
    bounded = (score_bound <= ATTN_SCORE_BOUND).astype(jnp.int32).reshape(1)
    att, mod_rest, (w_out_b, w_up_b, w_down_b) = _attention(
        q, k, vt, bounded, (w_out[0], w_up[0], w_down[0]), cs, w_mod[0], b_mod[0], 2 * d)

    hf_t, hb_t, _ = _mlstm(mk, mq_t, mv_t, gates, b_igate[0], b_fgate[0], n_tok, ())

    x1, h2 = _out_proj(att, hf_t, hb_t, mo, m_norm[0].reshape(M_WIDTH, 1), w_out_b, x2, mod_rest,
                       norm2[0].reshape(1, d))

    out = _conv_ffn(h2, w_up_b, conv_w[0], conv_b[0].reshape(1, -1), w_down_b, x1, mod_rest, norm_f.reshape(1, d))
    return out[None]
```

```python
import functools
import math

import jax
import jax.numpy as jnp
import numpy as np
from jax import lax
from jax.experimental import pallas as pl
from jax.experimental.pallas import tpu as pltpu

F32 = jnp.float32
BF16 = jnp.bfloat16

GRID_W = 64
HEAD_DIM = 128
N_Q_HEADS = 8
N_KV_HEADS = 2
Q_PER_KV = N_Q_HEADS // N_KV_HEADS
AXIS_DIM = HEAD_DIM // 2
ROPE_THETA = 10000.0
M_HEADS = 4
M_V = 256
M_QK = 128
N_DIR = 2
NORM_EPS = 1e-6
ATT_WIDTH = N_Q_HEADS * HEAD_DIM
KV_WIDTH = N_KV_HEADS * HEAD_DIM
M_WIDTH = M_HEADS * M_V
MQK_WIDTH = M_HEADS * M_QK
N_GATES = N_DIR * 2 * M_HEADS
N_UNITS = N_DIR * M_HEADS

OFF_AQ = 0
OFF_AK = OFF_AQ + ATT_WIDTH
OFF_AV = OFF_AK + KV_WIDTH
OFF_MQ = OFF_AV + KV_WIDTH
OFF_MK = OFF_MQ + MQK_WIDTH
OFF_MV = OFF_MK + MQK_WIDTH
OFF_MO = OFF_MV + M_WIDTH
OFF_G = OFF_MO + M_WIDTH

LANES = 128
MXU_WIDTH = 256
BF16_SUBLANES = 16
VMEM_LIMIT = 50 * 1024 * 1024

ROW_TILE = 256
MLSTM_CHUNK = 256
ATTN_TQ = 512
ATTN_SCORE_BOUND = 64.0
ATTN_BOUND_HEADROOM = 1.02
ATTN_UNIT_Q = 256
ATTN_TK_CAP = 768
ATTN_LOOKAHEAD = 4
OUT_TM = 512
OUT_ROW_SPLIT = 2
FFN_TM = 512
FFN_TF = 512
HALO = BF16_SUBLANES
MOD_TK = 128
WCAST_TN = 512
LOG2E = 1.4426950408889634


def _largest_divisor(n, cap, mult):
    best = None
    for d in range(mult, cap + 1, mult):
        if n % d == 0:
            best = d
    assert best is not None, (n, cap, mult)
    return best


def _cparams(sem, vmem=VMEM_LIMIT):
    return pltpu.CompilerParams(dimension_semantics=sem, vmem_limit_bytes=vmem)


def _rms(x, eps=NORM_EPS):
    return x * lax.rsqrt(jnp.mean(x * x, axis=-1, keepdims=True) + eps)


def _mod_kernel(cs_ref, w_ref, b_ref, o_ref):
    @pl.when(pl.program_id(0) == 0)
    def _():
        o_ref[0:1, :] = b_ref[...]
        o_ref[1:2, :] = b_ref[...]

    cs = cs_ref[...]
    s = cs * jax.nn.sigmoid(cs)
    w = w_ref[...]
    o_ref[0:1, :] += jnp.sum(s[:, 0:1] * w, axis=0, keepdims=True)
    o_ref[1:2, :] += jnp.sum(s[:, 1:2] * w, axis=0, keepdims=True)


def _modulation(cs, w_mod, b_mod, n):
    d = w_mod.shape[0]
    tk = MOD_TK
    return pl.pallas_call(
        _mod_kernel,
        grid=(d // tk,),
        in_specs=[
            pl.BlockSpec((tk, 2), lambda j: (j, 0)),
            pl.BlockSpec((tk, n), lambda j: (j, 0)),
            pl.BlockSpec((1, n), lambda j: (0, 0)),
        ],
        out_specs=pl.BlockSpec((2, n), lambda j: (0, 0)),
        out_shape=jax.ShapeDtypeStruct((2, n), F32),
        compiler_params=_cparams(("arbitrary",)),
        name="modulation",
    )(cs, w_mod, b_mod.reshape(1, -1))


def _gate_weight_kernel(wt_ref, o_ref):
    g = wt_ref[...]
    hi = g.astype(BF16)
    lo = (g - hi.astype(F32)).astype(BF16)
    pad = jnp.zeros((o_ref.shape[0] - 2 * N_GATES, g.shape[1]), BF16)
    o_ref[...] = jnp.concatenate([hi, lo, pad], axis=0)


def _gate_weights(w_t, row0):
    n, d = w_t.shape
    assert row0 % N_GATES == 0 and row0 + N_GATES == n
    return pl.pallas_call(
        _gate_weight_kernel,
        grid=(1,),
        in_specs=[pl.BlockSpec((N_GATES, d), lambda i: (row0 // N_GATES, 0))],
        out_specs=pl.BlockSpec((LANES, d), lambda i: (0, 0)),
        out_shape=jax.ShapeDtypeStruct((LANES, d), BF16),
        compiler_params=_cparams(("arbitrary",)),
        name="w_gate_split",
    )(w_t)


def _wcast_kernel(wt_ref, o_ref):
    o_ref[...] = wt_ref[...].T.astype(BF16)


def _cast_main_columns(w_t, n_main):
    n, d = w_t.shape
    tn = WCAST_TN
    assert n_main % tn == 0 and n_main <= n
    return pl.pallas_call(
        _wcast_kernel,
        grid=(n_main // tn,),
        in_specs=[pl.BlockSpec((tn, d), lambda j: (j, 0))],
        out_specs=pl.BlockSpec((d, tn), lambda j: (0, j)),
        out_shape=jax.ShapeDtypeStruct((d, n_main), BF16),
        compiler_params=_cparams(("arbitrary",)),
        name="w_in_cast",
    )(w_t)


def _swap_rope_halves(y):
    lane = lax.broadcasted_iota(jnp.int32, y.shape, 1)
    fwd = pltpu.roll(y, LANES - AXIS_DIM // 2, axis=1)
    bwd = pltpu.roll(y, AXIS_DIM // 2, axis=1)
    return jnp.where((lane & (AXIS_DIM // 2)) == 0, fwd, bwd)


def _inproj_kernel(x_ref, ctx_ref, mod_ref, n1_ref, w_ref, wg_ref, qg_ref, kg_ref, cos_ref, sin_ref,
                   q_ref, k_ref, vt_ref, mqt_ref, mk_ref, mvt_ref, mo_ref, g_ref):
    d = x_ref.shape[1]
    is_ctx = pl.program_id(0) == 0
    xin = jnp.where(is_ctx, ctx_ref[...], x_ref[...])
    shift = jnp.where(is_ctx, mod_ref[1:2, 0:d], mod_ref[0:1, 0:d])
    scale = jnp.where(is_ctx, mod_ref[1:2, d:2 * d], mod_ref[0:1, d:2 * d])
    hf = _rms(xin) * n1_ref[...] * (1.0 + scale) + shift
    h_hi = hf.astype(BF16)

    def proj(c0, width):
        return jnp.dot(h_hi, w_ref[:, c0:c0 + width], preferred_element_type=F32)

    cos = cos_ref[...]
    sin = sin_ref[...]

    def norm_rope(y, gain):
        yn = _rms(y) * gain
        return yn * cos + _swap_rope_halves(yn) * sin

    for j in range(ATT_WIDTH // MXU_WIDTH):
        y = proj(OFF_AQ + j * MXU_WIDTH, MXU_WIDTH)
        for t in range(MXU_WIDTH // HEAD_DIM):
            c = j * MXU_WIDTH + t * HEAD_DIM
            q_ref[:, c:c + HEAD_DIM] = norm_rope(y[:, t * HEAD_DIM:(t + 1) * HEAD_DIM], qg_ref[...]).astype(BF16)
    y = proj(OFF_AK, KV_WIDTH)
    for t in range(N_KV_HEADS):
        k_ref[:, t * HEAD_DIM:(t + 1) * HEAD_DIM] = norm_rope(
            y[:, t * HEAD_DIM:(t + 1) * HEAD_DIM], kg_ref[...]).astype(BF16)
    vt_ref[...] = proj(OFF_AV, KV_WIDTH).T.astype(BF16)
    for j in range(MQK_WIDTH // MXU_WIDTH):
        c = j * MXU_WIDTH
        mqt_ref[c:c + MXU_WIDTH, :] = proj(OFF_MQ + c, MXU_WIDTH).T.astype(BF16)
        mk_ref[:, c:c + MXU_WIDTH] = (proj(OFF_MK + c, MXU_WIDTH) * (M_QK ** -0.5)).astype(BF16)
    for j in range(M_WIDTH // MXU_WIDTH):
        c = j * MXU_WIDTH
        mvt_ref[c:c + MXU_WIDTH, :] = proj(OFF_MV + c, MXU_WIDTH).T.astype(BF16)
        mo_ref[:, c:c + MXU_WIDTH] = jax.nn.sigmoid(proj(OFF_MO + c, MXU_WIDTH)).astype(BF16)
    nt = (((1,), (1,)), ((), ()))
    gg = lax.dot_general(h_hi, wg_ref[...], nt, preferred_element_type=F32)
    gg = gg + pltpu.roll(gg, LANES - N_GATES, axis=1)
    g_ref[...] = gg[:, 0:N_GATES]


def _in_proj(x, ctx, mod, norm1, w_main, w_gate, q_gain, k_gain, cos_t, sin_t):
    t, d = x.shape
    n_ctx = ctx.shape[0]
    assert n_ctx == ROW_TILE and t % ROW_TILE == 0
    rows = n_ctx + t
    steps = rows // ROW_TILE
    full = lambda i: (0, 0)
    row_all = lambda i: (i, 0)
    row_x = lambda i: (jnp.maximum(i - 1, 0), 0)
    out_shape = [
        jax.ShapeDtypeStruct((t, ATT_WIDTH), BF16),
        jax.ShapeDtypeStruct((rows, KV_WIDTH), BF16),
        jax.ShapeDtypeStruct((KV_WIDTH, rows), BF16),
        jax.ShapeDtypeStruct((MQK_WIDTH, rows), BF16),
        jax.ShapeDtypeStruct((rows, MQK_WIDTH), BF16),
        jax.ShapeDtypeStruct((M_WIDTH, rows), BF16),
        jax.ShapeDtypeStruct((t, M_WIDTH), BF16),
        jax.ShapeDtypeStruct((rows, N_GATES), F32),
    ]
    out_specs = [
        pl.BlockSpec((ROW_TILE, ATT_WIDTH), row_x),
        pl.BlockSpec((ROW_TILE, KV_WIDTH), row_all),
        pl.BlockSpec((KV_WIDTH, ROW_TILE), lambda i: (0, i)),
        pl.BlockSpec((MQK_WIDTH, ROW_TILE), lambda i: (0, i)),
        pl.BlockSpec((ROW_TILE, MQK_WIDTH), row_all),
        pl.BlockSpec((M_WIDTH, ROW_TILE), lambda i: (0, i)),
        pl.BlockSpec((ROW_TILE, M_WIDTH), row_x),
        pl.BlockSpec((ROW_TILE, N_GATES), row_all),
    ]
    in_specs = [
        pl.BlockSpec((ROW_TILE, d), row_x),
        pl.BlockSpec((ROW_TILE, d), full),
        pl.BlockSpec(mod.shape, full),
        pl.BlockSpec((1, d), full),
        pl.BlockSpec(w_main.shape, full),
        pl.BlockSpec(w_gate.shape, full),
        pl.BlockSpec((1, HEAD_DIM), full),
        pl.BlockSpec((1, HEAD_DIM), full),
        pl.BlockSpec((ROW_TILE, HEAD_DIM), row_all),
        pl.BlockSpec((ROW_TILE, HEAD_DIM), row_all),
    ]
    return pl.pallas_call(
        _inproj_kernel,
        grid=(steps,),
        in_specs=in_specs,
        out_specs=out_specs,
        out_shape=out_shape,
        compiler_params=_cparams(("arbitrary",)),
        name="in_proj",
    )(x, ctx, mod, norm1, w_main, w_gate, q_gain, k_gain, cos_t, sin_t)


def _attn_kernel(bounded_ref, q_ref, k_ref, vt_ref, cs_ref, wm_a_ref, wm_b_ref, bm_ref, *rest, tk, n_cast):
    cast_in, (o_ref, modb_ref), rest = rest[:n_cast], rest[n_cast:n_cast + 2], rest[n_cast + 2:]
    cast_out, (qt_scr, acc_scr) = rest[:n_cast], rest[n_cast:]
    uq = ATTN_UNIT_Q

    @pl.when(jnp.logical_and(pl.program_id(0) == 0, pl.program_id(1) == 0))
    def _():
        modb_ref[0:1, :] = bm_ref[...]
        modb_ref[1:2, :] = bm_ref[...]

    n_sub = q_ref.shape[0] // uq
    n_chunks = k_ref.shape[0] // tk
    streams = [(r, h) for r in range(n_sub) for h in range(Q_PER_KV)]
    for g, (r, h) in enumerate(streams):
        qt_scr[g] = q_ref[r * uq:(r + 1) * uq, h * HEAD_DIM:(h + 1) * HEAD_DIM].astype(F32).T.astype(BF16)
    zero_row = jnp.zeros((1, uq), F32)

    def finish(l):
        for g, (r, h) in enumerate(streams):
            out = (acc_scr[g] / l[g]).T
            o_ref[r * uq:(r + 1) * uq, h * HEAD_DIM:(h + 1) * HEAD_DIM] = out.astype(o_ref.dtype)
        for w_ref, wo_ref in zip(cast_in, cast_out):
            wo_ref[...] = w_ref[...].astype(BF16)
        cs = cs_ref[...]
        s = cs * jax.nn.sigmoid(cs)
        half = wm_a_ref.shape[1]
        for n, wm_ref in enumerate((wm_a_ref, wm_b_ref)):
            wm = wm_ref[...]
            modb_ref[0:1, n * half:(n + 1) * half] += jnp.sum(s[:, 0:1] * wm, axis=0, keepdims=True)
            modb_ref[1:2, n * half:(n + 1) * half] += jnp.sum(s[:, 1:2] * wm, axis=0, keepdims=True)

    @pl.when(bounded_ref[0] == 1)
    def _():
        units = [(c, g) for c in range(n_chunks) for g in range(len(streams))]

        def scores(c, g):
            return jnp.dot(k_ref[c * tk:(c + 1) * tk, :], qt_scr[g], preferred_element_type=F32)

        l = [zero_row for _ in streams]
        pending = [scores(*u) for u in units[:ATTN_LOOKAHEAD]]
        for n, (c, g) in enumerate(units):
            st = pending.pop(0)
            if n + ATTN_LOOKAHEAD < len(units):
                pending.append(scores(*units[n + ATTN_LOOKAHEAD]))
            p = jnp.exp2(st)
            l[g] = l[g] + jnp.sum(p, axis=0, keepdims=True)
            pv = jnp.dot(vt_ref[:, c * tk:(c + 1) * tk], p.astype(BF16), preferred_element_type=F32)
            if c == 0:
                acc_scr[g] = pv
            else:
                acc_scr[g] += pv
        finish(l)

    @pl.when(bounded_ref[0] == 0)
    def _():
        acc_scr[...] = jnp.zeros(acc_scr.shape, F32)

        def chunk(c, carry):
            m, l = carry
            r0 = pl.multiple_of(c * tk, tk)
            kc = k_ref[pl.ds(r0, tk), :]
            vtc = vt_ref[:, pl.ds(r0, tk)]
            m_out, l_out = [], []
            for g in range(len(streams)):
                st = jnp.dot(kc, qt_scr[g], preferred_element_type=F32)
                m_new = jnp.maximum(m[g], jnp.max(st, axis=0, keepdims=True))
                p = jnp.exp2(st - m_new)
                alpha = jnp.exp2(m[g] - m_new)
                l_out.append(alpha * l[g] + jnp.sum(p, axis=0, keepdims=True))
                acc_scr[g] = alpha * acc_scr[g] + jnp.dot(vtc, p.astype(BF16), preferred_element_type=F32)
                m_out.append(m_new)
            return tuple(m_out), tuple(l_out)

        init = (tuple(jnp.full((1, uq), -jnp.inf, F32) for _ in streams), tuple(zero_row for _ in streams))
        _, l = lax.fori_loop(0, n_chunks, chunk, init)
        finish(l)


def _attention(q, k, vt, bounded, cast_weights, cs, w_mod, b_mod, n_done):
    t = q.shape[0]
    s_len = k.shape[0]
    tq = ATTN_TQ
    tk = _largest_divisor(s_len, ATTN_TK_CAP, MXU_WIDTH)
    group_w = Q_PER_KV * HEAD_DIM
    n_i = t // tq
    steps = N_KV_HEADS * n_i
    step = lambda h, i: h * n_i + i
    n_rest = w_mod.shape[1] - n_done
    half = n_rest // 2
    assert n_done % half == 0 and w_mod.shape[0] % steps == 0
    mod_rows = w_mod.shape[0] // steps
    mod_specs = [
        pl.BlockSpec((mod_rows, 2), lambda h, i: (step(h, i), 0)),
        pl.BlockSpec((mod_rows, half), lambda h, i: (step(h, i), n_done // half)),
        pl.BlockSpec((mod_rows, half), lambda h, i: (step(h, i), n_done // half + 1)),
        pl.BlockSpec((1, n_rest), lambda h, i: (0, 0)),
    ]
    cast_specs, cast_shapes = [], []
    for w in cast_weights:
        assert w.shape[0] % (steps * BF16_SUBLANES) == 0, w.shape
        cast_specs.append(pl.BlockSpec((w.shape[0] // steps, w.shape[1]), lambda h, i: (step(h, i), 0)))
        cast_shapes.append(jax.ShapeDtypeStruct(w.shape, BF16))
    outs = pl.pallas_call(
        functools.partial(_attn_kernel, tk=tk, n_cast=len(cast_weights)),
        grid=(N_KV_HEADS, n_i),
        in_specs=[
            pl.BlockSpec(memory_space=pltpu.SMEM),
            pl.BlockSpec((tq, group_w), lambda h, i: (i, h)),
            pl.BlockSpec((s_len, HEAD_DIM), lambda h, i: (0, h), pipeline_mode=pl.Buffered(1)),
            pl.BlockSpec((HEAD_DIM, s_len), lambda h, i: (h, 0), pipeline_mode=pl.Buffered(1)),
        ] + mod_specs + cast_specs,
        out_specs=[pl.BlockSpec((tq, group_w), lambda h, i: (i, h)),
                   pl.BlockSpec((2, n_rest), lambda h, i: (0, 0))] + cast_specs,
        out_shape=[jax.ShapeDtypeStruct((t, ATT_WIDTH), BF16),
                   jax.ShapeDtypeStruct((2, n_rest), F32)] + cast_shapes,
        scratch_shapes=[
            pltpu.VMEM((Q_PER_KV * tq // ATTN_UNIT_Q, HEAD_DIM, ATTN_UNIT_Q), BF16),
            pltpu.VMEM((Q_PER_KV * tq // ATTN_UNIT_Q, HEAD_DIM, ATTN_UNIT_Q), F32),
        ],
        compiler_params=_cparams(("arbitrary", "arbitrary")),
        name="attention",
    )(bounded, q, k, vt, cs, w_mod, w_mod, b_mod[n_done:].reshape(1, n_rest), *cast_weights)
    return outs[0], outs[1], tuple(outs[2:])


def _log_sigmoid(x):
    return jnp.minimum(x, 0.0) - jnp.log1p(jnp.exp(-jnp.abs(x)))


def _scan(x, axis, op, ident, reverse):
    n = x.shape[axis]
    idx = lax.broadcasted_iota(jnp.int32, x.shape, axis)
    shift = 1
    while shift < n:
        if reverse:
            moved = pltpu.roll(x, n - shift, axis=axis)
            ok = idx < n - shift
        else:
            moved = pltpu.roll(x, shift, axis=axis)
            ok = idx >= shift
        x = op(x, jnp.where(ok, moved, ident))
        shift *= 2
    return x


class _MlstmDirection:
    def __init__(self, reverse, k_ref, qt_ref, vt_ref, gi_ref, gf_ref, git_ref, gft_ref, h_ref,
                 ct_scr, mrow_scr, mcol_scr, visible):
        self.reverse, self.visible = reverse, visible
        self.k_ref, self.qt_ref, self.vt_ref, self.h_ref = k_ref, qt_ref, vt_ref, h_ref
        self.gi_ref, self.gf_ref, self.git_ref, self.gft_ref = gi_ref, gf_ref, git_ref, gft_ref
        self.ct_scr, self.mrow_scr, self.mcol_scr = ct_scr, mrow_scr, mcol_scr
        self.units = [((M_HEADS if reverse else 0) + hd, hd) for hd in range(M_HEADS)]

    def _k(self, hd):
        return self.k_ref[:, hd * M_QK:(hd + 1) * M_QK]

    def _vt(self, hd):
        return self.vt_ref[hd * M_V:(hd + 1) * M_V, :]

    def state_free_matmuls(self):
        self.early = []
        for u, hd in self.units:
            qt = self.qt_ref[hd * M_QK:(hd + 1) * M_QK, :]
            state = self.ct_scr[u]
            qk_t = jnp.dot(self._k(hd), qt, preferred_element_type=F32)
            inter_t = jnp.dot(state.astype(BF16), qt, preferred_element_type=F32)
            self.early.append((state, qk_t, inter_t))

    def gate_terms(self, brow_ref, bcol_ref):
        add = lambda a, b: a + b
        reverse = self.reverse
        i_c = self.gi_ref[...] + brow_ref[0:1, :]
        ls_c = _log_sigmoid(self.gf_ref[...] + brow_ref[1:2, :])
        self.a_c = i_c - _scan(ls_c, 0, add, 0.0, reverse)
        m_prev_r = self.mrow_scr[0:1, 0:N_UNITS]
        mm_r = jnp.maximum(m_prev_r, jnp.max(self.a_c, axis=0, keepdims=True))
        self.kscale_c = jnp.exp(self.a_c - mm_r)
        self.w_prev_r = jnp.exp(m_prev_r - mm_r)
        self.mrow_scr[0:1, 0:N_UNITS] = jnp.sum(ls_c, axis=0, keepdims=True) + mm_r
        i_r = self.git_ref[...] + bcol_ref[:, 0:1]
        ls_r = _log_sigmoid(self.gft_ref[...] + bcol_ref[:, 1:2])
        cum_r = _scan(ls_r, 1, add, 0.0, reverse)
        a_r = i_r - cum_r
        m_prev_c = self.mcol_scr[0:N_UNITS, 0:1]
        m_t = cum_r + jnp.maximum(_scan(a_r, 1, jnp.maximum, -jnp.inf, reverse), m_prev_c)
        self.r_r = cum_r - m_t
        self.w_inter_r = jnp.exp(cum_r + m_prev_c - m_t)
        self.floor_r = jnp.exp(-m_t)
        self.mcol_scr[0:N_UNITS, 0:1] = (jnp.sum(ls_r, axis=1, keepdims=True)
                                         + jnp.maximum(m_prev_c, jnp.max(a_r, axis=1, keepdims=True)))

    def state_update(self):
        for (u, hd), (state, _, _) in zip(self.units, self.early):
            kw = self._k(hd).astype(F32) * self.kscale_c[:, u:u + 1]
            w_prev = self.w_prev_r[0:1, u:u + 1]
            self.ct_scr[u, 0:M_V, :] = (w_prev * state[0:M_V]
                                        + jnp.dot(self._vt(hd), kw.astype(BF16), preferred_element_type=F32))
            self.ct_scr[u, M_V:M_V + 1, :] = w_prev * state[M_V:M_V + 1] + jnp.sum(kw, axis=0, keepdims=True)

    def outputs(self):
        for (u, hd), (_, qk_t, inter_t) in zip(self.units, self.early):
            decay_t = jnp.exp(jnp.where(self.visible, self.a_c[:, u:u + 1], -jnp.inf) + self.r_r[u:u + 1, :])
            s_t = qk_t * decay_t
            w_inter = self.w_inter_r[u:u + 1, :]
            num_t = (inter_t[0:M_V] * w_inter
                     + jnp.dot(self._vt(hd), s_t.astype(BF16), preferred_element_type=F32))
            den = inter_t[M_V:M_V + 1] * w_inter + jnp.sum(s_t, axis=0, keepdims=True)
            self.h_ref[hd * M_V:(hd + 1) * M_V, :] = num_t * (
                1.0 / jnp.maximum(jnp.abs(den), self.floor_r[u:u + 1, :]))


def _mlstm_kernel(kf_ref, qtf_ref, vtf_ref, gif_ref, gff_ref, gitf_ref, gftf_ref,
                  kb_ref, qtb_ref, vtb_ref, gib_ref, gfb_ref, gitb_ref, gftb_ref,
                  brow_ref, bcol_ref, *rest, n_cast):
    cast_in, (hf_ref, hb_ref), rest = rest[:n_cast], rest[n_cast:n_cast + 2], rest[n_cast + 2:]
    cast_out, (ct_scr, mrow_scr, mcol_scr) = rest[:n_cast], rest[n_cast:]
    L = kf_ref.shape[0]

    @pl.when(pl.program_id(0) == 0)
    def _():
        ct_scr[...] = jnp.zeros(ct_scr.shape, F32)
        mrow_scr[...] = jnp.zeros(mrow_scr.shape, F32)
        mcol_scr[...] = jnp.zeros(mcol_scr.shape, F32)

    src = lax.broadcasted_iota(jnp.int32, (L, L), 0)
    tgt = lax.broadcasted_iota(jnp.int32, (L, L), 1)
    dirs = (
        _MlstmDirection(False, kf_ref, qtf_ref, vtf_ref, gif_ref, gff_ref, gitf_ref, gftf_ref, hf_ref,
                        ct_scr, mrow_scr.at[0], mcol_scr.at[0], src <= tgt),
        _MlstmDirection(True, kb_ref, qtb_ref, vtb_ref, gib_ref, gfb_ref, gitb_ref, gftb_ref, hb_ref,
                        ct_scr, mrow_scr.at[1], mcol_scr.at[1], src >= tgt),
    )
    for d in dirs:
        d.state_free_matmuls()
    for d in dirs:
        d.gate_terms(brow_ref, bcol_ref)
    for d in dirs:
        d.state_update()
    for d in dirs:
        d.outputs()
    for w_ref, o_ref in zip(cast_in, cast_out):
        o_ref[...] = w_ref[...].astype(BF16)


def _mlstm(mk, mq_t, mv_t, gates, b_i, b_f, n_x, cast_weights):
    L = MLSTM_CHUNK
    rows = mk.shape[0]
    assert rows % L == 0 and n_x % L == 0 and rows - n_x == L
    nx = n_x // L
    steps = nx + 1
    g4 = gates.reshape(rows, N_DIR, 2, M_HEADS)
    g_in = g4[:, :, 0, :].reshape(rows, N_UNITS)
    g_fg = g4[:, :, 1, :].reshape(rows, N_UNITS)
    bias_row = jnp.stack([b_i.reshape(N_UNITS), b_f.reshape(N_UNITS)], axis=0)
    f_chunk = lambda g: g
    b_chunk = lambda g: jnp.where(g == 0, 0, nx + 1 - g)
    f_out = lambda g: (0, jnp.maximum(g - 1, 0))
    b_out = lambda g: (0, jnp.where(g == 0, nx - 1, nx - g))
    f_out_rows = lambda g: (jnp.maximum(g - 1, 0), 0)
    small = lambda g: (0, 0)

    def stream(chunk):
        return [
            pl.BlockSpec((L, MQK_WIDTH), lambda g: (chunk(g), 0)),
            pl.BlockSpec((MQK_WIDTH, L), lambda g: (0, chunk(g))),
            pl.BlockSpec((M_WIDTH, L), lambda g: (0, chunk(g))),
            pl.BlockSpec((L, N_UNITS), lambda g: (chunk(g), 0)),
            pl.BlockSpec((L, N_UNITS), lambda g: (chunk(g), 0)),
            pl.BlockSpec((N_UNITS, L), lambda g: (0, chunk(g))),
            pl.BlockSpec((N_UNITS, L), lambda g: (0, chunk(g))),
        ]

    operands = (mk, mq_t, mv_t, g_in, g_fg, g_in.T, g_fg.T)
    cast_specs, cast_shapes = [], []
    for w in cast_weights:
        assert w.shape[0] % (nx * BF16_SUBLANES) == 0, w.shape
        blk = (w.shape[0] // nx, w.shape[1])
        cast_specs.append(pl.BlockSpec(blk, f_out_rows))
        cast_shapes.append(jax.ShapeDtypeStruct(w.shape, BF16))
    outs = pl.pallas_call(
        functools.partial(_mlstm_kernel, n_cast=len(cast_weights)),
        grid=(steps,),
        in_specs=stream(f_chunk) + stream(b_chunk) + [pl.BlockSpec((2, N_UNITS), small),
                                                       pl.BlockSpec((N_UNITS, 2), small)] + cast_specs,
        out_specs=[pl.BlockSpec((M_WIDTH, L), f_out), pl.BlockSpec((M_WIDTH, L), b_out)] + cast_specs,
        out_shape=[jax.ShapeDtypeStruct((M_WIDTH, n_x), F32)] * 2 + cast_shapes,
        scratch_shapes=[
            pltpu.VMEM((N_UNITS, M_V + BF16_SUBLANES, M_QK), F32),
            pltpu.VMEM((N_DIR, 8, LANES), F32),
            pltpu.VMEM((N_DIR, 8, LANES), F32),
        ],
        compiler_params=_cparams(("arbitrary",)),
        name="mlstm",
    )(*operands, *operands, bias_row, bias_row.T, *cast_weights)
    return outs[0], outs[1], tuple(outs[2:])


def _outproj_kernel(att_ref, hft_ref, hbt_ref, mo_ref, mg_ref, w_ref, x_ref, mod_ref, n2_ref,
                    x1_ref, h2_ref):
    d = x_ref.shape[1]
    tm = x_ref.shape[0]
    rows = tm // OUT_ROW_SPLIT

    def project(r0):
        y = jnp.dot(att_ref[r0:r0 + rows, :], w_ref[0:ATT_WIDTH, :], preferred_element_type=F32)
        for hd in range(M_HEADS):
            c = hd * M_V
            ht = hft_ref[c:c + M_V, r0:r0 + rows] + hbt_ref[c:c + M_V, r0:r0 + rows]
            ht = ht * lax.rsqrt(jnp.mean(ht * ht, axis=0, keepdims=True) + NORM_EPS) * mg_ref[c:c + M_V, :]
            r = (ht.T * mo_ref[r0:r0 + rows, c:c + M_V].astype(F32)).astype(BF16)
            y = y + jnp.dot(r, w_ref[ATT_WIDTH + c:ATT_WIDTH + c + M_V, :], preferred_element_type=F32)
        return y

    ys = [project(b * rows) for b in range(OUT_ROW_SPLIT)]
    for b, y in enumerate(ys):
        r0 = b * rows
        x1 = x_ref[r0:r0 + rows, :] + mod_ref[0:1, 0:d] * y
        x1_ref[r0:r0 + rows, :] = x1
        h2 = _rms(x1) * n2_ref[...] * (1.0 + mod_ref[0:1, 2 * d:3 * d]) + mod_ref[0:1, d:2 * d]
        h2_ref[r0:r0 + rows, :] = h2.astype(BF16)


def _out_proj(att, hf_t, hb_t, mo, m_gain, w_out, x, mod, norm2):
    t, d = x.shape
    tm = OUT_TM
    row = lambda i: (i, 0)
    full = lambda i: (0, 0)
    return pl.pallas_call(
        _outproj_kernel,
        grid=(t // tm,),
        in_specs=[
            pl.BlockSpec((tm, ATT_WIDTH), row),
            pl.BlockSpec((M_WIDTH, tm), lambda i: (0, i)),
            pl.BlockSpec((M_WIDTH, tm), lambda i: (0, i)),
            pl.BlockSpec((tm, M_WIDTH), row),
            pl.BlockSpec((M_WIDTH, 1), full),
            pl.BlockSpec(w_out.shape, full),
            pl.BlockSpec((tm, d), row),
            pl.BlockSpec(mod.shape, full),
            pl.BlockSpec((1, d), full),
        ],
        out_specs=[pl.BlockSpec((tm, d), row), pl.BlockSpec((tm, d), row)],
        out_shape=[jax.ShapeDtypeStruct((t, d), F32), jax.ShapeDtypeStruct((t, d), BF16)],
        compiler_params=_cparams(("arbitrary",)),
        name="out_proj",
    )(att, hf_t, hb_t, mo, m_gain, w_out, x, mod, norm2)


def _ffn_kernel(h_ref, hp_ref, hn_ref, wg_ref, wv_ref, cwg_ref, cwv_ref, cbg_ref, cbv_ref, wd_ref,
                x1_ref, mod_ref, nf_ref, o_ref, hext_ref, ug_ref, uv_ref):
    i = pl.program_id(0)
    j = pl.program_id(1)
    tm = h_ref.shape[0]
    d = x1_ref.shape[1]

    @pl.when(j == 0)
    def _():
        hext_ref[0:HALO, :] = jnp.where(i == 0, jnp.zeros_like(hp_ref[...]), hp_ref[...])
        hext_ref[HALO:HALO + tm, :] = h_ref[...]
        hext_ref[HALO + tm:, :] = jnp.where(i == pl.num_programs(0) - 1, jnp.zeros_like(hn_ref[...]), hn_ref[...])
        o_ref[...] = jnp.zeros(o_ref.shape, F32)

    split = tm // 2 + HALO
    for r0, r1 in ((0, split), (split, tm + 2 * HALO)):
        ug_ref[r0:r1, :] = jnp.dot(hext_ref[r0:r1, :], wg_ref[...], preferred_element_type=F32)
        uv_ref[r0:r1, :] = jnp.dot(hext_ref[r0:r1, :], wv_ref[...], preferred_element_type=F32)

    def conv(u_ref, cw_ref, cb_ref, r0, n):
        return (cw_ref[0:1, :] * u_ref[HALO - 1 + r0:HALO - 1 + r0 + n, :]
                + cw_ref[1:2, :] * u_ref[HALO + r0:HALO + r0 + n, :]
                + cw_ref[2:3, :] * u_ref[HALO + 1 + r0:HALO + 1 + r0 + n, :]
                + cb_ref[...])

    cut = tm // 2 - HALO
    for r0, n in ((0, cut), (cut, tm - cut)):
        g = conv(ug_ref, cwg_ref, cbg_ref, r0, n)
        val = conv(uv_ref, cwv_ref, cbv_ref, r0, n)
        a = (g * jax.nn.sigmoid(g) * val).astype(BF16)
        o_ref[r0:r0 + n, :] += jnp.dot(a, wd_ref[...], preferred_element_type=F32)

    @pl.when(j == pl.num_programs(1) - 1)
    def _():
        y = x1_ref[...] + mod_ref[0:1, 3 * d:4 * d] * o_ref[...]
        o_ref[...] = _rms(y) * nf_ref[...]


def _conv_ffn(h2, w_up, conv_w, conv_b, w_down, x1, mod, norm_f):
    t, d = h2.shape
    d_ff = w_down.shape[0]
    tm, tf = FFN_TM, FFN_TF
    nf = d_ff // tf
    hb = tm // HALO
    last_halo = t // HALO - 1
    in_specs = [
        pl.BlockSpec((tm, d), lambda i, j: (i, 0)),
        pl.BlockSpec((HALO, d), lambda i, j: (jnp.maximum(i * hb - 1, 0), 0)),
        pl.BlockSpec((HALO, d), lambda i, j: (jnp.minimum((i + 1) * hb, last_halo), 0)),
        pl.BlockSpec((d, tf), lambda i, j: (0, j)),
        pl.BlockSpec((d, tf), lambda i, j: (0, j + nf)),
        pl.BlockSpec((3, tf), lambda i, j: (0, j)),
        pl.BlockSpec((3, tf), lambda i, j: (0, j + nf)),
        pl.BlockSpec((1, tf), lambda i, j: (0, j)),
        pl.BlockSpec((1, tf), lambda i, j: (0, j + nf)),
        pl.BlockSpec((tf, d), lambda i, j: (j, 0)),
        pl.BlockSpec((tm, d), lambda i, j: (i, 0)),
        pl.BlockSpec(mod.shape, lambda i, j: (0, 0)),
        pl.BlockSpec((1, d), lambda i, j: (0, 0)),
    ]
    return pl.pallas_call(
        _ffn_kernel,
        grid=(t // tm, nf),
        in_specs=in_specs,
        out_specs=pl.BlockSpec((tm, d), lambda i, j: (i, 0)),
        out_shape=jax.ShapeDtypeStruct((t, d), F32),
        scratch_shapes=[
            pltpu.VMEM((tm + 2 * HALO, d), BF16),
            pltpu.VMEM((tm + 2 * HALO, tf), F32),
            pltpu.VMEM((tm + 2 * HALO, tf), F32),
        ],
        compiler_params=_cparams(("arbitrary", "arbitrary")),
        name="conv_ffn",
    )(h2, h2, h2, w_up, w_up, conv_w, conv_w, conv_b, conv_b, w_down, x1, mod, norm_f)


def _rope_tables(n_ctx, n_tok):
    f32 = np.float32
    rows = n_tok // GRID_W
    row = np.repeat(np.arange(rows, dtype=f32), GRID_W)
    col = np.tile(np.arange(GRID_W, dtype=f32), rows)
    inv_freq = np.power(f32(ROPE_THETA), -np.arange(0, AXIS_DIM, 2, dtype=f32) / f32(AXIS_DIM)).astype(f32)
    ang_r = row[:, None] * inv_freq[None, :]
    ang_c = col[:, None] * inv_freq[None, :]
    cos = np.concatenate([np.cos(ang_r), np.cos(ang_r), np.cos(ang_c), np.cos(ang_c)], axis=1)
    sin = np.concatenate([-np.sin(ang_r), np.sin(ang_r), -np.sin(ang_c), np.sin(ang_c)], axis=1)
    cos = np.concatenate([np.ones((n_ctx, HEAD_DIM), f32), cos], axis=0).astype(f32)
    sin = np.concatenate([np.zeros((n_ctx, HEAD_DIM), f32), sin], axis=0).astype(f32)
    return jnp.asarray(cos), jnp.asarray(sin)


def kernel(x, c, ctx, c_ctx, w_mod, b_mod, norm1, w_in, q_norm, k_norm, b_igate, b_fgate, m_norm,
           w_out, norm2, w_up, conv_w, conv_b, w_down, norm_f):
    batch, n_tok, d = x.shape
    assert batch == 1 and w_mod.shape[0] == 1
    n_ctx = ctx.shape[1]
    x2 = x[0]
    ctx2 = ctx[0]

    cs = jnp.stack([c[0], c_ctx], axis=1)
    mod = _modulation(cs, w_mod[0], b_mod[0], 2 * d)

    w_in_t = w_in[0].T
    w_main = _cast_main_columns(w_in_t, OFF_G)
    w_gate = _gate_weights(w_in_t, OFF_G)
    cos_t, sin_t = _rope_tables(n_ctx, n_tok)
    q_gain = (q_norm[0] * (HEAD_DIM ** -0.5 * LOG2E)).reshape(1, HEAD_DIM)
    k_gain = k_norm[0].reshape(1, HEAD_DIM)
    q, k, vt, mq_t, mk, mv_t, mo, gates = _in_proj(
        x2, ctx2, mod, norm1[0].reshape(1, d), w_main, w_gate, q_gain, k_gain, cos_t, sin_t)

    score_bound = HEAD_DIM * jnp.max(jnp.abs(q_gain)) * jnp.max(jnp.abs(k_gain)) * ATTN_BOUND_HEADROOM
    bounded = (score_bound <= ATTN_SCORE_BOUND).astype(jnp.int32).reshape(1)
    att, mod_rest, (w_out_b, w_up_b, w_down_b) = _attention(
        q, k, vt, bounded, (w_out[0], w_up[0], w_down[0]), cs, w_mod[0], b_mod[0], 2 * d)

    hf_t, hb_t, _ = _mlstm(mk, mq_t, mv_t, gates, b_igate[0], b_fgate[0], n_tok, ())

    x1, h2 = _out_proj(att, hf_t, hb_t, mo, m_norm[0].reshape(M_WIDTH, 1), w_out_b, x2, mod_rest,
                       norm2[0].reshape(1, d))

    out = _conv_ffn(h2, w_up_b, conv_w[0], conv_b[0].reshape(1, -1), w_down_b, x1, mod_rest, norm_f.reshape(1, d))
    return out[None]
```

```python
import functools
import math

import jax
import jax.numpy as jnp
import numpy as np
from jax import lax
from jax.experimental import pallas as pl
from jax.experimental.pallas import tpu as pltpu

F32 = jnp.float32
BF16 = jnp.bfloat16

GRID_W = 64
HEAD_DIM = 128
N_Q_HEADS = 8
N_KV_HEADS = 2
Q_PER_KV = N_Q_HEADS // N_KV_HEADS
AXIS_DIM = HEAD_DIM // 2
ROPE_THETA = 10000.0
M_HEADS = 4
M_V = 256
M_QK = 128
N_DIR = 2
NORM_EPS = 1e-6
ATT_WIDTH = N_Q_HEADS * HEAD_DIM
KV_WIDTH = N_KV_HEADS * HEAD_DIM
M_WIDTH = M_HEADS * M_V
MQK_WIDTH = M_HEADS * M_QK
N_GATES = N_DIR * 2 * M_HEADS
N_UNITS = N_DIR * M_HEADS

OFF_AQ = 0
OFF_AK = OFF_AQ + ATT_WIDTH
OFF_AV = OFF_AK + KV_WIDTH
OFF_MQ = OFF_AV + KV_WIDTH
OFF_MK = OFF_MQ + MQK_WIDTH
OFF_MV = OFF_MK + MQK_WIDTH
OFF_MO = OFF_MV + M_WIDTH
OFF_G = OFF_MO + M_WIDTH

LANES = 128
MXU_WIDTH = 256
BF16_SUBLANES = 16
VMEM_LIMIT = 50 * 1024 * 1024

ROW_TILE = 256
MLSTM_CHUNK = 256
ATTN_TQ = 512
ATTN_SCORE_BOUND = 64.0
ATTN_BOUND_HEADROOM = 1.02
ATTN_UNIT_Q = 256
ATTN_TK_CAP = 768
ATTN_LOOKAHEAD = 2
OUT_TM = 512
OUT_ROW_SPLIT = 2
FFN_TM = 512
FFN_TF = 512
HALO = BF16_SUBLANES
MOD_TK = 128
WCAST_TN = 512
LOG2E = 1.4426950408889634


def _largest_divisor(n, cap, mult):
    best = None
    for d in range(mult, cap + 1, mult):
        if n % d == 0:
            best = d
    assert best is not None, (n, cap, mult)
    return best


def _cparams(sem, vmem=VMEM_LIMIT):
    return pltpu.CompilerParams(dimension_semantics=sem, vmem_limit_bytes=vmem)


def _rms(x, eps=NORM_EPS):
    return x * lax.rsqrt(jnp.mean(x * x, axis=-1, keepdims=True) + eps)


def _mod_kernel(cs_ref, w_ref, b_ref, o_ref):
    @pl.when(pl.program_id(0) == 0)
    def _():
        o_ref[0:1, :] = b_ref[...]
        o_ref[1:2, :] = b_ref[...]

    cs = cs_ref[...]
    s = cs * jax.nn.sigmoid(cs)
    w = w_ref[...]
    o_ref[0:1, :] += jnp.sum(s[:, 0:1] * w, axis=0, keepdims=True)
    o_ref[1:2, :] += jnp.sum(s[:, 1:2] * w, axis=0, keepdims=True)


def _modulation(cs, w_mod, b_mod, n):
    d = w_mod.shape[0]
    tk = MOD_TK
    return pl.pallas_call(
        _mod_kernel,
        grid=(d // tk,),
        in_specs=[
            pl.BlockSpec((tk, 2), lambda j: (j, 0)),
            pl.BlockSpec((tk, n), lambda j: (j, 0)),
            pl.BlockSpec((1, n), lambda j: (0, 0)),
        ],
        out_specs=pl.BlockSpec((2, n), lambda j: (0, 0)),
        out_shape=jax.ShapeDtypeStruct((2, n), F32),
        compiler_params=_cparams(("arbitrary",)),
        name="modulation",
    )(cs, w_mod, b_mod.reshape(1, -1))


def _gate_weight_kernel(wt_ref, o_ref):
    g = wt_ref[...]
    hi = g.astype(BF16)
    lo = (g - hi.astype(F32)).astype(BF16)
    pad = jnp.zeros((o_ref.shape[0] - 2 * N_GATES, g.shape[1]), BF16)
    o_ref[...] = jnp.concatenate([hi, lo, pad], axis=0)


def _gate_weights(w_t, row0):
    n, d = w_t.shape
    assert row0 % N_GATES == 0 and row0 + N_GATES == n
    return pl.pallas_call(
        _gate_weight_kernel,
        grid=(1,),
        in_specs=[pl.BlockSpec((N_GATES, d), lambda i: (row0 // N_GATES, 0))],
        out_specs=pl.BlockSpec((LANES, d), lambda i: (0, 0)),
        out_shape=jax.ShapeDtypeStruct((LANES, d), BF16),
        compiler_params=_cparams(("arbitrary",)),
        name="w_gate_split",
    )(w_t)


def _wcast_kernel(wt_ref, o_ref):
    o_ref[...] = wt_ref[...].T.astype(BF16)


def _cast_main_columns(w_t, n_main):
    n, d = w_t.shape
    tn = WCAST_TN
    assert n_main % tn == 0 and n_main <= n
    return pl.pallas_call(
        _wcast_kernel,
        grid=(n_main // tn,),
        in_specs=[pl.BlockSpec((tn, d), lambda j: (j, 0))],
        out_specs=pl.BlockSpec((d, tn), lambda j: (0, j)),
        out_shape=jax.ShapeDtypeStruct((d, n_main), BF16),
        compiler_params=_cparams(("arbitrary",)),
        name="w_in_cast",
    )(w_t)


def _swap_rope_halves(y):
    lane = lax.broadcasted_iota(jnp.int32, y.shape, 1)
    fwd = pltpu.roll(y, LANES - AXIS_DIM // 2, axis=1)
    bwd = pltpu.roll(y, AXIS_DIM // 2, axis=1)
    return jnp.where((lane & (AXIS_DIM // 2)) == 0, fwd, bwd)


def _inproj_kernel(x_ref, ctx_ref, mod_ref, n1_ref, w_ref, wg_ref, qg_ref, kg_ref, cos_ref, sin_ref,
                   q_ref, k_ref, vt_ref, mqt_ref, mk_ref, mvt_ref, mo_ref, g_ref):
    d = x_ref.shape[1]
    is_ctx = pl.program_id(0) == 0
    xin = jnp.where(is_ctx, ctx_ref[...], x_ref[...])
    shift = jnp.where(is_ctx, mod_ref[1:2, 0:d], mod_ref[0:1, 0:d])
    scale = jnp.where(is_ctx, mod_ref[1:2, d:2 * d], mod_ref[0:1, d:2 * d])
    hf = _rms(xin) * n1_ref[...] * (1.0 + scale) + shift
    h_hi = hf.astype(BF16)

    def proj(c0, width):
        return jnp.dot(h_hi, w_ref[:, c0:c0 + width], preferred_element_type=F32)

    cos = cos_ref[...]
    sin = sin_ref[...]

    def norm_rope(y, gain):
        yn = _rms(y) * gain
        return yn * cos + _swap_rope_halves(yn) * sin

    for j in range(ATT_WIDTH // MXU_WIDTH):
        y = proj(OFF_AQ + j * MXU_WIDTH, MXU_WIDTH)
        for t in range(MXU_WIDTH // HEAD_DIM):
            c = j * MXU_WIDTH + t * HEAD_DIM
            q_ref[:, c:c + HEAD_DIM] = norm_rope(y[:, t * HEAD_DIM:(t + 1) * HEAD_DIM], qg_ref[...]).astype(BF16)
    y = proj(OFF_AK, KV_WIDTH)
    for t in range(N_KV_HEADS):
        k_ref[:, t * HEAD_DIM:(t + 1) * HEAD_DIM] = norm_rope(
            y[:, t * HEAD_DIM:(t + 1) * HEAD_DIM], kg_ref[...]).astype(BF16)
    vt_ref[...] = proj(OFF_AV, KV_WIDTH).T.astype(BF16)
    for j in range(MQK_WIDTH // MXU_WIDTH):
        c = j * MXU_WIDTH
        mqt_ref[c:c + MXU_WIDTH, :] = proj(OFF_MQ + c, MXU_WIDTH).T.astype(BF16)
        mk_ref[:, c:c + MXU_WIDTH] = (proj(OFF_MK + c, MXU_WIDTH) * (M_QK ** -0.5)).astype(BF16)
    for j in range(M_WIDTH // MXU_WIDTH):
        c = j * MXU_WIDTH
        mvt_ref[c:c + MXU_WIDTH, :] = proj(OFF_MV + c, MXU_WIDTH).T.astype(BF16)
        mo_ref[:, c:c + MXU_WIDTH] = jax.nn.sigmoid(proj(OFF_MO + c, MXU_WIDTH)).astype(BF16)
    nt = (((1,), (1,)), ((), ()))
    gg = lax.dot_general(h_hi, wg_ref[...], nt, preferred_element_type=F32)
    gg = gg + pltpu.roll(gg, LANES - N_GATES, axis=1)
    g_ref[...] = gg[:, 0:N_GATES]


def _in_proj(x, ctx, mod, norm1, w_main, w_gate, q_gain, k_gain, cos_t, sin_t):
    t, d = x.shape
    n_ctx = ctx.shape[0]
    assert n_ctx == ROW_TILE and t % ROW_TILE == 0
    rows = n_ctx + t
    steps = rows // ROW_TILE
    full = lambda i: (0, 0)
    row_all = lambda i: (i, 0)
    row_x = lambda i: (jnp.maximum(i - 1, 0), 0)
    out_shape = [
        jax.ShapeDtypeStruct((t, ATT_WIDTH), BF16),
        jax.ShapeDtypeStruct((rows, KV_WIDTH), BF16),
        jax.ShapeDtypeStruct((KV_WIDTH, rows), BF16),
        jax.ShapeDtypeStruct((MQK_WIDTH, rows), BF16),
        jax.ShapeDtypeStruct((rows, MQK_WIDTH), BF16),
        jax.ShapeDtypeStruct((M_WIDTH, rows), BF16),
        jax.ShapeDtypeStruct((t, M_WIDTH), BF16),
        jax.ShapeDtypeStruct((rows, N_GATES), F32),
    ]
    out_specs = [
        pl.BlockSpec((ROW_TILE, ATT_WIDTH), row_x),
        pl.BlockSpec((ROW_TILE, KV_WIDTH), row_all),
        pl.BlockSpec((KV_WIDTH, ROW_TILE), lambda i: (0, i)),
        pl.BlockSpec((MQK_WIDTH, ROW_TILE), lambda i: (0, i)),
        pl.BlockSpec((ROW_TILE, MQK_WIDTH), row_all),
        pl.BlockSpec((M_WIDTH, ROW_TILE), lambda i: (0, i)),
        pl.BlockSpec((ROW_TILE, M_WIDTH), row_x),
        pl.BlockSpec((ROW_TILE, N_GATES), row_all),
    ]
    in_specs = [
        pl.BlockSpec((ROW_TILE, d), row_x),
        pl.BlockSpec((ROW_TILE, d), full),
        pl.BlockSpec(mod.shape, full),
        pl.BlockSpec((1, d), full),
        pl.BlockSpec(w_main.shape, full),
        pl.BlockSpec(w_gate.shape, full),
        pl.BlockSpec((1, HEAD_DIM), full),
        pl.BlockSpec((1, HEAD_DIM), full),
        pl.BlockSpec((ROW_TILE, HEAD_DIM), row_all),
        pl.BlockSpec((ROW_TILE, HEAD_DIM), row_all),
    ]
    return pl.pallas_call(
        _inproj_kernel,
        grid=(steps,),
        in_specs=in_specs,
        out_specs=out_specs,
        out_shape=out_shape,
        compiler_params=_cparams(("arbitrary",)),
        name="in_proj",
    )(x, ctx, mod, norm1, w_main, w_gate, q_gain, k_gain, cos_t, sin_t)


def _attn_kernel(bounded_ref, q_ref, k_ref, vt_ref, cs_ref, wm_a_ref, wm_b_ref, bm_ref, *rest, tk, n_cast):
    cast_in, (o_ref, modb_ref), rest = rest[:n_cast], rest[n_cast:n_cast + 2], rest[n_cast + 2:]
    cast_out, (qt_scr, acc_scr) = rest[:n_cast], rest[n_cast:]
    uq = ATTN_UNIT_Q

    @pl.when(jnp.logical_and(pl.program_id(0) == 0, pl.program_id(1) == 0))
    def _():
        modb_ref[0:1, :] = bm_ref[...]
        modb_ref[1:2, :] = bm_ref[...]

    n_sub = q_ref.shape[0] // uq
    n_chunks = k_ref.shape[0] // tk
    streams = [(r, h) for r in range(n_sub) for h in range(Q_PER_KV)]
    for g, (r, h) in enumerate(streams):
        qt_scr[g] = q_ref[r * uq:(r + 1) * uq, h * HEAD_DIM:(h + 1) * HEAD_DIM].astype(F32).T.astype(BF16)
    zero_row = jnp.zeros((1, uq), F32)

    def finish(l):
        for g, (r, h) in enumerate(streams):
            out = (acc_scr[g] / l[g]).T
            o_ref[r * uq:(r + 1) * uq, h * HEAD_DIM:(h + 1) * HEAD_DIM] = out.astype(o_ref.dtype)
        for w_ref, wo_ref in zip(cast_in, cast_out):
            wo_ref[...] = w_ref[...].astype(BF16)
        cs = cs_ref[...]
        s = cs * jax.nn.sigmoid(cs)
        half = wm_a_ref.shape[1]
        for n, wm_ref in enumerate((wm_a_ref, wm_b_ref)):
            wm = wm_ref[...]
            modb_ref[0:1, n * half:(n + 1) * half] += jnp.sum(s[:, 0:1] * wm, axis=0, keepdims=True)
            modb_ref[1:2, n * half:(n + 1) * half] += jnp.sum(s[:, 1:2] * wm, axis=0, keepdims=True)

    @pl.when(bounded_ref[0] == 1)
    def _():
        units = [(c, g) for c in range(n_chunks) for g in range(len(streams))]

        def scores(c, g):
            return jnp.dot(k_ref[c * tk:(c + 1) * tk, :], qt_scr[g], preferred_element_type=F32)

        l = [zero_row for _ in streams]
        pending = [scores(*u) for u in units[:ATTN_LOOKAHEAD]]
        for n, (c, g) in enumerate(units):
            st = pending.pop(0)
            if n + ATTN_LOOKAHEAD < len(units):
                pending.append(scores(*units[n + ATTN_LOOKAHEAD]))
            p = jnp.exp2(st)
            l[g] = l[g] + jnp.sum(p, axis=0, keepdims=True)
            pv = jnp.dot(vt_ref[:, c * tk:(c + 1) * tk], p.astype(BF16), preferred_element_type=F32)
            if c == 0:
                acc_scr[g] = pv
            else:
                acc_scr[g] += pv
        finish(l)

    @pl.when(bounded_ref[0] == 0)
    def _():
        acc_scr[...] = jnp.zeros(acc_scr.shape, F32)

        def chunk(c, carry):
            m, l = carry
            r0 = pl.multiple_of(c * tk, tk)
            kc = k_ref[pl.ds(r0, tk), :]
            vtc = vt_ref[:, pl.ds(r0, tk)]
            m_out, l_out = [], []
            for g in range(len(streams)):
                st = jnp.dot(kc, qt_scr[g], preferred_element_type=F32)
                m_new = jnp.maximum(m[g], jnp.max(st, axis=0, keepdims=True))
                p = jnp.exp2(st - m_new)
                alpha = jnp.exp2(m[g] - m_new)
                l_out.append(alpha * l[g] + jnp.sum(p, axis=0, keepdims=True))
                acc_scr[g] = alpha * acc_scr[g] + jnp.dot(vtc, p.astype(BF16), preferred_element_type=F32)
                m_out.append(m_new)
            return tuple(m_out), tuple(l_out)

        init = (tuple(jnp.full((1, uq), -jnp.inf, F32) for _ in streams), tuple(zero_row for _ in streams))
        _, l = lax.fori_loop(0, n_chunks, chunk, init)
        finish(l)


def _attention(q, k, vt, bounded, cast_weights, cs, w_mod, b_mod, n_done):
    t = q.shape[0]
    s_len = k.shape[0]
    tq = ATTN_TQ
    tk = _largest_divisor(s_len, ATTN_TK_CAP, MXU_WIDTH)
    group_w = Q_PER_KV * HEAD_DIM
    n_i = t // tq
    steps = N_KV_HEADS * n_i
    step = lambda h, i: h * n_i + i
    n_rest = w_mod.shape[1] - n_done
    half = n_rest // 2
    assert n_done % half == 0 and w_mod.shape[0] % steps == 0
    mod_rows = w_mod.shape[0] // steps
    mod_specs = [
        pl.BlockSpec((mod_rows, 2), lambda h, i: (step(h, i), 0)),
        pl.BlockSpec((mod_rows, half), lambda h, i: (step(h, i), n_done // half)),
        pl.BlockSpec((mod_rows, half), lambda h, i: (step(h, i), n_done // half + 1)),
        pl.BlockSpec((1, n_rest), lambda h, i: (0, 0)),
    ]
    cast_specs, cast_shapes = [], []
    for w in cast_weights:
        assert w.shape[0] % (steps * BF16_SUBLANES) == 0, w.shape
        cast_specs.append(pl.BlockSpec((w.shape[0] // steps, w.shape[1]), lambda h, i: (step(h, i), 0)))
        cast_shapes.append(jax.ShapeDtypeStruct(w.shape, BF16))
    outs = pl.pallas_call(
        functools.partial(_attn_kernel, tk=tk, n_cast=len(cast_weights)),
        grid=(N_KV_HEADS, n_i),
        in_specs=[
            pl.BlockSpec(memory_space=pltpu.SMEM),
            pl.BlockSpec((tq, group_w), lambda h, i: (i, h)),
            pl.BlockSpec((s_len, HEAD_DIM), lambda h, i: (0, h), pipeline_mode=pl.Buffered(1)),
            pl.BlockSpec((HEAD_DIM, s_len), lambda h, i: (h, 0), pipeline_mode=pl.Buffered(1)),
        ] + mod_specs + cast_specs,
        out_specs=[pl.BlockSpec((tq, group_w), lambda h, i: (i, h)),
                   pl.BlockSpec((2, n_rest), lambda h, i: (0, 0))] + cast_specs,
        out_shape=[jax.ShapeDtypeStruct((t, ATT_WIDTH), BF16),
                   jax.ShapeDtypeStruct((2, n_rest), F32)] + cast_shapes,
        scratch_shapes=[
            pltpu.VMEM((Q_PER_KV * tq // ATTN_UNIT_Q, HEAD_DIM, ATTN_UNIT_Q), BF16),
            pltpu.VMEM((Q_PER_KV * tq // ATTN_UNIT_Q, HEAD_DIM, ATTN_UNIT_Q), F32),
        ],
        compiler_params=_cparams(("arbitrary", "arbitrary")),
        name="attention",
    )(bounded, q, k, vt, cs, w_mod, w_mod, b_mod[n_done:].reshape(1, n_rest), *cast_weights)
    return outs[0], outs[1], tuple(outs[2:])


def _log_sigmoid(x):
    return jnp.minimum(x, 0.0) - jnp.log1p(jnp.exp(-jnp.abs(x)))


def _scan(x, axis, op, ident, reverse):
    n = x.shape[axis]
    idx = lax.broadcasted_iota(jnp.int32, x.shape, axis)
    shift = 1
    while shift < n:
        if reverse:
            moved = pltpu.roll(x, n - shift, axis=axis)
            ok = idx < n - shift
        else:
            moved = pltpu.roll(x, shift, axis=axis)
            ok = idx >= shift
        x = op(x, jnp.where(ok, moved, ident))
        shift *= 2
    return x


class _MlstmDirection:
    def __init__(self, reverse, k_ref, qt_ref, vt_ref, gi_ref, gf_ref, git_ref, gft_ref, h_ref,
                 ct_scr, mrow_scr, mcol_scr, visible):
        self.reverse, self.visible = reverse, visible
        self.k_ref, self.qt_ref, self.vt_ref, self.h_ref = k_ref, qt_ref, vt_ref, h_ref
        self.gi_ref, self.gf_ref, self.git_ref, self.gft_ref = gi_ref, gf_ref, git_ref, gft_ref
        self.ct_scr, self.mrow_scr, self.mcol_scr = ct_scr, mrow_scr, mcol_scr
        self.units = [((M_HEADS if reverse else 0) + hd, hd) for hd in range(M_HEADS)]

    def _k(self, hd):
        return self.k_ref[:, hd * M_QK:(hd + 1) * M_QK]

    def _vt(self, hd):
        return self.vt_ref[hd * M_V:(hd + 1) * M_V, :]

    def state_free_matmuls(self):
        self.early = []
        for u, hd in self.units:
            qt = self.qt_ref[hd * M_QK:(hd + 1) * M_QK, :]
            state = self.ct_scr[u]
            qk_t = jnp.dot(self._k(hd), qt, preferred_element_type=F32)
            inter_t = jnp.dot(state.astype(BF16), qt, preferred_element_type=F32)
            self.early.append((state, qk_t, inter_t))

    def gate_terms(self, brow_ref, bcol_ref):
        add = lambda a, b: a + b
        reverse = self.reverse
        i_c = self.gi_ref[...] + brow_ref[0:1, :]
        ls_c = _log_sigmoid(self.gf_ref[...] + brow_ref[1:2, :])
        self.a_c = i_c - _scan(ls_c, 0, add, 0.0, reverse)
        m_prev_r = self.mrow_scr[0:1, 0:N_UNITS]
        mm_r = jnp.maximum(m_prev_r, jnp.max(self.a_c, axis=0, keepdims=True))
        self.kscale_c = jnp.exp(self.a_c - mm_r)
        self.w_prev_r = jnp.exp(m_prev_r - mm_r)
        self.mrow_scr[0:1, 0:N_UNITS] = jnp.sum(ls_c, axis=0, keepdims=True) + mm_r
        i_r = self.git_ref[...] + bcol_ref[:, 0:1]
        ls_r = _log_sigmoid(self.gft_ref[...] + bcol_ref[:, 1:2])
        cum_r = _scan(ls_r, 1, add, 0.0, reverse)
        a_r = i_r - cum_r
        m_prev_c = self.mcol_scr[0:N_UNITS, 0:1]
        m_t = cum_r + jnp.maximum(_scan(a_r, 1, jnp.maximum, -jnp.inf, reverse), m_prev_c)
        self.r_r = cum_r - m_t
        self.w_inter_r = jnp.exp(cum_r + m_prev_c - m_t)
        self.floor_r = jnp.exp(-m_t)
        self.mcol_scr[0:N_UNITS, 0:1] = (jnp.sum(ls_r, axis=1, keepdims=True)
                                         + jnp.maximum(m_prev_c, jnp.max(a_r, axis=1, keepdims=True)))

    def state_update(self):
        for (u, hd), (state, _, _) in zip(self.units, self.early):
            kw = self._k(hd).astype(F32) * self.kscale_c[:, u:u + 1]
            w_prev = self.w_prev_r[0:1, u:u + 1]
            self.ct_scr[u, 0:M_V, :] = (w_prev * state[0:M_V]
                                        + jnp.dot(self._vt(hd), kw.astype(BF16), preferred_element_type=F32))
            self.ct_scr[u, M_V:M_V + 1, :] = w_prev * state[M_V:M_V + 1] + jnp.sum(kw, axis=0, keepdims=True)

    def outputs(self):
        for (u, hd), (_, qk_t, inter_t) in zip(self.units, self.early):
            decay_t = jnp.exp(jnp.where(self.visible, self.a_c[:, u:u + 1], -jnp.inf) + self.r_r[u:u + 1, :])
            s_t = qk_t * decay_t
            w_inter = self.w_inter_r[u:u + 1, :]
            num_t = (inter_t[0:M_V] * w_inter
                     + jnp.dot(self._vt(hd), s_t.astype(BF16), preferred_element_type=F32))
            den = inter_t[M_V:M_V + 1] * w_inter + jnp.sum(s_t, axis=0, keepdims=True)
            self.h_ref[hd * M_V:(hd + 1) * M_V, :] = (num_t * (
                1.0 / jnp.maximum(jnp.abs(den), self.floor_r[u:u + 1, :]))).astype(self.h_ref.dtype)


def _mlstm_kernel(kf_ref, qtf_ref, vtf_ref, gif_ref, gff_ref, gitf_ref, gftf_ref,
                  kb_ref, qtb_ref, vtb_ref, gib_ref, gfb_ref, gitb_ref, gftb_ref,
                  brow_ref, bcol_ref, *rest, n_cast):
    cast_in, (hf_ref, hb_ref), rest = rest[:n_cast], rest[n_cast:n_cast + 2], rest[n_cast + 2:]
    cast_out, (ct_scr, mrow_scr, mcol_scr) = rest[:n_cast], rest[n_cast:]
    L = kf_ref.shape[0]

    @pl.when(pl.program_id(0) == 0)
    def _():
        ct_scr[...] = jnp.zeros(ct_scr.shape, F32)
        mrow_scr[...] = jnp.zeros(mrow_scr.shape, F32)
        mcol_scr[...] = jnp.zeros(mcol_scr.shape, F32)

    src = lax.broadcasted_iota(jnp.int32, (L, L), 0)
    tgt = lax.broadcasted_iota(jnp.int32, (L, L), 1)
    dirs = (
        _MlstmDirection(False, kf_ref, qtf_ref, vtf_ref, gif_ref, gff_ref, gitf_ref, gftf_ref, hf_ref,
                        ct_scr, mrow_scr.at[0], mcol_scr.at[0], src <= tgt),
        _MlstmDirection(True, kb_ref, qtb_ref, vtb_ref, gib_ref, gfb_ref, gitb_ref, gftb_ref, hb_ref,
                        ct_scr, mrow_scr.at[1], mcol_scr.at[1], src >= tgt),
    )
    for d in dirs:
        d.state_free_matmuls()
    for d in dirs:
        d.gate_terms(brow_ref, bcol_ref)
    for d in dirs:
        d.state_update()
    for d in dirs:
        d.outputs()
    for w_ref, o_ref in zip(cast_in, cast_out):
        o_ref[...] = w_ref[...].astype(BF16)


def _mlstm(mk, mq_t, mv_t, gates, b_i, b_f, n_x, cast_weights):
    L = MLSTM_CHUNK
    rows = mk.shape[0]
    assert rows % L == 0 and n_x % L == 0 and rows - n_x == L
    nx = n_x // L
    steps = nx + 1
    g4 = gates.reshape(rows, N_DIR, 2, M_HEADS)
    g_in = g4[:, :, 0, :].reshape(rows, N_UNITS)
    g_fg = g4[:, :, 1, :].reshape(rows, N_UNITS)
    bias_row = jnp.stack([b_i.reshape(N_UNITS), b_f.reshape(N_UNITS)], axis=0)
    f_chunk = lambda g: g
    b_chunk = lambda g: jnp.where(g == 0, 0, nx + 1 - g)
    f_out = lambda g: (0, jnp.maximum(g - 1, 0))
    b_out = lambda g: (0, jnp.where(g == 0, nx - 1, nx - g))
    f_out_rows = lambda g: (jnp.maximum(g - 1, 0), 0)
    small = lambda g: (0, 0)

    def stream(chunk):
        return [
            pl.BlockSpec((L, MQK_WIDTH), lambda g: (chunk(g), 0)),
            pl.BlockSpec((MQK_WIDTH, L), lambda g: (0, chunk(g))),
            pl.BlockSpec((M_WIDTH, L), lambda g: (0, chunk(g))),
            pl.BlockSpec((L, N_UNITS), lambda g: (chunk(g), 0)),
            pl.BlockSpec((L, N_UNITS), lambda g: (chunk(g), 0)),
            pl.BlockSpec((N_UNITS, L), lambda g: (0, chunk(g))),
            pl.BlockSpec((N_UNITS, L), lambda g: (0, chunk(g))),
        ]

    operands = (mk, mq_t, mv_t, g_in, g_fg, g_in.T, g_fg.T)
    cast_specs, cast_shapes = [], []
    for w in cast_weights:
        assert w.shape[0] % (nx * BF16_SUBLANES) == 0, w.shape
        blk = (w.shape[0] // nx, w.shape[1])
        cast_specs.append(pl.BlockSpec(blk, f_out_rows))
        cast_shapes.append(jax.ShapeDtypeStruct(w.shape, BF16))
    outs = pl.pallas_call(
        functools.partial(_mlstm_kernel, n_cast=len(cast_weights)),
        grid=(steps,),
        in_specs=stream(f_chunk) + stream(b_chunk) + [pl.BlockSpec((2, N_UNITS), small),
                                                       pl.BlockSpec((N_UNITS, 2), small)] + cast_specs,
        out_specs=[pl.BlockSpec((M_WIDTH, L), f_out), pl.BlockSpec((M_WIDTH, L), b_out)] + cast_specs,
        out_shape=[jax.ShapeDtypeStruct((M_WIDTH, n_x), BF16)] * 2 + cast_shapes,
        scratch_shapes=[
            pltpu.VMEM((N_UNITS, M_V + BF16_SUBLANES, M_QK), F32),
            pltpu.VMEM((N_DIR, 8, LANES), F32),
            pltpu.VMEM((N_DIR, 8, LANES), F32),
        ],
        compiler_params=_cparams(("arbitrary",)),
        name="mlstm",
    )(*operands, *operands, bias_row, bias_row.T, *cast_weights)
    return outs[0], outs[1], tuple(outs[2:])


def _outproj_kernel(att_ref, hft_ref, hbt_ref, mo_ref, mg_ref, w_ref, x_ref, mod_ref, n2_ref,
                    x1_ref, h2_ref):
    d = x_ref.shape[1]
    tm = x_ref.shape[0]
    rows = tm // OUT_ROW_SPLIT

    def project(r0):
        y = jnp.dot(att_ref[r0:r0 + rows, :], w_ref[0:ATT_WIDTH, :], preferred_element_type=F32)
        for hd in range(M_HEADS):
            c = hd * M_V
            ht = (hft_ref[c:c + M_V, r0:r0 + rows].astype(F32)
                  + hbt_ref[c:c + M_V, r0:r0 + rows].astype(F32))
            ht = ht * lax.rsqrt(jnp.mean(ht * ht, axis=0, keepdims=True) + NORM_EPS) * mg_ref[c:c + M_V, :]
            r = (ht.T * mo_ref[r0:r0 + rows, c:c + M_V].astype(F32)).astype(BF16)
            y = y + jnp.dot(r, w_ref[ATT_WIDTH + c:ATT_WIDTH + c + M_V, :], preferred_element_type=F32)
        return y

    ys = [project(b * rows) for b in range(OUT_ROW_SPLIT)]
    for b, y in enumerate(ys):
        r0 = b * rows
        x1 = x_ref[r0:r0 + rows, :] + mod_ref[0:1, 0:d] * y
        x1_ref[r0:r0 + rows, :] = x1
        h2 = _rms(x1) * n2_ref[...] * (1.0 + mod_ref[0:1, 2 * d:3 * d]) + mod_ref[0:1, d:2 * d]
        h2_ref[r0:r0 + rows, :] = h2.astype(BF16)


def _out_proj(att, hf_t, hb_t, mo, m_gain, w_out, x, mod, norm2):
    t, d = x.shape
    tm = OUT_TM
    row = lambda i: (i, 0)
    full = lambda i: (0, 0)
    return pl.pallas_call(
        _outproj_kernel,
        grid=(t // tm,),
        in_specs=[
            pl.BlockSpec((tm, ATT_WIDTH), row),
            pl.BlockSpec((M_WIDTH, tm), lambda i: (0, i)),
            pl.BlockSpec((M_WIDTH, tm), lambda i: (0, i)),
            pl.BlockSpec((tm, M_WIDTH), row),
            pl.BlockSpec((M_WIDTH, 1), full),
            pl.BlockSpec(w_out.shape, full),
            pl.BlockSpec((tm, d), row),
            pl.BlockSpec(mod.shape, full),
            pl.BlockSpec((1, d), full),
        ],
        out_specs=[pl.BlockSpec((tm, d), row), pl.BlockSpec((tm, d), row)],
        out_shape=[jax.ShapeDtypeStruct((t, d), F32), jax.ShapeDtypeStruct((t, d), BF16)],
        compiler_params=_cparams(("arbitrary",)),
        name="out_proj",
    )(att, hf_t, hb_t, mo, m_gain, w_out, x, mod, norm2)


def _ffn_kernel(h_ref, hp_ref, hn_ref, wg_ref, wv_ref, cwg_ref, cwv_ref, cbg_ref, cbv_ref, wd_ref,
                x1_ref, mod_ref, nf_ref, o_ref, hext_ref, ug_ref, uv_ref):
    i = pl.program_id(0)
    j = pl.program_id(1)
    tm = h_ref.shape[0]
    d = x1_ref.shape[1]

    @pl.when(j == 0)
    def _():
        hext_ref[0:HALO, :] = jnp.where(i == 0, jnp.zeros_like(hp_ref[...]), hp_ref[...])
        hext_ref[HALO:HALO + tm, :] = h_ref[...]
        hext_ref[HALO + tm:, :] = jnp.where(i == pl.num_programs(0) - 1, jnp.zeros_like(hn_ref[...]), hn_ref[...])
        o_ref[...] = jnp.zeros(o_ref.shape, F32)

    split = tm // 2 + HALO
    cut = tm // 2 - HALO

    def conv(u_ref, cw_ref, cb_ref, r0, n):
        return (cw_ref[0:1, :] * u_ref[HALO - 1 + r0:HALO - 1 + r0 + n, :]
                + cw_ref[1:2, :] * u_ref[HALO + r0:HALO + r0 + n, :]
                + cw_ref[2:3, :] * u_ref[HALO + 1 + r0:HALO + 1 + r0 + n, :]
                + cb_ref[...])

    def hidden_block(final):
        for r0, r1 in ((0, split), (split, tm + 2 * HALO)):
            ug_ref[r0:r1, :] = jnp.dot(hext_ref[r0:r1, :], wg_ref[...], preferred_element_type=F32)
            uv_ref[r0:r1, :] = jnp.dot(hext_ref[r0:r1, :], wv_ref[...], preferred_element_type=F32)
        for r0, n in ((0, cut), (cut, tm - cut)):
            g = conv(ug_ref, cwg_ref, cbg_ref, r0, n)
            val = conv(uv_ref, cwv_ref, cbv_ref, r0, n)
            a = (g * jax.nn.sigmoid(g) * val).astype(BF16)
            acc = o_ref[r0:r0 + n, :] + jnp.dot(a, wd_ref[...], preferred_element_type=F32)
            if final:
                y = x1_ref[r0:r0 + n, :] + mod_ref[0:1, 3 * d:4 * d] * acc
                acc = _rms(y) * nf_ref[...]
            o_ref[r0:r0 + n, :] = acc

    last = pl.num_programs(1) - 1
    pl.when(j != last)(functools.partial(hidden_block, False))
    pl.when(j == last)(functools.partial(hidden_block, True))


def _conv_ffn(h2, w_up, conv_w, conv_b, w_down, x1, mod, norm_f):
    t, d = h2.shape
    d_ff = w_down.shape[0]
    tm, tf = FFN_TM, FFN_TF
    nf = d_ff // tf
    hb = tm // HALO
    last_halo = t // HALO - 1
    in_specs = [
        pl.BlockSpec((tm, d), lambda i, j: (i, 0)),
        pl.BlockSpec((HALO, d), lambda i, j: (jnp.maximum(i * hb - 1, 0), 0)),
        pl.BlockSpec((HALO, d), lambda i, j: (jnp.minimum((i + 1) * hb, last_halo), 0)),
        pl.BlockSpec((d, tf), lambda i, j: (0, j)),
        pl.BlockSpec((d, tf), lambda i, j: (0, j + nf)),
        pl.BlockSpec((3, tf), lambda i, j: (0, j)),
        pl.BlockSpec((3, tf), lambda i, j: (0, j + nf)),
        pl.BlockSpec((1, tf), lambda i, j: (0, j)),
        pl.BlockSpec((1, tf), lambda i, j: (0, j + nf)),
        pl.BlockSpec((tf, d), lambda i, j: (j, 0)),
        pl.BlockSpec((tm, d), lambda i, j: (i, 0)),
        pl.BlockSpec(mod.shape, lambda i, j: (0, 0)),
        pl.BlockSpec((1, d), lambda i, j: (0, 0)),
    ]
    return pl.pallas_call(
        _ffn_kernel,
        grid=(t // tm, nf),
        in_specs=in_specs,
        out_specs=pl.BlockSpec((tm, d), lambda i, j: (i, 0)),
        out_shape=jax.ShapeDtypeStruct((t, d), F32),
        scratch_shapes=[
            pltpu.VMEM((tm + 2 * HALO, d), BF16),
            pltpu.VMEM((tm + 2 * HALO, tf), F32),
            pltpu.VMEM((tm + 2 * HALO, tf), F32),
        ],
        compiler_params=_cparams(("arbitrary", "arbitrary")),
        name="conv_ffn",
    )(h2, h2, h2, w_up, w_up, conv_w, conv_w, conv_b, conv_b, w_down, x1, mod, norm_f)


def _rope_tables(n_ctx, n_tok):
    f32 = np.float32
    rows = n_tok // GRID_W
    row = np.repeat(np.arange(rows, dtype=f32), GRID_W)
    col = np.tile(np.arange(GRID_W, dtype=f32), rows)
    inv_freq = np.power(f32(ROPE_THETA), -np.arange(0, AXIS_DIM, 2, dtype=f32) / f32(AXIS_DIM)).astype(f32)
    ang_r = row[:, None] * inv_freq[None, :]
    ang_c = col[:, None] * inv_freq[None, :]
    cos = np.concatenate([np.cos(ang_r), np.cos(ang_r), np.cos(ang_c), np.cos(ang_c)], axis=1)
    sin = np.concatenate([-np.sin(ang_r), np.sin(ang_r), -np.sin(ang_c), np.sin(ang_c)], axis=1)
    cos = np.concatenate([np.ones((n_ctx, HEAD_DIM), f32), cos], axis=0).astype(f32)
    sin = np.concatenate([np.zeros((n_ctx, HEAD_DIM), f32), sin], axis=0).astype(f32)
    return jnp.asarray(cos), jnp.asarray(sin)


def kernel(x, c, ctx, c_ctx, w_mod, b_mod, norm1, w_in, q_norm, k_norm, b_igate, b_fgate, m_norm,
           w_out, norm2, w_up, conv_w, conv_b, w_down, norm_f):
    batch, n_tok, d = x.shape
    assert batch == 1 and w_mod.shape[0] == 1
    n_ctx = ctx.shape[1]
    x2 = x[0]
    ctx2 = ctx[0]

    cs = jnp.stack([c[0], c_ctx], axis=1)
    mod = _modulation(cs, w_mod[0], b_mod[0], 2 * d)

    w_in_t = w_in[0].T
    w_main = _cast_main_columns(w_in_t, OFF_G)
    w_gate = _gate_weights(w_in_t, OFF_G)
    cos_t, sin_t = _rope_tables(n_ctx, n_tok)
    q_gain = (q_norm[0] * (HEAD_DIM ** -0.5 * LOG2E)).reshape(1, HEAD_DIM)
    k_gain = k_norm[0].reshape(1, HEAD_DIM)
    q, k, vt, mq_t, mk, mv_t, mo, gates = _in_proj(
        x2, ctx2, mod, norm1[0].reshape(1, d), w_main, w_gate, q_gain, k_gain, cos_t, sin_t)

    score_bound = HEAD_DIM * jnp.max(jnp.abs(q_gain)) * jnp.max(jnp.abs(k_gain)) * ATTN_BOUND_HEADROOM
    bounded = (score_bound <= ATTN_SCORE_BOUND).astype(jnp.int32).reshape(1)
    att, mod_rest, (w_out_b, w_up_b, w_down_b) = _attention(
        q, k, vt, bounded, (w_out[0], w_up[0], w_down[0]), cs, w_mod[0], b_mod[0], 2 * d)

    hf_t, hb_t, _ = _mlstm(mk, mq_t, mv_t, gates, b_igate[0], b_fgate[0], n_tok, ())

    x1, h2 = _out_proj(att, hf_t, hb_t, mo, m_norm[0].reshape(M_WIDTH, 1), w_out_b, x2, mod_rest,
                       norm2[0].reshape(1, d))

    out = _conv_ffn(h2, w_up_b, conv_w[0], conv_b[0].reshape(1, -1), w_down_b, x1, mod_rest, norm_f.reshape(1, d))
    return out[None]
```

```python
import functools
import math

import jax
import jax.numpy as jnp
import numpy as np
from jax import lax
from jax.experimental import pallas as pl
from jax.experimental.pallas import tpu as pltpu

F32 = jnp.float32
BF16 = jnp.bfloat16

GRID_W = 64
HEAD_DIM = 128
N_Q_HEADS = 8
N_KV_HEADS = 2
Q_PER_KV = N_Q_HEADS // N_KV_HEADS
AXIS_DIM = HEAD_DIM // 2
ROPE_THETA = 10000.0
M_HEADS = 4
M_V = 256
M_QK = 128
N_DIR = 2
NORM_EPS = 1e-6
ATT_WIDTH = N_Q_HEADS * HEAD_DIM
KV_WIDTH = N_KV_HEADS * HEAD_DIM
M_WIDTH = M_HEADS * M_V
MQK_WIDTH = M_HEADS * M_QK
N_GATES = N_DIR * 2 * M_HEADS
N_UNITS = N_DIR * M_HEADS

OFF_AQ = 0
OFF_AK = OFF_AQ + ATT_WIDTH
OFF_AV = OFF_AK + KV_WIDTH
OFF_MQ = OFF_AV + KV_WIDTH
OFF_MK = OFF_MQ + MQK_WIDTH
OFF_MV = OFF_MK + MQK_WIDTH
OFF_MO = OFF_MV + M_WIDTH
OFF_G = OFF_MO + M_WIDTH

LANES = 128
MXU_WIDTH = 256
BF16_SUBLANES = 16
VMEM_LIMIT = 50 * 1024 * 1024

ROW_TILE = 256
MLSTM_CHUNK = 256
ATTN_TQ = 512
ATTN_SCORE_BOUND = 64.0
ATTN_BOUND_HEADROOM = 1.02
ATTN_UNIT_Q = 256
ATTN_TK_CAP = 768
ATTN_LOOKAHEAD = 2
OUT_TM = 512
OUT_ROW_SPLIT = 2
FFN_TM = 512
FFN_TF = 512
FFN_X1_SWITCH = 3
FFN_H2_SWITCH = 6
HALO = BF16_SUBLANES
MOD_TK = 128
WCAST_TN = 512
LOG2E = 1.4426950408889634


def _largest_divisor(n, cap, mult):
    best = None
    for d in range(mult, cap + 1, mult):
        if n % d == 0:
            best = d
    assert best is not None, (n, cap, mult)
    return best


def _cparams(sem, vmem=VMEM_LIMIT):
    return pltpu.CompilerParams(dimension_semantics=sem, vmem_limit_bytes=vmem)


def _rms(x, eps=NORM_EPS):
    return x * lax.rsqrt(jnp.mean(x * x, axis=-1, keepdims=True) + eps)


def _mod_kernel(cs_ref, w_ref, b_ref, o_ref):
    @pl.when(pl.program_id(0) == 0)
    def _():
        o_ref[0:1, :] = b_ref[...]
        o_ref[1:2, :] = b_ref[...]

    cs = cs_ref[...]
    s = cs * jax.nn.sigmoid(cs)
    w = w_ref[...]
    o_ref[0:1, :] += jnp.sum(s[:, 0:1] * w, axis=0, keepdims=True)
    o_ref[1:2, :] += jnp.sum(s[:, 1:2] * w, axis=0, keepdims=True)


def _modulation(cs, w_mod, b_mod, n):
    d = w_mod.shape[0]
    tk = MOD_TK
    return pl.pallas_call(
        _mod_kernel,
        grid=(d // tk,),
        in_specs=[
            pl.BlockSpec((tk, 2), lambda j: (j, 0)),
            pl.BlockSpec((tk, n), lambda j: (j, 0)),
            pl.BlockSpec((1, n), lambda j: (0, 0)),
        ],
        out_specs=pl.BlockSpec((2, n), lambda j: (0, 0)),
        out_shape=jax.ShapeDtypeStruct((2, n), F32),
        compiler_params=_cparams(("arbitrary",)),
        name="modulation",
    )(cs, w_mod, b_mod.reshape(1, -1))


def _gate_weight_kernel(wt_ref, o_ref):
    g = wt_ref[...]
    hi = g.astype(BF16)
    lo = (g - hi.astype(F32)).astype(BF16)
    pad = jnp.zeros((o_ref.shape[0] - 2 * N_GATES, g.shape[1]), BF16)
    o_ref[...] = jnp.concatenate([hi, lo, pad], axis=0)


def _gate_weights(w_t, row0):
    n, d = w_t.shape
    assert row0 % N_GATES == 0 and row0 + N_GATES == n
    return pl.pallas_call(
        _gate_weight_kernel,
        grid=(1,),
        in_specs=[pl.BlockSpec((N_GATES, d), lambda i: (row0 // N_GATES, 0))],
        out_specs=pl.BlockSpec((LANES, d), lambda i: (0, 0)),
        out_shape=jax.ShapeDtypeStruct((LANES, d), BF16),
        compiler_params=_cparams(("arbitrary",)),
        name="w_gate_split",
    )(w_t)


def _wcast_kernel(wt_ref, o_ref):
    o_ref[...] = wt_ref[...].T.astype(BF16)


def _cast_main_columns(w_t, n_main):
    n, d = w_t.shape
    tn = WCAST_TN
    assert n_main % tn == 0 and n_main <= n
    return pl.pallas_call(
        _wcast_kernel,
        grid=(n_main // tn,),
        in_specs=[pl.BlockSpec((tn, d), lambda j: (j, 0))],
        out_specs=pl.BlockSpec((d, tn), lambda j: (0, j)),
        out_shape=jax.ShapeDtypeStruct((d, n_main), BF16),
        compiler_params=_cparams(("arbitrary",)),
        name="w_in_cast",
    )(w_t)


def _swap_rope_halves(y):
    lane = lax.broadcasted_iota(jnp.int32, y.shape, 1)
    fwd = pltpu.roll(y, LANES - AXIS_DIM // 2, axis=1)
    bwd = pltpu.roll(y, AXIS_DIM // 2, axis=1)
    return jnp.where((lane & (AXIS_DIM // 2)) == 0, fwd, bwd)


def _inproj_kernel(x_ref, ctx_ref, mod_ref, n1_ref, w_ref, wg_ref, qg_ref, kg_ref, cos_ref, sin_ref,
                   qt_ref, k_ref, vt_ref, mqt_ref, mk_ref, mvt_ref, mo_ref, g_ref):
    d = x_ref.shape[1]
    is_ctx = pl.program_id(0) == 0
    xin = jnp.where(is_ctx, ctx_ref[...], x_ref[...])
    shift = jnp.where(is_ctx, mod_ref[1:2, 0:d], mod_ref[0:1, 0:d])
    scale = jnp.where(is_ctx, mod_ref[1:2, d:2 * d], mod_ref[0:1, d:2 * d])
    hf = _rms(xin) * n1_ref[...] * (1.0 + scale) + shift
    h_hi = hf.astype(BF16)

    def proj(c0, width):
        return jnp.dot(h_hi, w_ref[:, c0:c0 + width], preferred_element_type=F32)

    cos = cos_ref[...]
    sin = sin_ref[...]

    def norm_rope(y, gain):
        yn = _rms(y) * gain
        return yn * cos + _swap_rope_halves(yn) * sin

    for j in range(ATT_WIDTH // MXU_WIDTH):
        y = proj(OFF_AQ + j * MXU_WIDTH, MXU_WIDTH)
        for t in range(MXU_WIDTH // HEAD_DIM):
            c = j * MXU_WIDTH + t * HEAD_DIM
            qt_ref[c:c + HEAD_DIM, :] = norm_rope(y[:, t * HEAD_DIM:(t + 1) * HEAD_DIM], qg_ref[...]).T.astype(BF16)
    y = proj(OFF_AK, KV_WIDTH)
    for t in range(N_KV_HEADS):
        k_ref[:, t * HEAD_DIM:(t + 1) * HEAD_DIM] = norm_rope(
            y[:, t * HEAD_DIM:(t + 1) * HEAD_DIM], kg_ref[...]).astype(BF16)
    vt_ref[...] = proj(OFF_AV, KV_WIDTH).T.astype(BF16)
    for j in range(MQK_WIDTH // MXU_WIDTH):
        c = j * MXU_WIDTH
        mqt_ref[c:c + MXU_WIDTH, :] = proj(OFF_MQ + c, MXU_WIDTH).T.astype(BF16)
        mk_ref[:, c:c + MXU_WIDTH] = (proj(OFF_MK + c, MXU_WIDTH) * (M_QK ** -0.5)).astype(BF16)
    for j in range(M_WIDTH // MXU_WIDTH):
        c = j * MXU_WIDTH
        mvt_ref[c:c + MXU_WIDTH, :] = proj(OFF_MV + c, MXU_WIDTH).T.astype(BF16)
        mo_ref[:, c:c + MXU_WIDTH] = jax.nn.sigmoid(proj(OFF_MO + c, MXU_WIDTH)).astype(BF16)
    nt = (((1,), (1,)), ((), ()))
    gg = lax.dot_general(h_hi, wg_ref[...], nt, preferred_element_type=F32)
    gg = gg + pltpu.roll(gg, LANES - N_GATES, axis=1)
    g_ref[...] = gg[:, 0:N_GATES]


def _in_proj(x, ctx, mod, norm1, w_main, w_gate, q_gain, k_gain, cos_t, sin_t):
    t, d = x.shape
    n_ctx = ctx.shape[0]
    assert n_ctx == ROW_TILE and t % ROW_TILE == 0
    rows = n_ctx + t
    steps = rows // ROW_TILE
    full = lambda i: (0, 0)
    row_all = lambda i: (i, 0)
    row_x = lambda i: (jnp.maximum(i - 1, 0), 0)
    out_shape = [
        jax.ShapeDtypeStruct((ATT_WIDTH, t), BF16),
        jax.ShapeDtypeStruct((rows, KV_WIDTH), BF16),
        jax.ShapeDtypeStruct((KV_WIDTH, rows), BF16),
        jax.ShapeDtypeStruct((MQK_WIDTH, rows), BF16),
        jax.ShapeDtypeStruct((rows, MQK_WIDTH), BF16),
        jax.ShapeDtypeStruct((M_WIDTH, rows), BF16),
        jax.ShapeDtypeStruct((t, M_WIDTH), BF16),
        jax.ShapeDtypeStruct((rows, N_GATES), F32),
    ]
    out_specs = [
        pl.BlockSpec((ATT_WIDTH, ROW_TILE), lambda i: (0, jnp.maximum(i - 1, 0))),
        pl.BlockSpec((ROW_TILE, KV_WIDTH), row_all),
        pl.BlockSpec((KV_WIDTH, ROW_TILE), lambda i: (0, i)),
        pl.BlockSpec((MQK_WIDTH, ROW_TILE), lambda i: (0, i)),
        pl.BlockSpec((ROW_TILE, MQK_WIDTH), row_all),
        pl.BlockSpec((M_WIDTH, ROW_TILE), lambda i: (0, i)),
        pl.BlockSpec((ROW_TILE, M_WIDTH), row_x),
        pl.BlockSpec((ROW_TILE, N_GATES), row_all),
    ]
    in_specs = [
        pl.BlockSpec((ROW_TILE, d), row_x),
        pl.BlockSpec((ROW_TILE, d), full),
        pl.BlockSpec(mod.shape, full),
        pl.BlockSpec((1, d), full),
        pl.BlockSpec(w_main.shape, full),
        pl.BlockSpec(w_gate.shape, full),
        pl.BlockSpec((1, HEAD_DIM), full),
        pl.BlockSpec((1, HEAD_DIM), full),
        pl.BlockSpec((ROW_TILE, HEAD_DIM), row_all),
        pl.BlockSpec((ROW_TILE, HEAD_DIM), row_all),
    ]
    return pl.pallas_call(
        _inproj_kernel,
        grid=(steps,),
        in_specs=in_specs,
        out_specs=out_specs,
        out_shape=out_shape,
        compiler_params=_cparams(("arbitrary",)),
        name="in_proj",
    )(x, ctx, mod, norm1, w_main, w_gate, q_gain, k_gain, cos_t, sin_t)


def _attn_kernel(bounded_ref, qt_ref, k_ref, vt_ref, cs_ref, wm_a_ref, wm_b_ref, bm_ref, *rest, tk, n_cast):
    cast_in, (o_ref, modb_ref), rest = rest[:n_cast], rest[n_cast:n_cast + 2], rest[n_cast + 2:]
    cast_out, (acc_scr,) = rest[:n_cast], rest[n_cast:]
    uq = ATTN_UNIT_Q

    @pl.when(jnp.logical_and(pl.program_id(0) == 0, pl.program_id(1) == 0))
    def _():
        modb_ref[0:1, :] = bm_ref[...]
        modb_ref[1:2, :] = bm_ref[...]

    n_sub = qt_ref.shape[1] // uq
    n_chunks = k_ref.shape[0] // tk
    streams = [(r, h) for r in range(n_sub) for h in range(Q_PER_KV)]
    zero_row = jnp.zeros((1, uq), F32)

    def q_t(g):
        r, h = streams[g]
        return qt_ref[h * HEAD_DIM:(h + 1) * HEAD_DIM, r * uq:(r + 1) * uq]

    def finish(l):
        for g, (r, h) in enumerate(streams):
            o_ref[h * HEAD_DIM:(h + 1) * HEAD_DIM, r * uq:(r + 1) * uq] = (acc_scr[g] / l[g]).astype(o_ref.dtype)
        for w_ref, wo_ref in zip(cast_in, cast_out):
            wo_ref[...] = w_ref[...].astype(BF16)
        cs = cs_ref[...]
        s = cs * jax.nn.sigmoid(cs)
        half = wm_a_ref.shape[1]
        for n, wm_ref in enumerate((wm_a_ref, wm_b_ref)):
            wm = wm_ref[...]
            modb_ref[0:1, n * half:(n + 1) * half] += jnp.sum(s[:, 0:1] * wm, axis=0, keepdims=True)
            modb_ref[1:2, n * half:(n + 1) * half] += jnp.sum(s[:, 1:2] * wm, axis=0, keepdims=True)

    @pl.when(bounded_ref[0] == 1)
    def _():
        units = [(c, g) for c in range(n_chunks) for g in range(len(streams))]

        def scores(c, g):
            return jnp.dot(k_ref[c * tk:(c + 1) * tk, :], q_t(g), preferred_element_type=F32)

        l = [zero_row for _ in streams]
        pending = [scores(*u) for u in units[:ATTN_LOOKAHEAD]]
        for n, (c, g) in enumerate(units):
            st = pending.pop(0)
            if n + ATTN_LOOKAHEAD < len(units):
                pending.append(scores(*units[n + ATTN_LOOKAHEAD]))
            p = jnp.exp2(st)
            l[g] = l[g] + jnp.sum(p, axis=0, keepdims=True)
            pv = jnp.dot(vt_ref[:, c * tk:(c + 1) * tk], p.astype(BF16), preferred_element_type=F32)
            if c == 0:
                acc_scr[g] = pv
            else:
                acc_scr[g] += pv
        finish(l)

    @pl.when(bounded_ref[0] == 0)
    def _():
        acc_scr[...] = jnp.zeros(acc_scr.shape, F32)

        def chunk(c, carry):
            m, l = carry
            r0 = pl.multiple_of(c * tk, tk)
            kc = k_ref[pl.ds(r0, tk), :]
            vtc = vt_ref[:, pl.ds(r0, tk)]
            m_out, l_out = [], []
            for g in range(len(streams)):
                st = jnp.dot(kc, q_t(g), preferred_element_type=F32)
                m_new = jnp.maximum(m[g], jnp.max(st, axis=0, keepdims=True))
                p = jnp.exp2(st - m_new)
                alpha = jnp.exp2(m[g] - m_new)
                l_out.append(alpha * l[g] + jnp.sum(p, axis=0, keepdims=True))
                acc_scr[g] = alpha * acc_scr[g] + jnp.dot(vtc, p.astype(BF16), preferred_element_type=F32)
                m_out.append(m_new)
            return tuple(m_out), tuple(l_out)

        init = (tuple(jnp.full((1, uq), -jnp.inf, F32) for _ in streams), tuple(zero_row for _ in streams))
        _, l = lax.fori_loop(0, n_chunks, chunk, init)
        finish(l)


def _attention(q_t, k, vt, bounded, cast_weights, cs, w_mod, b_mod, n_done):
    t = q_t.shape[1]
    s_len = k.shape[0]
    tq = ATTN_TQ
    tk = _largest_divisor(s_len, ATTN_TK_CAP, MXU_WIDTH)
    group_w = Q_PER_KV * HEAD_DIM
    n_i = t // tq
    steps = N_KV_HEADS * n_i
    step = lambda h, i: h * n_i + i
    n_rest = w_mod.shape[1] - n_done
    half = n_rest // 2
    assert n_done % half == 0 and w_mod.shape[0] % steps == 0
    mod_rows = w_mod.shape[0] // steps
    mod_specs = [
        pl.BlockSpec((mod_rows, 2), lambda h, i: (step(h, i), 0)),
        pl.BlockSpec((mod_rows, half), lambda h, i: (step(h, i), n_done // half)),
        pl.BlockSpec((mod_rows, half), lambda h, i: (step(h, i), n_done // half + 1)),
        pl.BlockSpec((1, n_rest), lambda h, i: (0, 0)),
    ]
    cast_specs, cast_shapes = [], []
    for w in cast_weights:
        assert w.shape[0] % (steps * BF16_SUBLANES) == 0, w.shape
        cast_specs.append(pl.BlockSpec((w.shape[0] // steps, w.shape[1]), lambda h, i: (step(h, i), 0)))
        cast_shapes.append(jax.ShapeDtypeStruct(w.shape, BF16))
    outs = pl.pallas_call(
        functools.partial(_attn_kernel, tk=tk, n_cast=len(cast_weights)),
        grid=(N_KV_HEADS, n_i),
        in_specs=[
            pl.BlockSpec(memory_space=pltpu.SMEM),
            pl.BlockSpec((group_w, tq), lambda h, i: (h, i)),
            pl.BlockSpec((s_len, HEAD_DIM), lambda h, i: (0, h), pipeline_mode=pl.Buffered(1)),
            pl.BlockSpec((HEAD_DIM, s_len), lambda h, i: (h, 0), pipeline_mode=pl.Buffered(1)),
        ] + mod_specs + cast_specs,
        out_specs=[pl.BlockSpec((group_w, tq), lambda h, i: (h, i)),
                   pl.BlockSpec((2, n_rest), lambda h, i: (0, 0))] + cast_specs,
        out_shape=[jax.ShapeDtypeStruct((ATT_WIDTH, t), BF16),
                   jax.ShapeDtypeStruct((2, n_rest), F32)] + cast_shapes,
        scratch_shapes=[
            pltpu.VMEM((Q_PER_KV * tq // ATTN_UNIT_Q, HEAD_DIM, ATTN_UNIT_Q), F32),
        ],
        compiler_params=_cparams(("arbitrary", "arbitrary")),
        name="attention",
    )(bounded, q_t, k, vt, cs, w_mod, w_mod, b_mod[n_done:].reshape(1, n_rest), *cast_weights)
    return outs[0], outs[1], tuple(outs[2:])


def _log_sigmoid(x):
    return jnp.minimum(x, 0.0) - jnp.log1p(jnp.exp(-jnp.abs(x)))


def _scan(x, axis, op, ident, reverse):
    n = x.shape[axis]
    idx = lax.broadcasted_iota(jnp.int32, x.shape, axis)
    shift = 1
    while shift < n:
        if reverse:
            moved = pltpu.roll(x, n - shift, axis=axis)
            ok = idx < n - shift
        else:
            moved = pltpu.roll(x, shift, axis=axis)
            ok = idx >= shift
        x = op(x, jnp.where(ok, moved, ident))
        shift *= 2
    return x


class _MlstmDirection:
    def __init__(self, reverse, k_ref, qt_ref, vt_ref, gi_ref, gf_ref, git_ref, gft_ref, h_ref,
                 ct_scr, mrow_scr, mcol_scr, visible):
        self.reverse, self.visible = reverse, visible
        self.k_ref, self.qt_ref, self.vt_ref, self.h_ref = k_ref, qt_ref, vt_ref, h_ref
        self.gi_ref, self.gf_ref, self.git_ref, self.gft_ref = gi_ref, gf_ref, git_ref, gft_ref
        self.ct_scr, self.mrow_scr, self.mcol_scr = ct_scr, mrow_scr, mcol_scr
        self.units = [((M_HEADS if reverse else 0) + hd, hd) for hd in range(M_HEADS)]

    def _k(self, hd):
        return self.k_ref[:, hd * M_QK:(hd + 1) * M_QK]

    def _vt(self, hd):
        return self.vt_ref[hd * M_V:(hd + 1) * M_V, :]

    def state_free_matmuls(self):
        self.early = []
        for u, hd in self.units:
            qt = self.qt_ref[hd * M_QK:(hd + 1) * M_QK, :]
            state = self.ct_scr[u]
            qk_t = jnp.dot(self._k(hd), qt, preferred_element_type=F32)
            inter_t = jnp.dot(state.astype(BF16), qt, preferred_element_type=F32)
            self.early.append((state, qk_t, inter_t))

    def gate_terms(self, brow_ref, bcol_ref):
        add = lambda a, b: a + b
        reverse = self.reverse
        i_c = self.gi_ref[...] + brow_ref[0:1, :]
        ls_c = _log_sigmoid(self.gf_ref[...] + brow_ref[1:2, :])
        self.a_c = i_c - _scan(ls_c, 0, add, 0.0, reverse)
        m_prev_r = self.mrow_scr[0:1, 0:N_UNITS]
        mm_r = jnp.maximum(m_prev_r, jnp.max(self.a_c, axis=0, keepdims=True))
        self.kscale_c = jnp.exp(self.a_c - mm_r)
        self.w_prev_r = jnp.exp(m_prev_r - mm_r)
        self.mrow_scr[0:1, 0:N_UNITS] = jnp.sum(ls_c, axis=0, keepdims=True) + mm_r
        i_r = self.git_ref[...] + bcol_ref[:, 0:1]
        ls_r = _log_sigmoid(self.gft_ref[...] + bcol_ref[:, 1:2])
        cum_r = _scan(ls_r, 1, add, 0.0, reverse)
        a_r = i_r - cum_r
        m_prev_c = self.mcol_scr[0:N_UNITS, 0:1]
        m_t = cum_r + jnp.maximum(_scan(a_r, 1, jnp.maximum, -jnp.inf, reverse), m_prev_c)
        self.r_r = cum_r - m_t
        self.w_inter_r = jnp.exp(cum_r + m_prev_c - m_t)
        self.floor_r = jnp.exp(-m_t)
        self.mcol_scr[0:N_UNITS, 0:1] = (jnp.sum(ls_r, axis=1, keepdims=True)
                                         + jnp.maximum(m_prev_c, jnp.max(a_r, axis=1, keepdims=True)))

    def state_update(self):
        for (u, hd), (state, _, _) in zip(self.units, self.early):
            kw = self._k(hd).astype(F32) * self.kscale_c[:, u:u + 1]
            w_prev = self.w_prev_r[0:1, u:u + 1]
            self.ct_scr[u, 0:M_V, :] = (w_prev * state[0:M_V]
                                        + jnp.dot(self._vt(hd), kw.astype(BF16), preferred_element_type=F32))
            self.ct_scr[u, M_V:M_V + 1, :] = w_prev * state[M_V:M_V + 1] + jnp.sum(kw, axis=0, keepdims=True)

    def outputs(self):
        for (u, hd), (_, qk_t, inter_t) in zip(self.units, self.early):
            decay_t = jnp.exp(jnp.where(self.visible, self.a_c[:, u:u + 1], -jnp.inf) + self.r_r[u:u + 1, :])
            s_t = qk_t * decay_t
            w_inter = self.w_inter_r[u:u + 1, :]
            num_t = (inter_t[0:M_V] * w_inter
                     + jnp.dot(self._vt(hd), s_t.astype(BF16), preferred_element_type=F32))
            den = inter_t[M_V:M_V + 1] * w_inter + jnp.sum(s_t, axis=0, keepdims=True)
            self.h_ref[hd * M_V:(hd + 1) * M_V, :] = (num_t * (
                1.0 / jnp.maximum(jnp.abs(den), self.floor_r[u:u + 1, :]))).astype(self.h_ref.dtype)


def _mlstm_kernel(kf_ref, qtf_ref, vtf_ref, gif_ref, gff_ref, gitf_ref, gftf_ref,
                  kb_ref, qtb_ref, vtb_ref, gib_ref, gfb_ref, gitb_ref, gftb_ref,
                  brow_ref, bcol_ref, *rest, n_cast):
    cast_in, (hf_ref, hb_ref), rest = rest[:n_cast], rest[n_cast:n_cast + 2], rest[n_cast + 2:]
    cast_out, (ct_scr, mrow_scr, mcol_scr) = rest[:n_cast], rest[n_cast:]
    L = kf_ref.shape[0]

    @pl.when(pl.program_id(0) == 0)
    def _():
        ct_scr[...] = jnp.zeros(ct_scr.shape, F32)
        mrow_scr[...] = jnp.zeros(mrow_scr.shape, F32)
        mcol_scr[...] = jnp.zeros(mcol_scr.shape, F32)

    src = lax.broadcasted_iota(jnp.int32, (L, L), 0)
    tgt = lax.broadcasted_iota(jnp.int32, (L, L), 1)
    dirs = (
        _MlstmDirection(False, kf_ref, qtf_ref, vtf_ref, gif_ref, gff_ref, gitf_ref, gftf_ref, hf_ref,
                        ct_scr, mrow_scr.at[0], mcol_scr.at[0], src <= tgt),
        _MlstmDirection(True, kb_ref, qtb_ref, vtb_ref, gib_ref, gfb_ref, gitb_ref, gftb_ref, hb_ref,
                        ct_scr, mrow_scr.at[1], mcol_scr.at[1], src >= tgt),
    )
    for d in dirs:
        d.state_free_matmuls()
    for d in dirs:
        d.gate_terms(brow_ref, bcol_ref)
    for d in dirs:
        d.state_update()
    for d in dirs:
        d.outputs()
    for w_ref, o_ref in zip(cast_in, cast_out):
        o_ref[...] = w_ref[...].astype(BF16)


def _mlstm(mk, mq_t, mv_t, gates, b_i, b_f, n_x, cast_weights):
    L = MLSTM_CHUNK
    rows = mk.shape[0]
    assert rows % L == 0 and n_x % L == 0 and rows - n_x == L
    nx = n_x // L
    steps = nx + 1
    g4 = gates.reshape(rows, N_DIR, 2, M_HEADS)
    g_in = g4[:, :, 0, :].reshape(rows, N_UNITS)
    g_fg = g4[:, :, 1, :].reshape(rows, N_UNITS)
    bias_row = jnp.stack([b_i.reshape(N_UNITS), b_f.reshape(N_UNITS)], axis=0)
    f_chunk = lambda g: g
    b_chunk = lambda g: jnp.where(g == 0, 0, nx + 1 - g)
    f_out = lambda g: (0, jnp.maximum(g - 1, 0))
    b_out = lambda g: (0, jnp.where(g == 0, nx - 1, nx - g))
    f_out_rows = lambda g: (jnp.maximum(g - 1, 0), 0)
    small = lambda g: (0, 0)

    def stream(chunk):
        return [
            pl.BlockSpec((L, MQK_WIDTH), lambda g: (chunk(g), 0)),
            pl.BlockSpec((MQK_WIDTH, L), lambda g: (0, chunk(g))),
            pl.BlockSpec((M_WIDTH, L), lambda g: (0, chunk(g))),
            pl.BlockSpec((L, N_UNITS), lambda g: (chunk(g), 0)),
            pl.BlockSpec((L, N_UNITS), lambda g: (chunk(g), 0)),
            pl.BlockSpec((N_UNITS, L), lambda g: (0, chunk(g))),
            pl.BlockSpec((N_UNITS, L), lambda g: (0, chunk(g))),
        ]

    operands = (mk, mq_t, mv_t, g_in, g_fg, g_in.T, g_fg.T)
    cast_specs, cast_shapes = [], []
    for w in cast_weights:
        assert w.shape[0] % (nx * BF16_SUBLANES) == 0, w.shape
        blk = (w.shape[0] // nx, w.shape[1])
        cast_specs.append(pl.BlockSpec(blk, f_out_rows))
        cast_shapes.append(jax.ShapeDtypeStruct(w.shape, BF16))
    outs = pl.pallas_call(
        functools.partial(_mlstm_kernel, n_cast=len(cast_weights)),
        grid=(steps,),
        in_specs=stream(f_chunk) + stream(b_chunk) + [pl.BlockSpec((2, N_UNITS), small),
                                                       pl.BlockSpec((N_UNITS, 2), small)] + cast_specs,
        out_specs=[pl.BlockSpec((M_WIDTH, L), f_out), pl.BlockSpec((M_WIDTH, L), b_out)] + cast_specs,
        out_shape=[jax.ShapeDtypeStruct((M_WIDTH, n_x), BF16)] * 2 + cast_shapes,
        scratch_shapes=[
            pltpu.VMEM((N_UNITS, M_V + BF16_SUBLANES, M_QK), F32),
            pltpu.VMEM((N_DIR, 8, LANES), F32),
            pltpu.VMEM((N_DIR, 8, LANES), F32),
        ],
        compiler_params=_cparams(("arbitrary",)),
        name="mlstm",
    )(*operands, *operands, bias_row, bias_row.T, *cast_weights)
    return outs[0], outs[1], tuple(outs[2:])


def _outproj_kernel(att_ref, hft_ref, hbt_ref, mo_ref, mg_ref, w_ref, x_ref, mod_ref, n2_ref,
                    x1_ref, h2_ref):
    d = x_ref.shape[1]
    tm = x_ref.shape[0]
    rows = tm // OUT_ROW_SPLIT

    def project(r0):
        y = lax.dot_general(att_ref[:, r0:r0 + rows], w_ref[0:ATT_WIDTH, :], (((0,), (0,)), ((), ())),
                            preferred_element_type=F32)
        for hd in range(M_HEADS):
            c = hd * M_V
            ht = (hft_ref[c:c + M_V, r0:r0 + rows].astype(F32)
                  + hbt_ref[c:c + M_V, r0:r0 + rows].astype(F32))
            ht = ht * lax.rsqrt(jnp.mean(ht * ht, axis=0, keepdims=True) + NORM_EPS) * mg_ref[c:c + M_V, :]
            r = (ht.T * mo_ref[r0:r0 + rows, c:c + M_V].astype(F32)).astype(BF16)
            y = y + jnp.dot(r, w_ref[ATT_WIDTH + c:ATT_WIDTH + c + M_V, :], preferred_element_type=F32)
        return y

    ys = [project(b * rows) for b in range(OUT_ROW_SPLIT)]
    for b, y in enumerate(ys):
        r0 = b * rows
        x1 = x_ref[r0:r0 + rows, :] + mod_ref[0:1, 0:d] * y
        x1_ref[r0:r0 + rows, :] = x1
        h2 = _rms(x1) * n2_ref[...] * (1.0 + mod_ref[0:1, 2 * d:3 * d]) + mod_ref[0:1, d:2 * d]
        h2_ref[r0:r0 + rows, :] = h2.astype(BF16)


def _out_proj(att, hf_t, hb_t, mo, m_gain, w_out, x, mod, norm2):
    t, d = x.shape
    tm = OUT_TM
    row = lambda i: (i, 0)
    full = lambda i: (0, 0)
    return pl.pallas_call(
        _outproj_kernel,
        grid=(t // tm,),
        in_specs=[
            pl.BlockSpec((ATT_WIDTH, tm), lambda i: (0, i)),
            pl.BlockSpec((M_WIDTH, tm), lambda i: (0, i)),
            pl.BlockSpec((M_WIDTH, tm), lambda i: (0, i)),
            pl.BlockSpec((tm, M_WIDTH), row),
            pl.BlockSpec((M_WIDTH, 1), full),
            pl.BlockSpec(w_out.shape, full),
            pl.BlockSpec((tm, d), row),
            pl.BlockSpec(mod.shape, full),
            pl.BlockSpec((1, d), full),
        ],
        out_specs=[pl.BlockSpec((tm, d), row), pl.BlockSpec((tm, d), row)],
        out_shape=[jax.ShapeDtypeStruct((t, d), F32), jax.ShapeDtypeStruct((t, d), BF16)],
        compiler_params=_cparams(("arbitrary",)),
        name="out_proj",
    )(att, hf_t, hb_t, mo, m_gain, w_out, x, mod, norm2)


def _ffn_kernel(h_ref, hp_ref, hn_ref, wg_ref, wv_ref, cwg_ref, cwv_ref, cbg_ref, cbv_ref, wd_ref,
                x1_ref, mod_ref, nf_ref, o_ref, hext_ref, ug_ref, uv_ref):
    i = pl.program_id(0)
    j = pl.program_id(1)
    tm = h_ref.shape[0]
    d = x1_ref.shape[1]

    @pl.when(j == 0)
    def _():
        hext_ref[0:HALO, :] = jnp.where(i == 0, jnp.zeros_like(hp_ref[...]), hp_ref[...])
        hext_ref[HALO:HALO + tm, :] = h_ref[...]
        hext_ref[HALO + tm:, :] = jnp.where(i == pl.num_programs(0) - 1, jnp.zeros_like(hn_ref[...]), hn_ref[...])
        o_ref[...] = jnp.zeros(o_ref.shape, F32)

    split = tm // 2 + HALO
    cut = tm // 2 - HALO

    def conv(u_ref, cw_ref, cb_ref, r0, n):
        return (cw_ref[0:1, :] * u_ref[HALO - 1 + r0:HALO - 1 + r0 + n, :]
                + cw_ref[1:2, :] * u_ref[HALO + r0:HALO + r0 + n, :]
                + cw_ref[2:3, :] * u_ref[HALO + 1 + r0:HALO + 1 + r0 + n, :]
                + cb_ref[...])

    def hidden_block(final):
        for r0, r1 in ((0, split), (split, tm + 2 * HALO)):
            ug_ref[r0:r1, :] = jnp.dot(hext_ref[r0:r1, :], wg_ref[...], preferred_element_type=F32)
            uv_ref[r0:r1, :] = jnp.dot(hext_ref[r0:r1, :], wv_ref[...], preferred_element_type=F32)
        for r0, n in ((0, cut), (cut, tm - cut)):
            g = conv(ug_ref, cwg_ref, cbg_ref, r0, n)
            val = conv(uv_ref, cwv_ref, cbv_ref, r0, n)
            a = (g * jax.nn.sigmoid(g) * val).astype(BF16)
            acc = o_ref[r0:r0 + n, :] + jnp.dot(a, wd_ref[...], preferred_element_type=F32)
            if final:
                y = x1_ref[r0:r0 + n, :] + mod_ref[0:1, 3 * d:4 * d] * acc
                acc = _rms(y) * nf_ref[...]
            o_ref[r0:r0 + n, :] = acc

    last = pl.num_programs(1) - 1
    pl.when(j != last)(functools.partial(hidden_block, False))
    pl.when(j == last)(functools.partial(hidden_block, True))


def _conv_ffn(h2, w_up, conv_w, conv_b, w_down, x1, mod, norm_f):
    t, d = h2.shape
    d_ff = w_down.shape[0]
    tm, tf = FFN_TM, FFN_TF
    nf = d_ff // tf
    hb = tm // HALO
    last_halo = t // HALO - 1
    last_tile = t // tm - 1
    h_tile = lambda i, j: jnp.where(j >= FFN_H2_SWITCH, jnp.minimum(i + 1, last_tile), i)
    x_tile = lambda i, j: jnp.where(j >= FFN_X1_SWITCH, i, jnp.maximum(i - 1, 0))
    in_specs = [
        pl.BlockSpec((tm, d), lambda i, j: (h_tile(i, j), 0)),
        pl.BlockSpec((HALO, d), lambda i, j: (jnp.maximum(h_tile(i, j) * hb - 1, 0), 0)),
        pl.BlockSpec((HALO, d), lambda i, j: (jnp.minimum((h_tile(i, j) + 1) * hb, last_halo), 0)),
        pl.BlockSpec((d, tf), lambda i, j: (0, j)),
        pl.BlockSpec((d, tf), lambda i, j: (0, j + nf)),
        pl.BlockSpec((3, tf), lambda i, j: (0, j)),
        pl.BlockSpec((3, tf), lambda i, j: (0, j + nf)),
        pl.BlockSpec((1, tf), lambda i, j: (0, j)),
        pl.BlockSpec((1, tf), lambda i, j: (0, j + nf)),
        pl.BlockSpec((tf, d), lambda i, j: (j, 0)),
        pl.BlockSpec((tm, d), lambda i, j: (x_tile(i, j), 0)),
        pl.BlockSpec(mod.shape, lambda i, j: (0, 0)),
        pl.BlockSpec((1, d), lambda i, j: (0, 0)),
    ]
    return pl.pallas_call(
        _ffn_kernel,
        grid=(t // tm, nf),
        in_specs=in_specs,
        out_specs=pl.BlockSpec((tm, d), lambda i, j: (i, 0)),
        out_shape=jax.ShapeDtypeStruct((t, d), F32),
        scratch_shapes=[
            pltpu.VMEM((tm + 2 * HALO, d), BF16),
            pltpu.VMEM((tm + 2 * HALO, tf), F32),
            pltpu.VMEM((tm + 2 * HALO, tf), F32),
        ],
        compiler_params=_cparams(("arbitrary", "arbitrary")),
        name="conv_ffn",
    )(h2, h2, h2, w_up, w_up, conv_w, conv_w, conv_b, conv_b, w_down, x1, mod, norm_f)


def _rope_tables(n_ctx, n_tok):
    f32 = np.float32
    rows = n_tok // GRID_W
    row = np.repeat(np.arange(rows, dtype=f32), GRID_W)
    col = np.tile(np.arange(GRID_W, dtype=f32), rows)
    inv_freq = np.power(f32(ROPE_THETA), -np.arange(0, AXIS_DIM, 2, dtype=f32) / f32(AXIS_DIM)).astype(f32)
    ang_r = row[:, None] * inv_freq[None, :]
    ang_c = col[:, None] * inv_freq[None, :]
    cos = np.concatenate([np.cos(ang_r), np.cos(ang_r), np.cos(ang_c), np.cos(ang_c)], axis=1)
    sin = np.concatenate([-np.sin(ang_r), np.sin(ang_r), -np.sin(ang_c), np.sin(ang_c)], axis=1)
    cos = np.concatenate([np.ones((n_ctx, HEAD_DIM), f32), cos], axis=0).astype(f32)
    sin = np.concatenate([np.zeros((n_ctx, HEAD_DIM), f32), sin], axis=0).astype(f32)
    return jnp.asarray(cos), jnp.asarray(sin)


def kernel(x, c, ctx, c_ctx, w_mod, b_mod, norm1, w_in, q_norm, k_norm, b_igate, b_fgate, m_norm,
           w_out, norm2, w_up, conv_w, conv_b, w_down, norm_f):
    batch, n_tok, d = x.shape
    assert batch == 1 and w_mod.shape[0] == 1
    n_ctx = ctx.shape[1]
    x2 = x[0]
    ctx2 = ctx[0]

    cs = jnp.stack([c[0], c_ctx], axis=1)
    mod = _modulation(cs, w_mod[0], b_mod[0], 2 * d)

    w_in_t = w_in[0].T
    w_main = _cast_main_columns(w_in_t, OFF_G)
    w_gate = _gate_weights(w_in_t, OFF_G)
    cos_t, sin_t = _rope_tables(n_ctx, n_tok)
    q_gain = (q_norm[0] * (HEAD_DIM ** -0.5 * LOG2E)).reshape(1, HEAD_DIM)
    k_gain = k_norm[0].reshape(1, HEAD_DIM)
    q_t, k, vt, mq_t, mk, mv_t, mo, gates = _in_proj(
        x2, ctx2, mod, norm1[0].reshape(1, d), w_main, w_gate, q_gain, k_gain, cos_t, sin_t)

    score_bound = HEAD_DIM * jnp.max(jnp.abs(q_gain)) * jnp.max(jnp.abs(k_gain)) * ATTN_BOUND_HEADROOM
    bounded = (score_bound <= ATTN_SCORE_BOUND).astype(jnp.int32).reshape(1)
    att, mod_rest, (w_out_b, w_up_b, w_down_b) = _attention(
        q_t, k, vt, bounded, (w_out[0], w_up[0], w_down[0]), cs, w_mod[0], b_mod[0], 2 * d)

    hf_t, hb_t, _ = _mlstm(mk, mq_t, mv_t, gates, b_igate[0], b_fgate[0], n_tok, ())

    x1, h2 = _out_proj(att, hf_t, hb_t, mo, m_norm[0].reshape(M_WIDTH, 1), w_out_b, x2, mod_rest,
                       norm2[0].reshape(1, d))

    out = _conv_ffn(h2, w_up_b, conv_w[0], conv_b[0].reshape(1, -1), w_down_b, x1, mod_rest, norm_f.reshape(1, d))
    return out[None]
```

```python
import functools
import math

import jax
import jax.numpy as jnp
import numpy as np
from jax import lax
from jax.experimental import pallas as pl
from jax.experimental.pallas import tpu as pltpu

F32 = jnp.float32
BF16 = jnp.bfloat16

GRID_W = 64
HEAD_DIM = 128
N_Q_HEADS = 8
N_KV_HEADS = 2
Q_PER_KV = N_Q_HEADS // N_KV_HEADS
AXIS_DIM = HEAD_DIM // 2
ROPE_THETA = 10000.0
M_HEADS = 4
M_V = 256
M_QK = 128
N_DIR = 2
NORM_EPS = 1e-6
ATT_WIDTH = N_Q_HEADS * HEAD_DIM
KV_WIDTH = N_KV_HEADS * HEAD_DIM
M_WIDTH = M_HEADS * M_V
MQK_WIDTH = M_HEADS * M_QK
N_GATES = N_DIR * 2 * M_HEADS
N_UNITS = N_DIR * M_HEADS

OFF_AQ = 0
OFF_AK = OFF_AQ + ATT_WIDTH
OFF_AV = OFF_AK + KV_WIDTH
OFF_MQ = OFF_AV + KV_WIDTH
OFF_MK = OFF_MQ + MQK_WIDTH
OFF_MV = OFF_MK + MQK_WIDTH
OFF_MO = OFF_MV + M_WIDTH
OFF_G = OFF_MO + M_WIDTH

LANES = 128
MXU_WIDTH = 256
BF16_SUBLANES = 16
VMEM_LIMIT = 50 * 1024 * 1024

ROW_TILE = 256
MLSTM_CHUNK = 256
ATTN_TQ = 512
ATTN_SCORE_BOUND = 64.0
ATTN_BOUND_HEADROOM = 1.02
ATTN_UNIT_Q = 256
ATTN_TK_CAP = 768
ATTN_LOOKAHEAD = 2
OUT_TM = 512
OUT_ROW_SPLIT = 2
FFN_TM = 512
FFN_TF = 512
HALO = BF16_SUBLANES
MOD_TK = 256
WCAST_TN = 512
LOG2E = 1.4426950408889634


def _largest_divisor(n, cap, mult):
    best = None
    for d in range(mult, cap + 1, mult):
        if n % d == 0:
            best = d
    assert best is not None, (n, cap, mult)
    return best


def _cparams(sem, vmem=VMEM_LIMIT):
    return pltpu.CompilerParams(dimension_semantics=sem, vmem_limit_bytes=vmem)


def _rms(x, eps=NORM_EPS):
    return x * lax.rsqrt(jnp.mean(x * x, axis=-1, keepdims=True) + eps)


def _mod_kernel(cs_ref, w_ref, b_ref, o_ref):
    @pl.when(pl.program_id(0) == 0)
    def _():
        o_ref[0:1, :] = b_ref[...]
        o_ref[1:2, :] = b_ref[...]

    cs = cs_ref[...]
    s = cs * jax.nn.sigmoid(cs)
    w = w_ref[...]
    o_ref[0:1, :] += jnp.sum(s[:, 0:1] * w, axis=0, keepdims=True)
    o_ref[1:2, :] += jnp.sum(s[:, 1:2] * w, axis=0, keepdims=True)


def _modulation(cs, w_mod, b_mod, n):
    d = w_mod.shape[0]
    tk = MOD_TK
    return pl.pallas_call(
        _mod_kernel,
        grid=(d // tk,),
        in_specs=[
            pl.BlockSpec((tk, 2), lambda j: (j, 0)),
            pl.BlockSpec((tk, n), lambda j: (j, 0)),
            pl.BlockSpec((1, n), lambda j: (0, 0)),
        ],
        out_specs=pl.BlockSpec((2, n), lambda j: (0, 0)),
        out_shape=jax.ShapeDtypeStruct((2, n), F32),
        compiler_params=_cparams(("arbitrary",)),
        name="modulation",
    )(cs, w_mod, b_mod.reshape(1, -1))


def _gate_weight_kernel(wt_ref, o_ref):
    g = wt_ref[...]
    hi = g.astype(BF16)
    lo = (g - hi.astype(F32)).astype(BF16)
    pad = jnp.zeros((o_ref.shape[0] - 2 * N_GATES, g.shape[1]), BF16)
    o_ref[...] = jnp.concatenate([hi, lo, pad], axis=0)


def _gate_weights(w_t, row0):
    n, d = w_t.shape
    assert row0 % N_GATES == 0 and row0 + N_GATES == n
    return pl.pallas_call(
        _gate_weight_kernel,
        grid=(1,),
        in_specs=[pl.BlockSpec((N_GATES, d), lambda i: (row0 // N_GATES, 0))],
        out_specs=pl.BlockSpec((LANES, d), lambda i: (0, 0)),
        out_shape=jax.ShapeDtypeStruct((LANES, d), BF16),
        compiler_params=_cparams(("arbitrary",)),
        name="w_gate_split",
    )(w_t)


def _wcast_kernel(wt_ref, o_ref):
    o_ref[...] = wt_ref[...].T.astype(BF16)


def _cast_main_columns(w_t, n_main):
    n, d = w_t.shape
    tn = WCAST_TN
    assert n_main % tn == 0 and n_main <= n
    return pl.pallas_call(
        _wcast_kernel,
        grid=(n_main // tn,),
        in_specs=[pl.BlockSpec((tn, d), lambda j: (j, 0))],
        out_specs=pl.BlockSpec((d, tn), lambda j: (0, j)),
        out_shape=jax.ShapeDtypeStruct((d, n_main), BF16),
        compiler_params=_cparams(("arbitrary",)),
        name="w_in_cast",
    )(w_t)


def _swap_rope_halves(y):
    lane = lax.broadcasted_iota(jnp.int32, y.shape, 1)
    fwd = pltpu.roll(y, LANES - AXIS_DIM // 2, axis=1)
    bwd = pltpu.roll(y, AXIS_DIM // 2, axis=1)
    return jnp.where((lane & (AXIS_DIM // 2)) == 0, fwd, bwd)


def _inproj_kernel(x_ref, ctx_ref, mod_ref, n1_ref, w_ref, wg_ref, qg_ref, kg_ref, cos_ref, sin_ref,
                   qt_ref, k_ref, vt_ref, mqt_ref, mk_ref, mvt_ref, mo_ref, g_ref):
    d = x_ref.shape[1]
    is_ctx = pl.program_id(0) == 0
    xin = jnp.where(is_ctx, ctx_ref[...], x_ref[...])
    shift = jnp.where(is_ctx, mod_ref[1:2, 0:d], mod_ref[0:1, 0:d])
    scale = jnp.where(is_ctx, mod_ref[1:2, d:2 * d], mod_ref[0:1, d:2 * d])
    hf = _rms(xin) * n1_ref[...] * (1.0 + scale) + shift
    h_hi = hf.astype(BF16)

    def proj(c0, width):
        return jnp.dot(h_hi, w_ref[:, c0:c0 + width], preferred_element_type=F32)

    cos = cos_ref[...]
    sin = sin_ref[...]

    def norm_rope(y, gain):
        yn = _rms(y) * gain
        return yn * cos + _swap_rope_halves(yn) * sin

    for j in range(ATT_WIDTH // MXU_WIDTH):
        y = proj(OFF_AQ + j * MXU_WIDTH, MXU_WIDTH)
        for t in range(MXU_WIDTH // HEAD_DIM):
            c = j * MXU_WIDTH + t * HEAD_DIM
            qt_ref[c:c + HEAD_DIM, :] = norm_rope(y[:, t * HEAD_DIM:(t + 1) * HEAD_DIM], qg_ref[...]).T.astype(BF16)
    y = proj(OFF_AK, KV_WIDTH)
    for t in range(N_KV_HEADS):
        k_ref[:, t * HEAD_DIM:(t + 1) * HEAD_DIM] = norm_rope(
            y[:, t * HEAD_DIM:(t + 1) * HEAD_DIM], kg_ref[...]).astype(BF16)
    vt_ref[...] = proj(OFF_AV, KV_WIDTH).T.astype(BF16)
    for j in range(MQK_WIDTH // MXU_WIDTH):
        c = j * MXU_WIDTH
        mqt_ref[c:c + MXU_WIDTH, :] = proj(OFF_MQ + c, MXU_WIDTH).T.astype(BF16)
        mk_ref[:, c:c + MXU_WIDTH] = (proj(OFF_MK + c, MXU_WIDTH) * (M_QK ** -0.5)).astype(BF16)
    for j in range(M_WIDTH // MXU_WIDTH):
        c = j * MXU_WIDTH
        mvt_ref[c:c + MXU_WIDTH, :] = proj(OFF_MV + c, MXU_WIDTH).T.astype(BF16)
        mo_ref[:, c:c + MXU_WIDTH] = jax.nn.sigmoid(proj(OFF_MO + c, MXU_WIDTH)).astype(BF16)
    nt = (((1,), (1,)), ((), ()))
    gg = lax.dot_general(h_hi, wg_ref[...], nt, preferred_element_type=F32)
    gg = gg + pltpu.roll(gg, LANES - N_GATES, axis=1)
    g_ref[...] = gg[:, 0:N_GATES]


def _in_proj(x, ctx, mod, norm1, w_main, w_gate, q_gain, k_gain, cos_t, sin_t):
    t, d = x.shape
    n_ctx = ctx.shape[0]
    assert n_ctx == ROW_TILE and t % ROW_TILE == 0
    rows = n_ctx + t
    steps = rows // ROW_TILE
    full = lambda i: (0, 0)
    row_all = lambda i: (i, 0)
    row_x = lambda i: (jnp.maximum(i - 1, 0), 0)
    out_shape = [
        jax.ShapeDtypeStruct((ATT_WIDTH, t), BF16),
        jax.ShapeDtypeStruct((rows, KV_WIDTH), BF16),
        jax.ShapeDtypeStruct((KV_WIDTH, rows), BF16),
        jax.ShapeDtypeStruct((MQK_WIDTH, rows), BF16),
        jax.ShapeDtypeStruct((rows, MQK_WIDTH), BF16),
        jax.ShapeDtypeStruct((M_WIDTH, rows), BF16),
        jax.ShapeDtypeStruct((t, M_WIDTH), BF16),
        jax.ShapeDtypeStruct((rows, N_GATES), F32),
    ]
    out_specs = [
        pl.BlockSpec((ATT_WIDTH, ROW_TILE), lambda i: (0, jnp.maximum(i - 1, 0))),
        pl.BlockSpec((ROW_TILE, KV_WIDTH), row_all),
        pl.BlockSpec((KV_WIDTH, ROW_TILE), lambda i: (0, i)),
        pl.BlockSpec((MQK_WIDTH, ROW_TILE), lambda i: (0, i)),
        pl.BlockSpec((ROW_TILE, MQK_WIDTH), row_all),
        pl.BlockSpec((M_WIDTH, ROW_TILE), lambda i: (0, i)),
        pl.BlockSpec((ROW_TILE, M_WIDTH), row_x),
        pl.BlockSpec((ROW_TILE, N_GATES), row_all),
    ]
    in_specs = [
        pl.BlockSpec((ROW_TILE, d), row_x),
        pl.BlockSpec((ROW_TILE, d), full),
        pl.BlockSpec(mod.shape, full),
        pl.BlockSpec((1, d), full),
        pl.BlockSpec(w_main.shape, full),
        pl.BlockSpec(w_gate.shape, full),
        pl.BlockSpec((1, HEAD_DIM), full),
        pl.BlockSpec((1, HEAD_DIM), full),
        pl.BlockSpec((ROW_TILE, HEAD_DIM), row_all),
        pl.BlockSpec((ROW_TILE, HEAD_DIM), row_all),
    ]
    return pl.pallas_call(
        _inproj_kernel,
        grid=(steps,),
        in_specs=in_specs,
        out_specs=out_specs,
        out_shape=out_shape,
        compiler_params=_cparams(("arbitrary",)),
        name="in_proj",
    )(x, ctx, mod, norm1, w_main, w_gate, q_gain, k_gain, cos_t, sin_t)


def _attn_kernel(bounded_ref, qt_ref, k_ref, vt_ref, cs_ref, wm_a_ref, wm_b_ref, bm_ref, *rest, tk, n_cast):
    cast_in, (o_ref, modb_ref), rest = rest[:n_cast], rest[n_cast:n_cast + 2], rest[n_cast + 2:]
    cast_out, (acc_scr,) = rest[:n_cast], rest[n_cast:]
    uq = ATTN_UNIT_Q

    @pl.when(jnp.logical_and(pl.program_id(0) == 0, pl.program_id(1) == 0))
    def _():
        modb_ref[0:1, :] = bm_ref[...]
        modb_ref[1:2, :] = bm_ref[...]

    n_sub = qt_ref.shape[1] // uq
    n_chunks = k_ref.shape[0] // tk
    streams = [(r, h) for r in range(n_sub) for h in range(Q_PER_KV)]
    zero_row = jnp.zeros((1, uq), F32)

    def q_t(g):
        r, h = streams[g]
        return qt_ref[h * HEAD_DIM:(h + 1) * HEAD_DIM, r * uq:(r + 1) * uq]

    def finish(l):
        for g, (r, h) in enumerate(streams):
            o_ref[h * HEAD_DIM:(h + 1) * HEAD_DIM, r * uq:(r + 1) * uq] = (acc_scr[g] / l[g]).astype(o_ref.dtype)
        for w_ref, wo_ref in zip(cast_in, cast_out):
            wo_ref[...] = w_ref[...].astype(BF16)
        cs = cs_ref[...]
        s = cs * jax.nn.sigmoid(cs)
        half = wm_a_ref.shape[1]
        for n, wm_ref in enumerate((wm_a_ref, wm_b_ref)):
            wm = wm_ref[...]
            modb_ref[0:1, n * half:(n + 1) * half] += jnp.sum(s[:, 0:1] * wm, axis=0, keepdims=True)
            modb_ref[1:2, n * half:(n + 1) * half] += jnp.sum(s[:, 1:2] * wm, axis=0, keepdims=True)

    @pl.when(bounded_ref[0] == 1)
    def _():
        units = [(c, g) for c in range(n_chunks) for g in range(len(streams))]

        def scores(c, g):
            return jnp.dot(k_ref[c * tk:(c + 1) * tk, :], q_t(g), preferred_element_type=F32)

        l = [zero_row for _ in streams]
        pending = [scores(*u) for u in units[:ATTN_LOOKAHEAD]]
        for n, (c, g) in enumerate(units):
            st = pending.pop(0)
            if n + ATTN_LOOKAHEAD < len(units):
                pending.append(scores(*units[n + ATTN_LOOKAHEAD]))
            p = jnp.exp2(st)
            l[g] = l[g] + jnp.sum(p, axis=0, keepdims=True)
            pv = jnp.dot(vt_ref[:, c * tk:(c + 1) * tk], p.astype(BF16), preferred_element_type=F32)
            if c == 0:
                acc_scr[g] = pv
            else:
                acc_scr[g] += pv
        finish(l)

    @pl.when(bounded_ref[0] == 0)
    def _():
        acc_scr[...] = jnp.zeros(acc_scr.shape, F32)

        def chunk(c, carry):
            m, l = carry
            r0 = pl.multiple_of(c * tk, tk)
            kc = k_ref[pl.ds(r0, tk), :]
            vtc = vt_ref[:, pl.ds(r0, tk)]
            m_out, l_out = [], []
            for g in range(len(streams)):
                st = jnp.dot(kc, q_t(g), preferred_element_type=F32)
                m_new = jnp.maximum(m[g], jnp.max(st, axis=0, keepdims=True))
                p = jnp.exp2(st - m_new)
                alpha = jnp.exp2(m[g] - m_new)
                l_out.append(alpha * l[g] + jnp.sum(p, axis=0, keepdims=True))
                acc_scr[g] = alpha * acc_scr[g] + jnp.dot(vtc, p.astype(BF16), preferred_element_type=F32)
                m_out.append(m_new)
            return tuple(m_out), tuple(l_out)

        init = (tuple(jnp.full((1, uq), -jnp.inf, F32) for _ in streams), tuple(zero_row for _ in streams))
        _, l = lax.fori_loop(0, n_chunks, chunk, init)
        finish(l)


def _attention(q_t, k, vt, bounded, cast_weights, cs, w_mod, b_mod, n_done):
    t = q_t.shape[1]
    s_len = k.shape[0]
    tq = ATTN_TQ
    tk = _largest_divisor(s_len, ATTN_TK_CAP, MXU_WIDTH)
    group_w = Q_PER_KV * HEAD_DIM
    n_i = t // tq
    steps = N_KV_HEADS * n_i
    step = lambda h, i: h * n_i + i
    n_rest = w_mod.shape[1] - n_done
    half = n_rest // 2
    assert n_done % half == 0 and w_mod.shape[0] % steps == 0
    mod_rows = w_mod.shape[0] // steps
    mod_specs = [
        pl.BlockSpec((mod_rows, 2), lambda h, i: (step(h, i), 0)),
        pl.BlockSpec((mod_rows, half), lambda h, i: (step(h, i), n_done // half)),
        pl.BlockSpec((mod_rows, half), lambda h, i: (step(h, i), n_done // half + 1)),
        pl.BlockSpec((1, n_rest), lambda h, i: (0, 0)),
    ]
    cast_specs, cast_shapes = [], []
    for w in cast_weights:
        assert w.shape[0] % (steps * BF16_SUBLANES) == 0, w.shape
        cast_specs.append(pl.BlockSpec((w.shape[0] // steps, w.shape[1]), lambda h, i: (step(h, i), 0)))
        cast_shapes.append(jax.ShapeDtypeStruct(w.shape, BF16))
    outs = pl.pallas_call(
        functools.partial(_attn_kernel, tk=tk, n_cast=len(cast_weights)),
        grid=(N_KV_HEADS, n_i),
        in_specs=[
            pl.BlockSpec(memory_space=pltpu.SMEM),
            pl.BlockSpec((group_w, tq), lambda h, i: (h, i)),
            pl.BlockSpec((s_len, HEAD_DIM), lambda h, i: (0, h), pipeline_mode=pl.Buffered(1)),
            pl.BlockSpec((HEAD_DIM, s_len), lambda h, i: (h, 0), pipeline_mode=pl.Buffered(1)),
        ] + mod_specs + cast_specs,
        out_specs=[pl.BlockSpec((group_w, tq), lambda h, i: (h, i)),
                   pl.BlockSpec((2, n_rest), lambda h, i: (0, 0))] + cast_specs,
        out_shape=[jax.ShapeDtypeStruct((ATT_WIDTH, t), BF16),
                   jax.ShapeDtypeStruct((2, n_rest), F32)] + cast_shapes,
        scratch_shapes=[
            pltpu.VMEM((Q_PER_KV * tq // ATTN_UNIT_Q, HEAD_DIM, ATTN_UNIT_Q), F32),
        ],
        compiler_params=_cparams(("arbitrary", "arbitrary")),
        name="attention",
    )(bounded, q_t, k, vt, cs, w_mod, w_mod, b_mod[n_done:].reshape(1, n_rest), *cast_weights)
    return outs[0], outs[1], tuple(outs[2:])


def _log_sigmoid(x):
    return jnp.minimum(x, 0.0) - jnp.log1p(jnp.exp(-jnp.abs(x)))


def _scan(x, axis, op, ident, reverse):
    n = x.shape[axis]
    idx = lax.broadcasted_iota(jnp.int32, x.shape, axis)
    shift = 1
    while shift < n:
        if reverse:
            moved = pltpu.roll(x, n - shift, axis=axis)
            ok = idx < n - shift
        else:
            moved = pltpu.roll(x, shift, axis=axis)
            ok = idx >= shift
        x = op(x, jnp.where(ok, moved, ident))
        shift *= 2
    return x


class _MlstmDirection:
    def __init__(self, reverse, k_ref, qt_ref, vt_ref, gi_ref, gf_ref, git_ref, gft_ref, h_ref,
                 ct_scr, mrow_scr, mcol_scr, visible):
        self.reverse, self.visible = reverse, visible
        self.k_ref, self.qt_ref, self.vt_ref, self.h_ref = k_ref, qt_ref, vt_ref, h_ref
        self.gi_ref, self.gf_ref, self.git_ref, self.gft_ref = gi_ref, gf_ref, git_ref, gft_ref
        self.ct_scr, self.mrow_scr, self.mcol_scr = ct_scr, mrow_scr, mcol_scr
        self.units = [((M_HEADS if reverse else 0) + hd, hd) for hd in range(M_HEADS)]

    def _k(self, hd):
        return self.k_ref[:, hd * M_QK:(hd + 1) * M_QK]

    def _vt(self, hd):
        return self.vt_ref[hd * M_V:(hd + 1) * M_V, :]

    def state_free_matmuls(self):
        self.early = []
        for u, hd in self.units:
            qt = self.qt_ref[hd * M_QK:(hd + 1) * M_QK, :]
            state = self.ct_scr[u]
            qk_t = jnp.dot(self._k(hd), qt, preferred_element_type=F32)
            inter_t = jnp.dot(state.astype(BF16), qt, preferred_element_type=F32)
            self.early.append((state, qk_t, inter_t))

    def gate_terms(self, brow_ref, bcol_ref):
        add = lambda a, b: a + b
        reverse = self.reverse
        i_c = self.gi_ref[...] + brow_ref[0:1, :]
        ls_c = _log_sigmoid(self.gf_ref[...] + brow_ref[1:2, :])
        self.a_c = i_c - _scan(ls_c, 0, add, 0.0, reverse)
        m_prev_r = self.mrow_scr[0:1, 0:N_UNITS]
        mm_r = jnp.maximum(m_prev_r, jnp.max(self.a_c, axis=0, keepdims=True))
        self.kscale_c = jnp.exp(self.a_c - mm_r)
        self.w_prev_r = jnp.exp(m_prev_r - mm_r)
        self.mrow_scr[0:1, 0:N_UNITS] = jnp.sum(ls_c, axis=0, keepdims=True) + mm_r
        i_r = self.git_ref[...] + bcol_ref[:, 0:1]
        ls_r = _log_sigmoid(self.gft_ref[...] + bcol_ref[:, 1:2])
        cum_r = _scan(ls_r, 1, add, 0.0, reverse)
        a_r = i_r - cum_r
        m_prev_c = self.mcol_scr[0:N_UNITS, 0:1]
        m_t = cum_r + jnp.maximum(_scan(a_r, 1, jnp.maximum, -jnp.inf, reverse), m_prev_c)
        self.r_r = cum_r - m_t
        self.w_inter_r = jnp.exp(cum_r + m_prev_c - m_t)
        self.floor_r = jnp.exp(-m_t)
        self.mcol_scr[0:N_UNITS, 0:1] = (jnp.sum(ls_r, axis=1, keepdims=True)
                                         + jnp.maximum(m_prev_c, jnp.max(a_r, axis=1, keepdims=True)))

    def state_update(self):
        for (u, hd), (state, _, _) in zip(self.units, self.early):
            kw = self._k(hd).astype(F32) * self.kscale_c[:, u:u + 1]
            w_prev = self.w_prev_r[0:1, u:u + 1]
            self.ct_scr[u, 0:M_V, :] = (w_prev * state[0:M_V]
                                        + jnp.dot(self._vt(hd), kw.astype(BF16), preferred_element_type=F32))
            self.ct_scr[u, M_V:M_V + 1, :] = w_prev * state[M_V:M_V + 1] + jnp.sum(kw, axis=0, keepdims=True)

    def outputs(self):
        for (u, hd), (_, qk_t, inter_t) in zip(self.units, self.early):
            decay_t = jnp.exp(jnp.where(self.visible, self.a_c[:, u:u + 1], -jnp.inf) + self.r_r[u:u + 1, :])
            s_t = qk_t * decay_t
            w_inter = self.w_inter_r[u:u + 1, :]
            num_t = (inter_t[0:M_V] * w_inter
                     + jnp.dot(self._vt(hd), s_t.astype(BF16), preferred_element_type=F32))
            den = inter_t[M_V:M_V + 1] * w_inter + jnp.sum(s_t, axis=0, keepdims=True)
            self.h_ref[hd * M_V:(hd + 1) * M_V, :] = (num_t * (
                1.0 / jnp.maximum(jnp.abs(den), self.floor_r[u:u + 1, :]))).astype(self.h_ref.dtype)


def _mlstm_kernel(kf_ref, qtf_ref, vtf_ref, gif_ref, gff_ref, gitf_ref, gftf_ref,
                  kb_ref, qtb_ref, vtb_ref, gib_ref, gfb_ref, gitb_ref, gftb_ref,
                  brow_ref, bcol_ref, *rest, n_cast):
    cast_in, (hf_ref, hb_ref), rest = rest[:n_cast], rest[n_cast:n_cast + 2], rest[n_cast + 2:]
    cast_out, (ct_scr, mrow_scr, mcol_scr) = rest[:n_cast], rest[n_cast:]
    L = kf_ref.shape[0]

    @pl.when(pl.program_id(0) == 0)
    def _():
        ct_scr[...] = jnp.zeros(ct_scr.shape, F32)
        mrow_scr[...] = jnp.zeros(mrow_scr.shape, F32)
        mcol_scr[...] = jnp.zeros(mcol_scr.shape, F32)

    src = lax.broadcasted_iota(jnp.int32, (L, L), 0)
    tgt = lax.broadcasted_iota(jnp.int32, (L, L), 1)
    dirs = (
        _MlstmDirection(False, kf_ref, qtf_ref, vtf_ref, gif_ref, gff_ref, gitf_ref, gftf_ref, hf_ref,
                        ct_scr, mrow_scr.at[0], mcol_scr.at[0], src <= tgt),
        _MlstmDirection(True, kb_ref, qtb_ref, vtb_ref, gib_ref, gfb_ref, gitb_ref, gftb_ref, hb_ref,
                        ct_scr, mrow_scr.at[1], mcol_scr.at[1], src >= tgt),
    )
    for d in dirs:
        d.state_free_matmuls()
    for d in dirs:
        d.gate_terms(brow_ref, bcol_ref)
    for d in dirs:
        d.state_update()
    for d in dirs:
        d.outputs()
    for w_ref, o_ref in zip(cast_in, cast_out):
        o_ref[...] = w_ref[...].astype(BF16)


def _mlstm(mk, mq_t, mv_t, gates, b_i, b_f, n_x, cast_weights):
    L = MLSTM_CHUNK
    rows = mk.shape[0]
    assert rows % L == 0 and n_x % L == 0 and rows - n_x == L
    nx = n_x // L
    steps = nx + 1
    g4 = gates.reshape(rows, N_DIR, 2, M_HEADS)
    g_in = g4[:, :, 0, :].reshape(rows, N_UNITS)
    g_fg = g4[:, :, 1, :].reshape(rows, N_UNITS)
    bias_row = jnp.stack([b_i.reshape(N_UNITS), b_f.reshape(N_UNITS)], axis=0)
    f_chunk = lambda g: g
    b_chunk = lambda g: jnp.where(g == 0, 0, nx + 1 - g)
    f_out = lambda g: (0, jnp.maximum(g - 1, 0))
    b_out = lambda g: (0, jnp.where(g == 0, nx - 1, nx - g))
    f_out_rows = lambda g: (jnp.maximum(g - 1, 0), 0)
    small = lambda g: (0, 0)

    def stream(chunk):
        return [
            pl.BlockSpec((L, MQK_WIDTH), lambda g: (chunk(g), 0)),
            pl.BlockSpec((MQK_WIDTH, L), lambda g: (0, chunk(g))),
            pl.BlockSpec((M_WIDTH, L), lambda g: (0, chunk(g))),
            pl.BlockSpec((L, N_UNITS), lambda g: (chunk(g), 0)),
            pl.BlockSpec((L, N_UNITS), lambda g: (chunk(g), 0)),
            pl.BlockSpec((N_UNITS, L), lambda g: (0, chunk(g))),
            pl.BlockSpec((N_UNITS, L), lambda g: (0, chunk(g))),
        ]

    operands = (mk, mq_t, mv_t, g_in, g_fg, g_in.T, g_fg.T)
    cast_specs, cast_shapes = [], []
    for w in cast_weights:
        assert w.shape[0] % (nx * BF16_SUBLANES) == 0, w.shape
        blk = (w.shape[0] // nx, w.shape[1])
        cast_specs.append(pl.BlockSpec(blk, f_out_rows))
        cast_shapes.append(jax.ShapeDtypeStruct(w.shape, BF16))
    outs = pl.pallas_call(
        functools.partial(_mlstm_kernel, n_cast=len(cast_weights)),
        grid=(steps,),
        in_specs=stream(f_chunk) + stream(b_chunk) + [pl.BlockSpec((2, N_UNITS), small),
                                                       pl.BlockSpec((N_UNITS, 2), small)] + cast_specs,
        out_specs=[pl.BlockSpec((M_WIDTH, L), f_out), pl.BlockSpec((M_WIDTH, L), b_out)] + cast_specs,
        out_shape=[jax.ShapeDtypeStruct((M_WIDTH, n_x), BF16)] * 2 + cast_shapes,
        scratch_shapes=[
            pltpu.VMEM((N_UNITS, M_V + BF16_SUBLANES, M_QK), F32),
            pltpu.VMEM((N_DIR, 8, LANES), F32),
            pltpu.VMEM((N_DIR, 8, LANES), F32),
        ],
        compiler_params=_cparams(("arbitrary",)),
        name="mlstm",
    )(*operands, *operands, bias_row, bias_row.T, *cast_weights)
    return outs[0], outs[1], tuple(outs[2:])


def _outproj_kernel(att_ref, hft_ref, hbt_ref, mo_ref, mg_ref, w_ref, x_ref, mod_ref, n2_ref,
                    x1_ref, h2_ref):
    d = x_ref.shape[1]
    tm = x_ref.shape[0]
    rows = tm // OUT_ROW_SPLIT

    def project(r0):
        y = lax.dot_general(att_ref[:, r0:r0 + rows], w_ref[0:ATT_WIDTH, :], (((0,), (0,)), ((), ())),
                            preferred_element_type=F32)
        for hd in range(M_HEADS):
            c = hd * M_V
            ht = (hft_ref[c:c + M_V, r0:r0 + rows].astype(F32)
                  + hbt_ref[c:c + M_V, r0:r0 + rows].astype(F32))
            ht = ht * lax.rsqrt(jnp.mean(ht * ht, axis=0, keepdims=True) + NORM_EPS) * mg_ref[c:c + M_V, :]
            r = (ht.T * mo_ref[r0:r0 + rows, c:c + M_V].astype(F32)).astype(BF16)
            y = y + jnp.dot(r, w_ref[ATT_WIDTH + c:ATT_WIDTH + c + M_V, :], preferred_element_type=F32)
        return y

    ys = [project(b * rows) for b in range(OUT_ROW_SPLIT)]
    for b, y in enumerate(ys):
        r0 = b * rows
        x1 = x_ref[r0:r0 + rows, :] + mod_ref[0:1, 0:d] * y
        x1_ref[r0:r0 + rows, :] = x1
        h2 = _rms(x1) * n2_ref[...] * (1.0 + mod_ref[0:1, 2 * d:3 * d]) + mod_ref[0:1, d:2 * d]
        h2_ref[r0:r0 + rows, :] = h2.astype(BF16)


def _out_proj(att, hf_t, hb_t, mo, m_gain, w_out, x, mod, norm2):
    t, d = x.shape
    tm = OUT_TM
    row = lambda i: (i, 0)
    full = lambda i: (0, 0)
    return pl.pallas_call(
        _outproj_kernel,
        grid=(t // tm,),
        in_specs=[
            pl.BlockSpec((ATT_WIDTH, tm), lambda i: (0, i)),
            pl.BlockSpec((M_WIDTH, tm), lambda i: (0, i)),
            pl.BlockSpec((M_WIDTH, tm), lambda i: (0, i)),
            pl.BlockSpec((tm, M_WIDTH), row),
            pl.BlockSpec((M_WIDTH, 1), full),
            pl.BlockSpec(w_out.shape, full),
            pl.BlockSpec((tm, d), row),
            pl.BlockSpec(mod.shape, full),
            pl.BlockSpec((1, d), full),
        ],
        out_specs=[pl.BlockSpec((tm, d), row), pl.BlockSpec((tm, d), row)],
        out_shape=[jax.ShapeDtypeStruct((t, d), F32), jax.ShapeDtypeStruct((t, d), BF16)],
        compiler_params=_cparams(("arbitrary",)),
        name="out_proj",
    )(att, hf_t, hb_t, mo, m_gain, w_out, x, mod, norm2)


def _ffn_kernel(h_ref, hp_ref, hn_ref, wg_ref, wv_ref, cwg_ref, cwv_ref, cbg_ref, cbv_ref, wd_ref,
                x1_ref, mod_ref, nf_ref, o_ref, hext_ref, ug_ref, uv_ref):
    i = pl.program_id(0)
    j = pl.program_id(1)
    tm = h_ref.shape[0]
    d = x1_ref.shape[1]

    @pl.when(j == 0)
    def _():
        hext_ref[0:HALO, :] = jnp.where(i == 0, jnp.zeros_like(hp_ref[...]), hp_ref[...])
        hext_ref[HALO:HALO + tm, :] = h_ref[...]
        hext_ref[HALO + tm:, :] = jnp.where(i == pl.num_programs(0) - 1, jnp.zeros_like(hn_ref[...]), hn_ref[...])
        o_ref[...] = jnp.zeros(o_ref.shape, F32)

    split = tm // 2 + HALO
    cut = tm // 2 - HALO

    def conv(u_ref, cw_ref, cb_ref, r0, n):
        return (cw_ref[0:1, :] * u_ref[HALO - 1 + r0:HALO - 1 + r0 + n, :]
                + cw_ref[1:2, :] * u_ref[HALO + r0:HALO + r0 + n, :]
                + cw_ref[2:3, :] * u_ref[HALO + 1 + r0:HALO + 1 + r0 + n, :]
                + cb_ref[...])

    def hidden_block(final):
        for r0, r1 in ((0, split), (split, tm + 2 * HALO)):
            ug_ref[r0:r1, :] = jnp.dot(hext_ref[r0:r1, :], wg_ref[...], preferred_element_type=F32)
            uv_ref[r0:r1, :] = jnp.dot(hext_ref[r0:r1, :], wv_ref[...], preferred_element_type=F32)
        for r0, n in ((0, cut), (cut, tm - cut)):
            g = conv(ug_ref, cwg_ref, cbg_ref, r0, n)
            val = conv(uv_ref, cwv_ref, cbv_ref, r0, n)
            a = (g * jax.nn.sigmoid(g) * val).astype(BF16)
            acc = o_ref[r0:r0 + n, :] + jnp.dot(a, wd_ref[...], preferred_element_type=F32)
            if final:
                y = x1_ref[r0:r0 + n, :] + mod_ref[0:1, 3 * d:4 * d] * acc
                acc = _rms(y) * nf_ref[...]
            o_ref[r0:r0 + n, :] = acc

    last = pl.num_programs(1) - 1
    pl.when(j != last)(functools.partial(hidden_block, False))
    pl.when(j == last)(functools.partial(hidden_block, True))


def _conv_ffn(h2, w_up, conv_w, conv_b, w_down, x1, mod, norm_f):
    t, d = h2.shape
    d_ff = w_down.shape[0]
    tm, tf = FFN_TM, FFN_TF
    nf = d_ff // tf
    hb = tm // HALO
    last_halo = t // HALO - 1
    in_specs = [
        pl.BlockSpec((tm, d), lambda i, j: (i, 0)),
        pl.BlockSpec((HALO, d), lambda i, j: (jnp.maximum(i * hb - 1, 0), 0)),
        pl.BlockSpec((HALO, d), lambda i, j: (jnp.minimum((i + 1) * hb, last_halo), 0)),
        pl.BlockSpec((d, tf), lambda i, j: (0, j)),
        pl.BlockSpec((d, tf), lambda i, j: (0, j + nf)),
        pl.BlockSpec((3, tf), lambda i, j: (0, j)),
        pl.BlockSpec((3, tf), lambda i, j: (0, j + nf)),
        pl.BlockSpec((1, tf), lambda i, j: (0, j)),
        pl.BlockSpec((1, tf), lambda i, j: (0, j + nf)),
        pl.BlockSpec((tf, d), lambda i, j: (j, 0)),
        pl.BlockSpec((tm, d), lambda i, j: (i, 0)),
        pl.BlockSpec(mod.shape, lambda i, j: (0, 0)),
        pl.BlockSpec((1, d), lambda i, j: (0, 0)),
    ]
    return pl.pallas_call(
        _ffn_kernel,
        grid=(t // tm, nf),
        in_specs=in_specs,
        out_specs=pl.BlockSpec((tm, d), lambda i, j: (i, 0)),
        out_shape=jax.ShapeDtypeStruct((t, d), F32),
        scratch_shapes=[
            pltpu.VMEM((tm + 2 * HALO, d), BF16),
            pltpu.VMEM((tm + 2 * HALO, tf), F32),
            pltpu.VMEM((tm + 2 * HALO, tf), F32),
        ],
        compiler_params=_cparams(("arbitrary", "arbitrary")),
        name="conv_ffn",
    )(h2, h2, h2, w_up, w_up, conv_w, conv_w, conv_b, conv_b, w_down, x1, mod, norm_f)


def _rope_tables(n_ctx, n_tok):
    f32 = np.float32
    rows = n_tok // GRID_W
    row = np.repeat(np.arange(rows, dtype=f32), GRID_W)
    col = np.tile(np.arange(GRID_W, dtype=f32), rows)
    inv_freq = np.power(f32(ROPE_THETA), -np.arange(0, AXIS_DIM, 2, dtype=f32) / f32(AXIS_DIM)).astype(f32)
    ang_r = row[:, None] * inv_freq[None, :]
    ang_c = col[:, None] * inv_freq[None, :]
    cos = np.concatenate([np.cos(ang_r), np.cos(ang_r), np.cos(ang_c), np.cos(ang_c)], axis=1)
    sin = np.concatenate([-np.sin(ang_r), np.sin(ang_r), -np.sin(ang_c), np.sin(ang_c)], axis=1)
    cos = np.concatenate([np.ones((n_ctx, HEAD_DIM), f32), cos], axis=0).astype(f32)
    sin = np.concatenate([np.zeros((n_ctx, HEAD_DIM), f32), sin], axis=0).astype(f32)
    return jnp.asarray(cos), jnp.asarray(sin)


def kernel(x, c, ctx, c_ctx, w_mod, b_mod, norm1, w_in, q_norm, k_norm, b_igate, b_fgate, m_norm,
           w_out, norm2, w_up, conv_w, conv_b, w_down, norm_f):
    batch, n_tok, d = x.shape
    assert batch == 1 and w_mod.shape[0] == 1
    n_ctx = ctx.shape[1]
    x2 = x[0]
    ctx2 = ctx[0]

    cs = jnp.stack([c[0], c_ctx], axis=1)
    mod = _modulation(cs, w_mod[0], b_mod[0], 2 * d)

    w_in_t = w_in[0].T
    w_main = _cast_main_columns(w_in_t, OFF_G)
    w_gate = _gate_weights(w_in_t, OFF_G)
    cos_t, sin_t = _rope_tables(n_ctx, n_tok)
    q_gain = (q_norm[0] * (HEAD_DIM ** -0.5 * LOG2E)).reshape(1, HEAD_DIM)
    k_gain = k_norm[0].reshape(1, HEAD_DIM)
    q_t, k, vt, mq_t, mk, mv_t, mo, gates = _in_proj(
        x2, ctx2, mod, norm1[0].reshape(1, d), w_main, w_gate, q_gain, k_gain, cos_t, sin_t)

    score_bound = HEAD_DIM * jnp.max(jnp.abs(q_gain)) * jnp.max(jnp.abs(k_gain)) * ATTN_BOUND_HEADROOM
    bounded = (score_bound <= ATTN_SCORE_BOUND).astype(jnp.int32).reshape(1)
    att, mod_rest, (w_out_b, w_up_b, w_down_b) = _attention(
        q_t, k, vt, bounded, (w_out[0], w_up[0], w_down[0]), cs, w_mod[0], b_mod[0], 2 * d)

    hf_t, hb_t, _ = _mlstm(mk, mq_t, mv_t, gates, b_igate[0], b_fgate[0], n_tok, ())

    x1, h2 = _out_proj(att, hf_t, hb_t, mo, m_norm[0].reshape(M_WIDTH, 1), w_out_b, x2, mod_rest,
                       norm2[0].reshape(1, d))

    out = _conv_ffn(h2, w_up_b, conv_w[0], conv_b[0].reshape(1, -1), w_down_b, x1, mod_rest, norm_f.reshape(1, d))
    return out[None]
```

```python
import functools
import math

import jax
import jax.numpy as jnp
import numpy as np
from jax import lax
from jax.experimental import pallas as pl
from jax.experimental.pallas import tpu as pltpu

F32 = jnp.float32
BF16 = jnp.bfloat16

GRID_W = 64
HEAD_DIM = 128
N_Q_HEADS = 8
N_KV_HEADS = 2
Q_PER_KV = N_Q_HEADS // N_KV_HEADS
AXIS_DIM = HEAD_DIM // 2
ROPE_THETA = 10000.0
M_HEADS = 4
M_V = 256
M_QK = 128
N_DIR = 2
NORM_EPS = 1e-6
ATT_WIDTH = N_Q_HEADS * HEAD_DIM
KV_WIDTH = N_KV_HEADS * HEAD_DIM
M_WIDTH = M_HEADS * M_V
MQK_WIDTH = M_HEADS * M_QK
N_GATES = N_DIR * 2 * M_HEADS
N_UNITS = N_DIR * M_HEADS

OFF_AQ = 0
OFF_AK = OFF_AQ + ATT_WIDTH
OFF_AV = OFF_AK + KV_WIDTH
OFF_MQ = OFF_AV + KV_WIDTH
OFF_MK = OFF_MQ + MQK_WIDTH
OFF_MV = OFF_MK + MQK_WIDTH
OFF_MO = OFF_MV + M_WIDTH
OFF_G = OFF_MO + M_WIDTH

LANES = 128
MXU_WIDTH = 256
BF16_SUBLANES = 16
VMEM_LIMIT = 50 * 1024 * 1024

ROW_TILE = 256
MLSTM_CHUNK = 256
ATTN_TQ = 512
ATTN_SCORE_BOUND = 64.0
ATTN_BOUND_HEADROOM = 1.02
ATTN_UNIT_Q = 256
ATTN_TK_CAP = 768
ATTN_LOOKAHEAD = 2
OUT_TM = 512
OUT_ROW_SPLIT = 2
FFN_TM = 512
FFN_TF = 512
HALO = BF16_SUBLANES
MOD_TK = 256
WCAST_TN = 512
LOG2E = 1.4426950408889634


def _largest_divisor(n, cap, mult):
    best = None
    for d in range(mult, cap + 1, mult):
        if n % d == 0:
            best = d
    assert best is not None, (n, cap, mult)
    return best


def _cparams(sem, vmem=VMEM_LIMIT):
    return pltpu.CompilerParams(dimension_semantics=sem, vmem_limit_bytes=vmem)


def _rms(x, eps=NORM_EPS):
    return x * lax.rsqrt(jnp.mean(x * x, axis=-1, keepdims=True) + eps)


def _mod_kernel(cs_ref, w_ref, b_ref, o_ref):
    @pl.when(pl.program_id(0) == 0)
    def _():
        o_ref[0:1, :] = b_ref[...]
        o_ref[1:2, :] = b_ref[...]

    cs = cs_ref[...]
    s = cs * jax.nn.sigmoid(cs)
    w = w_ref[...]
    o_ref[0:1, :] += jnp.sum(s[:, 0:1] * w, axis=0, keepdims=True)
    o_ref[1:2, :] += jnp.sum(s[:, 1:2] * w, axis=0, keepdims=True)


def _modulation(cs, w_mod, b_mod, n):
    d = w_mod.shape[0]
    tk = MOD_TK
    return pl.pallas_call(
        _mod_kernel,
        grid=(d // tk,),
        in_specs=[
            pl.BlockSpec((tk, 2), lambda j: (j, 0)),
            pl.BlockSpec((tk, n), lambda j: (j, 0)),
            pl.BlockSpec((1, n), lambda j: (0, 0)),
        ],
        out_specs=pl.BlockSpec((2, n), lambda j: (0, 0)),
        out_shape=jax.ShapeDtypeStruct((2, n), F32),
        compiler_params=_cparams(("arbitrary",)),
        name="modulation",
    )(cs, w_mod, b_mod.reshape(1, -1))


def _gate_weight_kernel(wt_ref, o_ref):
    g = wt_ref[...]
    hi = g.astype(BF16)
    lo = (g - hi.astype(F32)).astype(BF16)
    pad = jnp.zeros((o_ref.shape[0] - 2 * N_GATES, g.shape[1]), BF16)
    o_ref[...] = jnp.concatenate([hi, lo, pad], axis=0)


def _gate_weights(w_t, row0):
    n, d = w_t.shape
    assert row0 % N_GATES == 0 and row0 + N_GATES == n
    return pl.pallas_call(
        _gate_weight_kernel,
        grid=(1,),
        in_specs=[pl.BlockSpec((N_GATES, d), lambda i: (row0 // N_GATES, 0))],
        out_specs=pl.BlockSpec((LANES, d), lambda i: (0, 0)),
        out_shape=jax.ShapeDtypeStruct((LANES, d), BF16),
        compiler_params=_cparams(("arbitrary",)),
        name="w_gate_split",
    )(w_t)


def _wcast_kernel(wt_ref, o_ref):
    o_ref[...] = wt_ref[...].T.astype(BF16)


def _cast_main_columns(w_t, n_main):
    n, d = w_t.shape
    tn = WCAST_TN
    assert n_main % tn == 0 and n_main <= n
    return pl.pallas_call(
        _wcast_kernel,
        grid=(n_main // tn,),
        in_specs=[pl.BlockSpec((tn, d), lambda j: (j, 0))],
        out_specs=pl.BlockSpec((d, tn), lambda j: (0, j)),
        out_shape=jax.ShapeDtypeStruct((d, n_main), BF16),
        compiler_params=_cparams(("arbitrary",)),
        name="w_in_cast",
    )(w_t)


def _swap_rope_halves(y):
    lane = lax.broadcasted_iota(jnp.int32, y.shape, 1)
    fwd = pltpu.roll(y, LANES - AXIS_DIM // 2, axis=1)
    bwd = pltpu.roll(y, AXIS_DIM // 2, axis=1)
    return jnp.where((lane & (AXIS_DIM // 2)) == 0, fwd, bwd)


def _inproj_kernel(x_ref, ctx_ref, mod_ref, n1_ref, w_ref, wg_ref, qg_ref, kg_ref, cos_ref, sin_ref,
                   qt_ref, k_ref, vt_ref, mqt_ref, mk_ref, mvt_ref, mo_ref, g_ref):
    d = x_ref.shape[1]
    is_ctx = pl.program_id(0) == 0
    xin = jnp.where(is_ctx, ctx_ref[...], x_ref[...])
    shift = jnp.where(is_ctx, mod_ref[1:2, 0:d], mod_ref[0:1, 0:d])
    scale = jnp.where(is_ctx, mod_ref[1:2, d:2 * d], mod_ref[0:1, d:2 * d])
    hf = _rms(xin) * n1_ref[...] * (1.0 + scale) + shift
    h_hi = hf.astype(BF16)

    def proj(c0, width):
        return jnp.dot(h_hi, w_ref[:, c0:c0 + width], preferred_element_type=F32)

    cos = cos_ref[...]
    sin = sin_ref[...]

    def norm_rope(y, gain):
        yn = _rms(y) * gain
        return yn * cos + _swap_rope_halves(yn) * sin

    for j in range(ATT_WIDTH // MXU_WIDTH):
        y = proj(OFF_AQ + j * MXU_WIDTH, MXU_WIDTH)
        for t in range(MXU_WIDTH // HEAD_DIM):
            c = j * MXU_WIDTH + t * HEAD_DIM
            qt_ref[c:c + HEAD_DIM, :] = norm_rope(y[:, t * HEAD_DIM:(t + 1) * HEAD_DIM], qg_ref[...]).T.astype(BF16)
    y = proj(OFF_AK, KV_WIDTH)
    for t in range(N_KV_HEADS):
        k_ref[:, t * HEAD_DIM:(t + 1) * HEAD_DIM] = norm_rope(
            y[:, t * HEAD_DIM:(t + 1) * HEAD_DIM], kg_ref[...]).astype(BF16)
    vt_ref[...] = proj(OFF_AV, KV_WIDTH).T.astype(BF16)
    for j in range(MQK_WIDTH // MXU_WIDTH):
        c = j * MXU_WIDTH
        mqt_ref[c:c + MXU_WIDTH, :] = proj(OFF_MQ + c, MXU_WIDTH).T.astype(BF16)
        mk_ref[:, c:c + MXU_WIDTH] = (proj(OFF_MK + c, MXU_WIDTH) * (M_QK ** -0.5)).astype(BF16)
    for j in range(M_WIDTH // MXU_WIDTH):
        c = j * MXU_WIDTH
        mvt_ref[c:c + MXU_WIDTH, :] = proj(OFF_MV + c, MXU_WIDTH).T.astype(BF16)
        mo_ref[:, c:c + MXU_WIDTH] = jax.nn.sigmoid(proj(OFF_MO + c, MXU_WIDTH)).astype(BF16)
    nt = (((1,), (1,)), ((), ()))
    gg = lax.dot_general(h_hi, wg_ref[...], nt, preferred_element_type=F32)
    gg = gg + pltpu.roll(gg, LANES - N_GATES, axis=1)
    g_ref[...] = gg[:, 0:N_GATES]


def _in_proj(x, ctx, mod, norm1, w_main, w_gate, q_gain, k_gain, cos_t, sin_t):
    t, d = x.shape
    n_ctx = ctx.shape[0]
    assert n_ctx == ROW_TILE and t % ROW_TILE == 0
    rows = n_ctx + t
    steps = rows // ROW_TILE
    full = lambda i: (0, 0)
    row_all = lambda i: (i, 0)
    row_x = lambda i: (jnp.maximum(i - 1, 0), 0)
    out_shape = [
        jax.ShapeDtypeStruct((ATT_WIDTH, t), BF16),
        jax.ShapeDtypeStruct((rows, KV_WIDTH), BF16),
        jax.ShapeDtypeStruct((KV_WIDTH, rows), BF16),
        jax.ShapeDtypeStruct((MQK_WIDTH, rows), BF16),
        jax.ShapeDtypeStruct((rows, MQK_WIDTH), BF16),
        jax.ShapeDtypeStruct((M_WIDTH, rows), BF16),
        jax.ShapeDtypeStruct((t, M_WIDTH), BF16),
        jax.ShapeDtypeStruct((rows, N_GATES), F32),
    ]
    out_specs = [
        pl.BlockSpec((ATT_WIDTH, ROW_TILE), lambda i: (0, jnp.maximum(i - 1, 0))),
        pl.BlockSpec((ROW_TILE, KV_WIDTH), row_all),
        pl.BlockSpec((KV_WIDTH, ROW_TILE), lambda i: (0, i)),
        pl.BlockSpec((MQK_WIDTH, ROW_TILE), lambda i: (0, i)),
        pl.BlockSpec((ROW_TILE, MQK_WIDTH), row_all),
        pl.BlockSpec((M_WIDTH, ROW_TILE), lambda i: (0, i)),
        pl.BlockSpec((ROW_TILE, M_WIDTH), row_x),
        pl.BlockSpec((ROW_TILE, N_GATES), row_all),
    ]
    in_specs = [
        pl.BlockSpec((ROW_TILE, d), row_x),
        pl.BlockSpec((ROW_TILE, d), full),
        pl.BlockSpec(mod.shape, full),
        pl.BlockSpec((1, d), full),
        pl.BlockSpec(w_main.shape, full),
        pl.BlockSpec(w_gate.shape, full),
        pl.BlockSpec((1, HEAD_DIM), full),
        pl.BlockSpec((1, HEAD_DIM), full),
        pl.BlockSpec((ROW_TILE, HEAD_DIM), row_all),
        pl.BlockSpec((ROW_TILE, HEAD_DIM), row_all),
    ]
    return pl.pallas_call(
        _inproj_kernel,
        grid=(steps,),
        in_specs=in_specs,
        out_specs=out_specs,
        out_shape=out_shape,
        compiler_params=_cparams(("arbitrary",)),
        name="in_proj",
    )(x, ctx, mod, norm1, w_main, w_gate, q_gain, k_gain, cos_t, sin_t)


def _attn_kernel(bounded_ref, qt_ref, k_ref, vt_ref, cs_ref, wm_a_ref, wm_b_ref, bm_ref, *rest, tk, n_cast):
    cast_in, (o_ref, modb_ref), rest = rest[:n_cast], rest[n_cast:n_cast + 2], rest[n_cast + 2:]
    cast_out, (acc_scr,) = rest[:n_cast], rest[n_cast:]
    uq = ATTN_UNIT_Q

    @pl.when(jnp.logical_and(pl.program_id(0) == 0, pl.program_id(1) == 0))
    def _():
        modb_ref[0:1, :] = bm_ref[...]
        modb_ref[1:2, :] = bm_ref[...]

    n_sub = qt_ref.shape[1] // uq
    n_chunks = k_ref.shape[0] // tk
    streams = [(r, h) for r in range(n_sub) for h in range(Q_PER_KV)]
    zero_row = jnp.zeros((1, uq), F32)

    def q_t(g):
        r, h = streams[g]
        return qt_ref[h * HEAD_DIM:(h + 1) * HEAD_DIM, r * uq:(r + 1) * uq]

    def finish(l):
        for g, (r, h) in enumerate(streams):
            o_ref[h * HEAD_DIM:(h + 1) * HEAD_DIM, r * uq:(r + 1) * uq] = (acc_scr[g] / l[g]).astype(o_ref.dtype)
        for w_ref, wo_ref in zip(cast_in, cast_out):
            wo_ref[...] = w_ref[...].astype(BF16)
        cs = cs_ref[...]
        s = cs * jax.nn.sigmoid(cs)
        half = wm_a_ref.shape[1]
        for n, wm_ref in enumerate((wm_a_ref, wm_b_ref)):
            wm = wm_ref[...]
            modb_ref[0:1, n * half:(n + 1) * half] += jnp.sum(s[:, 0:1] * wm, axis=0, keepdims=True)
            modb_ref[1:2, n * half:(n + 1) * half] += jnp.sum(s[:, 1:2] * wm, axis=0, keepdims=True)

    @pl.when(bounded_ref[0] == 1)
    def _():
        units = [(c, g) for c in range(n_chunks) for g in range(len(streams))]

        def scores(c, g):
            return jnp.dot(k_ref[c * tk:(c + 1) * tk, :], q_t(g), preferred_element_type=F32)

        l = [zero_row for _ in streams]
        pending = [scores(*u) for u in units[:ATTN_LOOKAHEAD]]
        for n, (c, g) in enumerate(units):
            st = pending.pop(0)
            if n + ATTN_LOOKAHEAD < len(units):
                pending.append(scores(*units[n + ATTN_LOOKAHEAD]))
            p = jnp.exp2(st)
            l[g] = l[g] + jnp.sum(p, axis=0, keepdims=True)
            pv = jnp.dot(vt_ref[:, c * tk:(c + 1) * tk], p.astype(BF16), preferred_element_type=F32)
            if c == 0:
                acc_scr[g] = pv
            else:
                acc_scr[g] += pv
        finish(l)

    @pl.when(bounded_ref[0] == 0)
    def _():
        acc_scr[...] = jnp.zeros(acc_scr.shape, F32)

        def chunk(c, carry):
            m, l = carry
            r0 = pl.multiple_of(c * tk, tk)
            kc = k_ref[pl.ds(r0, tk), :]
            vtc = vt_ref[:, pl.ds(r0, tk)]
            m_out, l_out = [], []
            for g in range(len(streams)):
                st = jnp.dot(kc, q_t(g), preferred_element_type=F32)
                m_new = jnp.maximum(m[g], jnp.max(st, axis=0, keepdims=True))
                p = jnp.exp2(st - m_new)
                alpha = jnp.exp2(m[g] - m_new)
                l_out.append(alpha * l[g] + jnp.sum(p, axis=0, keepdims=True))
                acc_scr[g] = alpha * acc_scr[g] + jnp.dot(vtc, p.astype(BF16), preferred_element_type=F32)
                m_out.append(m_new)
            return tuple(m_out), tuple(l_out)

        init = (tuple(jnp.full((1, uq), -jnp.inf, F32) for _ in streams), tuple(zero_row for _ in streams))
        _, l = lax.fori_loop(0, n_chunks, chunk, init)
        finish(l)


def _attention(q_t, k, vt, bounded, cast_weights, cs, w_mod, b_mod, n_done):
    t = q_t.shape[1]
    s_len = k.shape[0]
    tq = ATTN_TQ
    tk = _largest_divisor(s_len, ATTN_TK_CAP, MXU_WIDTH)
    group_w = Q_PER_KV * HEAD_DIM
    n_i = t // tq
    steps = N_KV_HEADS * n_i
    step = lambda h, i: h * n_i + i
    n_rest = w_mod.shape[1] - n_done
    half = n_rest // 2
    assert n_done % half == 0 and w_mod.shape[0] % steps == 0
    mod_rows = w_mod.shape[0] // steps
    mod_specs = [
        pl.BlockSpec((mod_rows, 2), lambda h, i: (step(h, i), 0)),
        pl.BlockSpec((mod_rows, half), lambda h, i: (step(h, i), n_done // half)),
        pl.BlockSpec((mod_rows, half), lambda h, i: (step(h, i), n_done // half + 1)),
        pl.BlockSpec((1, n_rest), lambda h, i: (0, 0)),
    ]
    cast_specs, cast_shapes = [], []
    for w in cast_weights:
        assert w.shape[0] % (steps * BF16_SUBLANES) == 0, w.shape
        cast_specs.append(pl.BlockSpec((w.shape[0] // steps, w.shape[1]), lambda h, i: (step(h, i), 0)))
        cast_shapes.append(jax.ShapeDtypeStruct(w.shape, BF16))
    outs = pl.pallas_call(
        functools.partial(_attn_kernel, tk=tk, n_cast=len(cast_weights)),
        grid=(N_KV_HEADS, n_i),
        in_specs=[
            pl.BlockSpec(memory_space=pltpu.SMEM),
            pl.BlockSpec((group_w, tq), lambda h, i: (h, i)),
            pl.BlockSpec((s_len, HEAD_DIM), lambda h, i: (0, h), pipeline_mode=pl.Buffered(1)),
            pl.BlockSpec((HEAD_DIM, s_len), lambda h, i: (h, 0), pipeline_mode=pl.Buffered(1)),
        ] + mod_specs + cast_specs,
        out_specs=[pl.BlockSpec((group_w, tq), lambda h, i: (h, i)),
                   pl.BlockSpec((2, n_rest), lambda h, i: (0, 0))] + cast_specs,
        out_shape=[jax.ShapeDtypeStruct((ATT_WIDTH, t), BF16),
                   jax.ShapeDtypeStruct((2, n_rest), F32)] + cast_shapes,
        scratch_shapes=[
            pltpu.VMEM((Q_PER_KV * tq // ATTN_UNIT_Q, HEAD_DIM, ATTN_UNIT_Q), F32),
        ],
        compiler_params=_cparams(("arbitrary", "arbitrary")),
        name="attention",
    )(bounded, q_t, k, vt, cs, w_mod, w_mod, b_mod[n_done:].reshape(1, n_rest), *cast_weights)
    return outs[0], outs[1], tuple(outs[2:])


def _log_sigmoid(x):
    return jnp.minimum(x, 0.0) - jnp.log1p(jnp.exp(-jnp.abs(x)))


def _scan(x, axis, op, ident, reverse):
    n = x.shape[axis]
    idx = lax.broadcasted_iota(jnp.int32, x.shape, axis)
    shift = 1
    while shift < n:
        if reverse:
            moved = pltpu.roll(x, n - shift, axis=axis)
            ok = idx < n - shift
        else:
            moved = pltpu.roll(x, shift, axis=axis)
            ok = idx >= shift
        x = op(x, jnp.where(ok, moved, ident))
        shift *= 2
    return x


class _MlstmDirection:
    def __init__(self, reverse, k_ref, qt_ref, vt_ref, gi_ref, gf_ref, git_ref, gft_ref, h_ref,
                 ct_scr, mrow_scr, mcol_scr, visible):
        self.reverse, self.visible = reverse, visible
        self.k_ref, self.qt_ref, self.vt_ref, self.h_ref = k_ref, qt_ref, vt_ref, h_ref
        self.gi_ref, self.gf_ref, self.git_ref, self.gft_ref = gi_ref, gf_ref, git_ref, gft_ref
        self.ct_scr, self.mrow_scr, self.mcol_scr = ct_scr, mrow_scr, mcol_scr
        self.units = [((M_HEADS if reverse else 0) + hd, hd) for hd in range(M_HEADS)]

    def _k(self, hd):
        return self.k_ref[:, hd * M_QK:(hd + 1) * M_QK]

    def _vt(self, hd):
        return self.vt_ref[hd * M_V:(hd + 1) * M_V, :]

    def state_free_matmuls(self):
        self.early = []
        for u, hd in self.units:
            qt = self.qt_ref[hd * M_QK:(hd + 1) * M_QK, :]
            state = self.ct_scr[u]
            qk_t = jnp.dot(self._k(hd), qt, preferred_element_type=F32)
            inter_t = jnp.dot(state.astype(BF16), qt, preferred_element_type=F32)
            self.early.append((state, qk_t, inter_t))

    def gate_terms(self, brow_ref, bcol_ref):
        add = lambda a, b: a + b
        reverse = self.reverse
        i_c = self.gi_ref[...] + brow_ref[0:1, :]
        ls_c = _log_sigmoid(self.gf_ref[...] + brow_ref[1:2, :])
        self.a_c = i_c - _scan(ls_c, 0, add, 0.0, reverse)
        m_prev_r = self.mrow_scr[0:1, 0:N_UNITS]
        mm_r = jnp.maximum(m_prev_r, jnp.max(self.a_c, axis=0, keepdims=True))
        self.kscale_c = jnp.exp(self.a_c - mm_r)
        self.w_prev_r = jnp.exp(m_prev_r - mm_r)
        self.mrow_scr[0:1, 0:N_UNITS] = jnp.sum(ls_c, axis=0, keepdims=True) + mm_r
        i_r = self.git_ref[...] + bcol_ref[:, 0:1]
        ls_r = _log_sigmoid(self.gft_ref[...] + bcol_ref[:, 1:2])
        cum_r = _scan(ls_r, 1, add, 0.0, reverse)
        a_r = i_r - cum_r
        m_prev_c = self.mcol_scr[0:N_UNITS, 0:1]
        m_t = cum_r + jnp.maximum(_scan(a_r, 1, jnp.maximum, -jnp.inf, reverse), m_prev_c)
        self.r_r = cum_r - m_t
        self.w_inter_r = jnp.exp(cum_r + m_prev_c - m_t)
        self.floor_r = jnp.exp(-m_t)
        self.mcol_scr[0:N_UNITS, 0:1] = (jnp.sum(ls_r, axis=1, keepdims=True)
                                         + jnp.maximum(m_prev_c, jnp.max(a_r, axis=1, keepdims=True)))

    def state_update(self):
        for (u, hd), (state, _, _) in zip(self.units, self.early):
            kw = self._k(hd).astype(F32) * self.kscale_c[:, u:u + 1]
            w_prev = self.w_prev_r[0:1, u:u + 1]
            self.ct_scr[u, 0:M_V, :] = (w_prev * state[0:M_V]
                                        + jnp.dot(self._vt(hd), kw.astype(BF16), preferred_element_type=F32))
            self.ct_scr[u, M_V:M_V + 1, :] = w_prev * state[M_V:M_V + 1] + jnp.sum(kw, axis=0, keepdims=True)

    def outputs(self):
        for (u, hd), (_, qk_t, inter_t) in zip(self.units, self.early):
            decay_t = jnp.exp(jnp.where(self.visible, self.a_c[:, u:u + 1], -jnp.inf) + self.r_r[u:u + 1, :])
            s_t = qk_t * decay_t
            w_inter = self.w_inter_r[u:u + 1, :]
            num_t = (inter_t[0:M_V] * w_inter
                     + jnp.dot(self._vt(hd), s_t.astype(BF16), preferred_element_type=F32))
            den = inter_t[M_V:M_V + 1] * w_inter + jnp.sum(s_t, axis=0, keepdims=True)
            self.h_ref[hd * M_V:(hd + 1) * M_V, :] = (num_t * (
                1.0 / jnp.maximum(jnp.abs(den), self.floor_r[u:u + 1, :]))).astype(self.h_ref.dtype)


def _mlstm_kernel(kf_ref, qtf_ref, vtf_ref, gif_ref, gff_ref, gitf_ref, gftf_ref,
                  kb_ref, qtb_ref, vtb_ref, gib_ref, gfb_ref, gitb_ref, gftb_ref,
                  brow_ref, bcol_ref, *rest, n_cast):
    cast_in, (hf_ref, hb_ref), rest = rest[:n_cast], rest[n_cast:n_cast + 2], rest[n_cast + 2:]
    cast_out, (ct_scr, mrow_scr, mcol_scr) = rest[:n_cast], rest[n_cast:]
    L = kf_ref.shape[0]

    @pl.when(pl.program_id(0) == 0)
    def _():
        ct_scr[...] = jnp.zeros(ct_scr.shape, F32)
        mrow_scr[...] = jnp.zeros(mrow_scr.shape, F32)
        mcol_scr[...] = jnp.zeros(mcol_scr.shape, F32)

    src = lax.broadcasted_iota(jnp.int32, (L, L), 0)
    tgt = lax.broadcasted_iota(jnp.int32, (L, L), 1)
    dirs = (
        _MlstmDirection(False, kf_ref, qtf_ref, vtf_ref, gif_ref, gff_ref, gitf_ref, gftf_ref, hf_ref,
                        ct_scr, mrow_scr.at[0], mcol_scr.at[0], src <= tgt),
        _MlstmDirection(True, kb_ref, qtb_ref, vtb_ref, gib_ref, gfb_ref, gitb_ref, gftb_ref, hb_ref,
                        ct_scr, mrow_scr.at[1], mcol_scr.at[1], src >= tgt),
    )
    for d in dirs:
        d.state_free_matmuls()
    for d in dirs:
        d.gate_terms(brow_ref, bcol_ref)
    for d in dirs:
        d.state_update()
    for d in dirs:
        d.outputs()
    for w_ref, o_ref in zip(cast_in, cast_out):
        o_ref[...] = w_ref[...].astype(BF16)


def _mlstm(mk, mq_t, mv_t, gates, b_i, b_f, n_x, cast_weights):
    L = MLSTM_CHUNK
    rows = mk.shape[0]
    assert rows % L == 0 and n_x % L == 0 and rows - n_x == L
    nx = n_x // L
    steps = nx + 1
    g4 = gates.reshape(rows, N_DIR, 2, M_HEADS)
    g_in = g4[:, :, 0, :].reshape(rows, N_UNITS)
    g_fg = g4[:, :, 1, :].reshape(rows, N_UNITS)
    bias_row = jnp.stack([b_i.reshape(N_UNITS), b_f.reshape(N_UNITS)], axis=0)
    f_chunk = lambda g: g
    b_chunk = lambda g: jnp.where(g == 0, 0, nx + 1 - g)
    f_out = lambda g: (0, jnp.maximum(g - 1, 0))
    b_out = lambda g: (0, jnp.where(g == 0, nx - 1, nx - g))
    f_out_rows = lambda g: (jnp.maximum(g - 1, 0), 0)
    small = lambda g: (0, 0)

    def stream(chunk):
        return [
            pl.BlockSpec((L, MQK_WIDTH), lambda g: (chunk(g), 0)),
            pl.BlockSpec((MQK_WIDTH, L), lambda g: (0, chunk(g))),
            pl.BlockSpec((M_WIDTH, L), lambda g: (0, chunk(g))),
            pl.BlockSpec((L, N_UNITS), lambda g: (chunk(g), 0)),
            pl.BlockSpec((L, N_UNITS), lambda g: (chunk(g), 0)),
            pl.BlockSpec((N_UNITS, L), lambda g: (0, chunk(g))),
            pl.BlockSpec((N_UNITS, L), lambda g: (0, chunk(g))),
        ]

    operands = (mk, mq_t, mv_t, g_in, g_fg, g_in.T, g_fg.T)
    cast_specs, cast_shapes = [], []
    for w in cast_weights:
        assert w.shape[0] % (nx * BF16_SUBLANES) == 0, w.shape
        blk = (w.shape[0] // nx, w.shape[1])
        cast_specs.append(pl.BlockSpec(blk, f_out_rows))
        cast_shapes.append(jax.ShapeDtypeStruct(w.shape, BF16))
    outs = pl.pallas_call(
        functools.partial(_mlstm_kernel, n_cast=len(cast_weights)),
        grid=(steps,),
        in_specs=stream(f_chunk) + stream(b_chunk) + [pl.BlockSpec((2, N_UNITS), small),
                                                       pl.BlockSpec((N_UNITS, 2), small)] + cast_specs,
        out_specs=[pl.BlockSpec((M_WIDTH, L), f_out), pl.BlockSpec((M_WIDTH, L), b_out)] + cast_specs,
        out_shape=[jax.ShapeDtypeStruct((M_WIDTH, n_x), BF16)] * 2 + cast_shapes,
        scratch_shapes=[
            pltpu.VMEM((N_UNITS, M_V + BF16_SUBLANES, M_QK), F32),
            pltpu.VMEM((N_DIR, 8, LANES), F32),
            pltpu.VMEM((N_DIR, 8, LANES), F32),
        ],
        compiler_params=_cparams(("arbitrary",)),
        name="mlstm",
    )(*operands, *operands, bias_row, bias_row.T, *cast_weights)
    return outs[0], outs[1], tuple(outs[2:])


def _outproj_kernel(att_ref, hft_ref, hbt_ref, mo_ref, mg_ref, w_ref, x_ref, mod_ref, n2_ref,
                    x1_ref, h2_ref):
    d = x_ref.shape[1]
    tm = x_ref.shape[0]
    rows = tm // OUT_ROW_SPLIT

    def project(r0):
        y = lax.dot_general(att_ref[:, r0:r0 + rows], w_ref[0:ATT_WIDTH, :], (((0,), (0,)), ((), ())),
                            preferred_element_type=F32)
        for hd in range(M_HEADS):
            c = hd * M_V
            ht = (hft_ref[c:c + M_V, r0:r0 + rows].astype(F32)
                  + hbt_ref[c:c + M_V, r0:r0 + rows].astype(F32))
            ht = ht * lax.rsqrt(jnp.mean(ht * ht, axis=0, keepdims=True) + NORM_EPS) * mg_ref[c:c + M_V, :]
            r = (ht.T * mo_ref[r0:r0 + rows, c:c + M_V].astype(F32)).astype(BF16)
            y = y + jnp.dot(r, w_ref[ATT_WIDTH + c:ATT_WIDTH + c + M_V, :], preferred_element_type=F32)
        return y

    ys = [project(b * rows) for b in range(OUT_ROW_SPLIT)]
    for b, y in enumerate(ys):
        r0 = b * rows
        x1 = x_ref[r0:r0 + rows, :] + mod_ref[0:1, 0:d] * y
        x1_ref[r0:r0 + rows, :] = x1
        h2 = _rms(x1) * n2_ref[...] * (1.0 + mod_ref[0:1, 2 * d:3 * d]) + mod_ref[0:1, d:2 * d]
        h2_ref[r0:r0 + rows, :] = h2.astype(BF16)


def _out_proj(att, hf_t, hb_t, mo, m_gain, w_out, x, mod, norm2):
    t, d = x.shape
    tm = OUT_TM
    row = lambda i: (i, 0)
    full = lambda i: (0, 0)
    return pl.pallas_call(
        _outproj_kernel,
        grid=(t // tm,),
        in_specs=[
            pl.BlockSpec((ATT_WIDTH, tm), lambda i: (0, i)),
            pl.BlockSpec((M_WIDTH, tm), lambda i: (0, i)),
            pl.BlockSpec((M_WIDTH, tm), lambda i: (0, i)),
            pl.BlockSpec((tm, M_WIDTH), row),
            pl.BlockSpec((M_WIDTH, 1), full),
            pl.BlockSpec(w_out.shape, full),
            pl.BlockSpec((tm, d), row),
            pl.BlockSpec(mod.shape, full),
            pl.BlockSpec((1, d), full),
        ],
        out_specs=[pl.BlockSpec((tm, d), row), pl.BlockSpec((tm, d), row)],
        out_shape=[jax.ShapeDtypeStruct((t, d), F32), jax.ShapeDtypeStruct((t, d), BF16)],
        compiler_params=_cparams(("arbitrary",)),
        name="out_proj",
    )(att, hf_t, hb_t, mo, m_gain, w_out, x, mod, norm2)


def _ffn_kernel(h_ref, hp_ref, hn_ref, wg_ref, wv_ref, cwg_ref, cwv_ref, cbg_ref, cbv_ref, wd_ref,
                x1_ref, mod_ref, nf_ref, o_ref, hext_ref, ug_ref, uv_ref):
    i = pl.program_id(0)
    j = pl.program_id(1)
    tm = h_ref.shape[0]
    d = x1_ref.shape[1]

    @pl.when(j == 0)
    def _():
        hext_ref[0:HALO, :] = jnp.where(i == 0, jnp.zeros_like(hp_ref[...]), hp_ref[...])
        hext_ref[HALO:HALO + tm, :] = h_ref[...]
        hext_ref[HALO + tm:, :] = jnp.where(i == pl.num_programs(0) - 1, jnp.zeros_like(hn_ref[...]), hn_ref[...])
        o_ref[...] = jnp.zeros(o_ref.shape, F32)

    split = tm // 2 + HALO
    cut = tm // 2 - HALO

    def conv(u_ref, cw_ref, cb_ref, r0, n):
        return (cw_ref[0:1, :] * u_ref[HALO - 1 + r0:HALO - 1 + r0 + n, :]
                + cw_ref[1:2, :] * u_ref[HALO + r0:HALO + r0 + n, :]
                + cw_ref[2:3, :] * u_ref[HALO + 1 + r0:HALO + 1 + r0 + n, :]
                + cb_ref[...])

    def hidden_block(final):
        for r0, r1 in ((0, split), (split, tm + 2 * HALO)):
            ug_ref[r0:r1, :] = jnp.dot(hext_ref[r0:r1, :], wg_ref[...], preferred_element_type=F32)
            uv_ref[r0:r1, :] = jnp.dot(hext_ref[r0:r1, :], wv_ref[...], preferred_element_type=F32)
        for r0, n in ((0, cut), (cut, tm - cut)):
            g = conv(ug_ref, cwg_ref, cbg_ref, r0, n)
            val = conv(uv_ref, cwv_ref, cbv_ref, r0, n)
            hg = 0.5 * g
            a = ((hg + hg * jnp.tanh(hg)) * val).astype(BF16)
            acc = o_ref[r0:r0 + n, :] + jnp.dot(a, wd_ref[...], preferred_element_type=F32)
            if final:
                y = x1_ref[r0:r0 + n, :] + mod_ref[0:1, 3 * d:4 * d] * acc
                acc = _rms(y) * nf_ref[...]
            o_ref[r0:r0 + n, :] = acc

    last = pl.num_programs(1) - 1
    pl.when(j != last)(functools.partial(hidden_block, False))
    pl.when(j == last)(functools.partial(hidden_block, True))


def _conv_ffn(h2, w_up, conv_w, conv_b, w_down, x1, mod, norm_f):
    t, d = h2.shape
    d_ff = w_down.shape[0]
    tm, tf = FFN_TM, FFN_TF
    nf = d_ff // tf
    hb = tm // HALO
    last_halo = t // HALO - 1
    in_specs = [
        pl.BlockSpec((tm, d), lambda i, j: (i, 0)),
        pl.BlockSpec((HALO, d), lambda i, j: (jnp.maximum(i * hb - 1, 0), 0)),
        pl.BlockSpec((HALO, d), lambda i, j: (jnp.minimum((i + 1) * hb, last_halo), 0)),
        pl.BlockSpec((d, tf), lambda i, j: (0, j)),
        pl.BlockSpec((d, tf), lambda i, j: (0, j + nf)),
        pl.BlockSpec((3, tf), lambda i, j: (0, j)),
        pl.BlockSpec((3, tf), lambda i, j: (0, j + nf)),
        pl.BlockSpec((1, tf), lambda i, j: (0, j)),
        pl.BlockSpec((1, tf), lambda i, j: (0, j + nf)),
        pl.BlockSpec((tf, d), lambda i, j: (j, 0)),
        pl.BlockSpec((tm, d), lambda i, j: (i, 0)),
        pl.BlockSpec(mod.shape, lambda i, j: (0, 0)),
        pl.BlockSpec((1, d), lambda i, j: (0, 0)),
    ]
    return pl.pallas_call(
        _ffn_kernel,
        grid=(t // tm, nf),
        in_specs=in_specs,
        out_specs=pl.BlockSpec((tm, d), lambda i, j: (i, 0)),
        out_shape=jax.ShapeDtypeStruct((t, d), F32),
        scratch_shapes=[
            pltpu.VMEM((tm + 2 * HALO, d), BF16),
            pltpu.VMEM((tm + 2 * HALO, tf), F32),
            pltpu.VMEM((tm + 2 * HALO, tf), F32),
        ],
        compiler_params=_cparams(("arbitrary", "arbitrary")),
        name="conv_ffn",
    )(h2, h2, h2, w_up, w_up, conv_w, conv_w, conv_b, conv_b, w_down, x1, mod, norm_f)


def _rope_tables(n_ctx, n_tok):
    f32 = np.float32
    rows = n_tok // GRID_W
    row = np.repeat(np.arange(rows, dtype=f32), GRID_W)
    col = np.tile(np.arange(GRID_W, dtype=f32), rows)
    inv_freq = np.power(f32(ROPE_THETA), -np.arange(0, AXIS_DIM, 2, dtype=f32) / f32(AXIS_DIM)).astype(f32)
    ang_r = row[:, None] * inv_freq[None, :]
    ang_c = col[:, None] * inv_freq[None, :]
    cos = np.concatenate([np.cos(ang_r), np.cos(ang_r), np.cos(ang_c), np.cos(ang_c)], axis=1)
    sin = np.concatenate([-np.sin(ang_r), np.sin(ang_r), -np.sin(ang_c), np.sin(ang_c)], axis=1)
    cos = np.concatenate([np.ones((n_ctx, HEAD_DIM), f32), cos], axis=0).astype(f32)
    sin = np.concatenate([np.zeros((n_ctx, HEAD_DIM), f32), sin], axis=0).astype(f32)
    return jnp.asarray(cos), jnp.asarray(sin)


def kernel(x, c, ctx, c_ctx, w_mod, b_mod, norm1, w_in, q_norm, k_norm, b_igate, b_fgate, m_norm,
           w_out, norm2, w_up, conv_w, conv_b, w_down, norm_f):
    batch, n_tok, d = x.shape
    assert batch == 1 and w_mod.shape[0] == 1
    n_ctx = ctx.shape[1]
    x2 = x[0]
    ctx2 = ctx[0]

    cs = jnp.stack([c[0], c_ctx], axis=1)
    mod = _modulation(cs, w_mod[0], b_mod[0], 2 * d)

    w_in_t = w_in[0].T
    w_main = _cast_main_columns(w_in_t, OFF_G)
    w_gate = _gate_weights(w_in_t, OFF_G)
    cos_t, sin_t = _rope_tables(n_ctx, n_tok)
    q_gain = (q_norm[0] * (HEAD_DIM ** -0.5 * LOG2E)).reshape(1, HEAD_DIM)
    k_gain = k_norm[0].reshape(1, HEAD_DIM)
    q_t, k, vt, mq_t, mk, mv_t, mo, gates = _in_proj(
        x2, ctx2, mod, norm1[0].reshape(1, d), w_main, w_gate, q_gain, k_gain, cos_t, sin_t)

    score_bound = HEAD_DIM * jnp.max(jnp.abs(q_gain)) * jnp.max(jnp.abs(k_gain)) * ATTN_BOUND_HEADROOM
    bounded = (score_bound <= ATTN_SCORE_BOUND).astype(jnp.int32).reshape(1)
    att, mod_rest, (w_out_b, w_up_b, w_down_b) = _attention(
        q_t, k, vt, bounded, (w_out[0], w_up[0], w_down[0]), cs, w_mod[0], b_mod[0], 2 * d)

    hf_t, hb_t, _ = _mlstm(mk, mq_t, mv_t, gates, b_igate[0], b_fgate[0], n_tok, ())

    x1, h2 = _out_proj(att, hf_t, hb_t, mo, m_norm[0].reshape(M_WIDTH, 1), w_out_b, x2, mod_rest,
                       norm2[0].reshape(1, d))

    out = _conv_ffn(h2, w_up_b, conv_w[0], conv_b[0].reshape(1, -1), w_down_b, x1, mod_rest, norm_f.reshape(1, d))
    return out[None]
```

```python
import functools

import jax
import jax.numpy as jnp
import numpy as np
from jax import lax
from jax.experimental import pallas as pl
from jax.experimental.pallas import tpu as pltpu

F32 = jnp.float32
BF16 = jnp.bfloat16

GRID_W = 64
HEAD_DIM = 128
N_Q_HEADS = 8
N_KV_HEADS = 2
Q_PER_KV = N_Q_HEADS // N_KV_HEADS
AXIS_DIM = HEAD_DIM // 2
ROPE_THETA = 10000.0
M_HEADS = 4
M_V = 256
M_QK = 128
N_DIR = 2
NORM_EPS = 1e-6
ATT_WIDTH = N_Q_HEADS * HEAD_DIM
KV_WIDTH = N_KV_HEADS * HEAD_DIM
M_WIDTH = M_HEADS * M_V
MQK_WIDTH = M_HEADS * M_QK
N_GATES = N_DIR * 2 * M_HEADS
N_UNITS = N_DIR * M_HEADS

OFF_AQ = 0
OFF_AK = OFF_AQ + ATT_WIDTH
OFF_AV = OFF_AK + KV_WIDTH
OFF_MQ = OFF_AV + KV_WIDTH
OFF_MK = OFF_MQ + MQK_WIDTH
OFF_MV = OFF_MK + MQK_WIDTH
OFF_MO = OFF_MV + M_WIDTH
OFF_G = OFF_MO + M_WIDTH

LANES = 128
MXU_WIDTH = 256
BF16_SUBLANES = 16
VMEM_LIMIT = 50 * 1024 * 1024

ROW_TILE = 256
MLSTM_CHUNK = 256
ATTN_TQ = 512
ATTN_SCORE_BOUND = 64.0
ATTN_BOUND_HEADROOM = 1.02
ATTN_UNIT_Q = 256
ATTN_TK_CAP = 768
ATTN_LOOKAHEAD = 2
OUT_TM = 512
OUT_ROW_SPLIT = 2
FFN_TM = 512
FFN_TF = 512
HALO = BF16_SUBLANES
MOD_TK = 256
WCAST_TN = 512
LOG2E = 1.4426950408889634


def _largest_divisor(n, cap, mult):
    best = None
    for d in range(mult, cap + 1, mult):
        if n % d == 0:
            best = d
    assert best is not None, (n, cap, mult)
    return best


def _cparams(sem, vmem=VMEM_LIMIT):
    return pltpu.CompilerParams(dimension_semantics=sem, vmem_limit_bytes=vmem)


def _rms(x, eps=NORM_EPS):
    return x * lax.rsqrt(jnp.mean(x * x, axis=-1, keepdims=True) + eps)


def _mod_kernel(cs_ref, w_ref, b_ref, o_ref):
    @pl.when(pl.program_id(0) == 0)
    def _():
        o_ref[0:1, :] = b_ref[...]
        o_ref[1:2, :] = b_ref[...]

    cs = cs_ref[...]
    s = cs * jax.nn.sigmoid(cs)
    w = w_ref[...]
    o_ref[0:1, :] += jnp.sum(s[:, 0:1] * w, axis=0, keepdims=True)
    o_ref[1:2, :] += jnp.sum(s[:, 1:2] * w, axis=0, keepdims=True)


def _modulation(cs, w_mod, b_mod, n):
    d = w_mod.shape[0]
    tk = MOD_TK
    return pl.pallas_call(
        _mod_kernel,
        grid=(d // tk,),
        in_specs=[
            pl.BlockSpec((tk, 2), lambda j: (j, 0)),
            pl.BlockSpec((tk, n), lambda j: (j, 0)),
            pl.BlockSpec((1, n), lambda j: (0, 0)),
        ],
        out_specs=pl.BlockSpec((2, n), lambda j: (0, 0)),
        out_shape=jax.ShapeDtypeStruct((2, n), F32),
        compiler_params=_cparams(("arbitrary",)),
        name="modulation",
    )(cs, w_mod, b_mod.reshape(1, -1))


def _gate_weight_kernel(wt_ref, o_ref):
    g = wt_ref[...]
    hi = g.astype(BF16)
    lo = (g - hi.astype(F32)).astype(BF16)
    pad = jnp.zeros((o_ref.shape[0] - 2 * N_GATES, g.shape[1]), BF16)
    o_ref[...] = jnp.concatenate([hi, lo, pad], axis=0)


def _gate_weights(w_t, row0):
    n, d = w_t.shape
    assert row0 % N_GATES == 0 and row0 + N_GATES == n
    return pl.pallas_call(
        _gate_weight_kernel,
        grid=(1,),
        in_specs=[pl.BlockSpec((N_GATES, d), lambda i: (row0 // N_GATES, 0))],
        out_specs=pl.BlockSpec((LANES, d), lambda i: (0, 0)),
        out_shape=jax.ShapeDtypeStruct((LANES, d), BF16),
        compiler_params=_cparams(("arbitrary",)),
        name="w_gate_split",
    )(w_t)


def _wcast_kernel(wt_ref, o_ref):
    o_ref[...] = wt_ref[...].T.astype(BF16)


def _cast_main_columns(w_t, n_main):
    n, d = w_t.shape
    tn = WCAST_TN
    assert n_main % tn == 0 and n_main <= n
    return pl.pallas_call(
        _wcast_kernel,
        grid=(n_main // tn,),
        in_specs=[pl.BlockSpec((tn, d), lambda j: (j, 0))],
        out_specs=pl.BlockSpec((d, tn), lambda j: (0, j)),
        out_shape=jax.ShapeDtypeStruct((d, n_main), BF16),
        compiler_params=_cparams(("arbitrary",)),
        name="w_in_cast",
    )(w_t)


def _swap_rope_halves(y):
    lane = lax.broadcasted_iota(jnp.int32, y.shape, 1)
    fwd = pltpu.roll(y, LANES - AXIS_DIM // 2, axis=1)
    bwd = pltpu.roll(y, AXIS_DIM // 2, axis=1)
    return jnp.where((lane & (AXIS_DIM // 2)) == 0, fwd, bwd)


def _inproj_kernel(x_ref, ctx_ref, mod_ref, n1_ref, w_ref, wg_ref, qg_ref, kg_ref, cos_ref, sin_ref,
                   qt_ref, k_ref, vt_ref, mqt_ref, mk_ref, mvt_ref, mo_ref, g_ref):
    d = x_ref.shape[1]
    is_ctx = pl.program_id(0) == 0
    xin = jnp.where(is_ctx, ctx_ref[...], x_ref[...])
    shift = jnp.where(is_ctx, mod_ref[1:2, 0:d], mod_ref[0:1, 0:d])
    scale = jnp.where(is_ctx, mod_ref[1:2, d:2 * d], mod_ref[0:1, d:2 * d])
    hf = _rms(xin) * n1_ref[...] * (1.0 + scale) + shift
    h = hf.astype(BF16)

    def proj(c0, width):
        return jnp.dot(h, w_ref[:, c0:c0 + width], preferred_element_type=F32)

    cos = cos_ref[...]
    sin = sin_ref[...]

    def norm_rope(y, gain):
        yn = _rms(y) * gain
        return yn * cos + _swap_rope_halves(yn) * sin

    for j in range(ATT_WIDTH // MXU_WIDTH):
        y = proj(OFF_AQ + j * MXU_WIDTH, MXU_WIDTH)
        for t in range(MXU_WIDTH // HEAD_DIM):
            c = j * MXU_WIDTH + t * HEAD_DIM
            qt_ref[c:c + HEAD_DIM, :] = norm_rope(y[:, t * HEAD_DIM:(t + 1) * HEAD_DIM], qg_ref[...]).T.astype(BF16)
    y = proj(OFF_AK, KV_WIDTH)
    for t in range(N_KV_HEADS):
        k_ref[:, t * HEAD_DIM:(t + 1) * HEAD_DIM] = norm_rope(
            y[:, t * HEAD_DIM:(t + 1) * HEAD_DIM], kg_ref[...]).astype(BF16)
    vt_ref[...] = proj(OFF_AV, KV_WIDTH).T.astype(BF16)
    for j in range(MQK_WIDTH // MXU_WIDTH):
        c = j * MXU_WIDTH
        mqt_ref[c:c + MXU_WIDTH, :] = proj(OFF_MQ + c, MXU_WIDTH).T.astype(BF16)
        mk_ref[:, c:c + MXU_WIDTH] = (proj(OFF_MK + c, MXU_WIDTH) * (M_QK ** -0.5)).astype(BF16)
    for j in range(M_WIDTH // MXU_WIDTH):
        c = j * MXU_WIDTH
        mvt_ref[c:c + MXU_WIDTH, :] = proj(OFF_MV + c, MXU_WIDTH).T.astype(BF16)
        mo_ref[:, c:c + MXU_WIDTH] = jax.nn.sigmoid(proj(OFF_MO + c, MXU_WIDTH)).astype(BF16)
    nt = (((1,), (1,)), ((), ()))
    gg = lax.dot_general(h, wg_ref[...], nt, preferred_element_type=F32)
    gg = gg + pltpu.roll(gg, LANES - N_GATES, axis=1)
    g_ref[...] = gg[:, 0:N_GATES]


def _in_proj(x, ctx, mod, norm1, w_main, w_gate, q_gain, k_gain, cos_t, sin_t):
    t, d = x.shape
    n_ctx = ctx.shape[0]
    assert n_ctx == ROW_TILE and t % ROW_TILE == 0
    rows = n_ctx + t
    steps = rows // ROW_TILE
    full = lambda i: (0, 0)
    row_all = lambda i: (i, 0)
    row_x = lambda i: (jnp.maximum(i - 1, 0), 0)
    out_shape = [
        jax.ShapeDtypeStruct((ATT_WIDTH, t), BF16),
        jax.ShapeDtypeStruct((rows, KV_WIDTH), BF16),
        jax.ShapeDtypeStruct((KV_WIDTH, rows), BF16),
        jax.ShapeDtypeStruct((MQK_WIDTH, rows), BF16),
        jax.ShapeDtypeStruct((rows, MQK_WIDTH), BF16),
        jax.ShapeDtypeStruct((M_WIDTH, rows), BF16),
        jax.ShapeDtypeStruct((t, M_WIDTH), BF16),
        jax.ShapeDtypeStruct((rows, N_GATES), F32),
    ]
    out_specs = [
        pl.BlockSpec((ATT_WIDTH, ROW_TILE), lambda i: (0, jnp.maximum(i - 1, 0))),
        pl.BlockSpec((ROW_TILE, KV_WIDTH), row_all),
        pl.BlockSpec((KV_WIDTH, ROW_TILE), lambda i: (0, i)),
        pl.BlockSpec((MQK_WIDTH, ROW_TILE), lambda i: (0, i)),
        pl.BlockSpec((ROW_TILE, MQK_WIDTH), row_all),
        pl.BlockSpec((M_WIDTH, ROW_TILE), lambda i: (0, i)),
        pl.BlockSpec((ROW_TILE, M_WIDTH), row_x),
        pl.BlockSpec((ROW_TILE, N_GATES), row_all),
    ]
    in_specs = [
        pl.BlockSpec((ROW_TILE, d), row_x),
        pl.BlockSpec((ROW_TILE, d), full),
        pl.BlockSpec(mod.shape, full),
        pl.BlockSpec((1, d), full),
        pl.BlockSpec(w_main.shape, full),
        pl.BlockSpec(w_gate.shape, full),
        pl.BlockSpec((1, HEAD_DIM), full),
        pl.BlockSpec((1, HEAD_DIM), full),
        pl.BlockSpec((ROW_TILE, HEAD_DIM), row_all),
        pl.BlockSpec((ROW_TILE, HEAD_DIM), row_all),
    ]
    return pl.pallas_call(
        _inproj_kernel,
        grid=(steps,),
        in_specs=in_specs,
        out_specs=out_specs,
        out_shape=out_shape,
        compiler_params=_cparams(("arbitrary",)),
        name="in_proj",
    )(x, ctx, mod, norm1, w_main, w_gate, q_gain, k_gain, cos_t, sin_t)


def _attn_kernel(bounded_ref, qt_ref, k_ref, vt_ref, cs_ref, wm_a_ref, wm_b_ref, bm_ref, *rest, tk, n_cast):
    cast_in, (o_ref, modb_ref), rest = rest[:n_cast], rest[n_cast:n_cast + 2], rest[n_cast + 2:]
    cast_out, (acc_scr,) = rest[:n_cast], rest[n_cast:]
    uq = ATTN_UNIT_Q

    @pl.when(jnp.logical_and(pl.program_id(0) == 0, pl.program_id(1) == 0))
    def _():
        modb_ref[0:1, :] = bm_ref[...]
        modb_ref[1:2, :] = bm_ref[...]

    n_sub = qt_ref.shape[1] // uq
    n_chunks = k_ref.shape[0] // tk
    streams = [(r, h) for r in range(n_sub) for h in range(Q_PER_KV)]
    zero_row = jnp.zeros((1, uq), F32)

    def q_t(g):
        r, h = streams[g]
        return qt_ref[h * HEAD_DIM:(h + 1) * HEAD_DIM, r * uq:(r + 1) * uq]

    def finish(l):
        for g, (r, h) in enumerate(streams):
            o_ref[h * HEAD_DIM:(h + 1) * HEAD_DIM, r * uq:(r + 1) * uq] = (acc_scr[g] / l[g]).astype(o_ref.dtype)
        for w_ref, wo_ref in zip(cast_in, cast_out):
            wo_ref[...] = w_ref[...].astype(BF16)
        cs = cs_ref[...]
        s = cs * jax.nn.sigmoid(cs)
        half = wm_a_ref.shape[1]
        for n, wm_ref in enumerate((wm_a_ref, wm_b_ref)):
            wm = wm_ref[...]
            modb_ref[0:1, n * half:(n + 1) * half] += jnp.sum(s[:, 0:1] * wm, axis=0, keepdims=True)
            modb_ref[1:2, n * half:(n + 1) * half] += jnp.sum(s[:, 1:2] * wm, axis=0, keepdims=True)

    @pl.when(bounded_ref[0] == 1)
    def _():
        units = [(c, g) for c in range(n_chunks) for g in range(len(streams))]

        def scores(c, g):
            return jnp.dot(k_ref[c * tk:(c + 1) * tk, :], q_t(g), preferred_element_type=F32)

        l = [zero_row for _ in streams]
        pending = [scores(*u) for u in units[:ATTN_LOOKAHEAD]]
        for n, (c, g) in enumerate(units):
            st = pending.pop(0)
            if n + ATTN_LOOKAHEAD < len(units):
                pending.append(scores(*units[n + ATTN_LOOKAHEAD]))
            p = jnp.exp2(st)
            l[g] = l[g] + jnp.sum(p, axis=0, keepdims=True)
            pv = jnp.dot(vt_ref[:, c * tk:(c + 1) * tk], p.astype(BF16), preferred_element_type=F32)
            if c == 0:
                acc_scr[g] = pv
            else:
                acc_scr[g] += pv
        finish(l)

    @pl.when(bounded_ref[0] == 0)
    def _():
        acc_scr[...] = jnp.zeros(acc_scr.shape, F32)

        def chunk(c, carry):
            m, l = carry
            r0 = pl.multiple_of(c * tk, tk)
            kc = k_ref[pl.ds(r0, tk), :]
            vtc = vt_ref[:, pl.ds(r0, tk)]
            m_out, l_out = [], []
            for g in range(len(streams)):
                st = jnp.dot(kc, q_t(g), preferred_element_type=F32)
                m_new = jnp.maximum(m[g], jnp.max(st, axis=0, keepdims=True))
                p = jnp.exp2(st - m_new)
                alpha = jnp.exp2(m[g] - m_new)
                l_out.append(alpha * l[g] + jnp.sum(p, axis=0, keepdims=True))
                acc_scr[g] = alpha * acc_scr[g] + jnp.dot(vtc, p.astype(BF16), preferred_element_type=F32)
                m_out.append(m_new)
            return tuple(m_out), tuple(l_out)

        init = (tuple(jnp.full((1, uq), -jnp.inf, F32) for _ in streams), tuple(zero_row for _ in streams))
        _, l = lax.fori_loop(0, n_chunks, chunk, init)
        finish(l)


def _attention(q_t, k, vt, bounded, cast_weights, cs, w_mod, b_mod, n_done):
    t = q_t.shape[1]
    s_len = k.shape[0]
    tq = ATTN_TQ
    tk = _largest_divisor(s_len, ATTN_TK_CAP, MXU_WIDTH)
    group_w = Q_PER_KV * HEAD_DIM
    n_i = t // tq
    steps = N_KV_HEADS * n_i
    step = lambda h, i: h * n_i + i
    n_rest = w_mod.shape[1] - n_done
    half = n_rest // 2
    assert n_done % half == 0 and w_mod.shape[0] % steps == 0
    mod_rows = w_mod.shape[0] // steps
    mod_specs = [
        pl.BlockSpec((mod_rows, 2), lambda h, i: (step(h, i), 0)),
        pl.BlockSpec((mod_rows, half), lambda h, i: (step(h, i), n_done // half)),
        pl.BlockSpec((mod_rows, half), lambda h, i: (step(h, i), n_done // half + 1)),
        pl.BlockSpec((1, n_rest), lambda h, i: (0, 0)),
    ]
    cast_specs, cast_shapes = [], []
    for w in cast_weights:
        assert w.shape[0] % (steps * BF16_SUBLANES) == 0, w.shape
        cast_specs.append(pl.BlockSpec((w.shape[0] // steps, w.shape[1]), lambda h, i: (step(h, i), 0)))
        cast_shapes.append(jax.ShapeDtypeStruct(w.shape, BF16))
    outs = pl.pallas_call(
        functools.partial(_attn_kernel, tk=tk, n_cast=len(cast_weights)),
        grid=(N_KV_HEADS, n_i),
        in_specs=[
            pl.BlockSpec(memory_space=pltpu.SMEM),
            pl.BlockSpec((group_w, tq), lambda h, i: (h, i)),
            pl.BlockSpec((s_len, HEAD_DIM), lambda h, i: (0, h), pipeline_mode=pl.Buffered(1)),
            pl.BlockSpec((HEAD_DIM, s_len), lambda h, i: (h, 0), pipeline_mode=pl.Buffered(1)),
        ] + mod_specs + cast_specs,
        out_specs=[pl.BlockSpec((group_w, tq), lambda h, i: (h, i)),
                   pl.BlockSpec((2, n_rest), lambda h, i: (0, 0))] + cast_specs,
        out_shape=[jax.ShapeDtypeStruct((ATT_WIDTH, t), BF16),
                   jax.ShapeDtypeStruct((2, n_rest), F32)] + cast_shapes,
        scratch_shapes=[
            pltpu.VMEM((Q_PER_KV * tq // ATTN_UNIT_Q, HEAD_DIM, ATTN_UNIT_Q), F32),
        ],
        compiler_params=_cparams(("arbitrary", "arbitrary")),
        name="attention",
    )(bounded, q_t, k, vt, cs, w_mod, w_mod, b_mod[n_done:].reshape(1, n_rest), *cast_weights)
    return outs[0], outs[1], tuple(outs[2:])


def _log_sigmoid(x):
    return jnp.minimum(x, 0.0) - jnp.log1p(jnp.exp(-jnp.abs(x)))


def _scan(x, axis, op, ident, reverse):
    n = x.shape[axis]
    idx = lax.broadcasted_iota(jnp.int32, x.shape, axis)
    shift = 1
    while shift < n:
        if reverse:
            moved = pltpu.roll(x, n - shift, axis=axis)
            ok = idx < n - shift
        else:
            moved = pltpu.roll(x, shift, axis=axis)
            ok = idx >= shift
        x = op(x, jnp.where(ok, moved, ident))
        shift *= 2
    return x


class _MlstmDirection:
    def __init__(self, reverse, k_ref, qt_ref, vt_ref, gi_ref, gf_ref, git_ref, gft_ref, h_ref,
                 ct_scr, mrow_scr, mcol_scr, visible):
        self.reverse, self.visible = reverse, visible
        self.k_ref, self.qt_ref, self.vt_ref, self.h_ref = k_ref, qt_ref, vt_ref, h_ref
        self.gi_ref, self.gf_ref, self.git_ref, self.gft_ref = gi_ref, gf_ref, git_ref, gft_ref
        self.ct_scr, self.mrow_scr, self.mcol_scr = ct_scr, mrow_scr, mcol_scr
        self.units = [((M_HEADS if reverse else 0) + hd, hd) for hd in range(M_HEADS)]

    def _k(self, hd):
        return self.k_ref[:, hd * M_QK:(hd + 1) * M_QK]

    def _vt(self, hd):
        return self.vt_ref[hd * M_V:(hd + 1) * M_V, :]

    def state_free_matmuls(self):
        self.early = []
        for u, hd in self.units:
            qt = self.qt_ref[hd * M_QK:(hd + 1) * M_QK, :]
            state = self.ct_scr[u]
            qk_t = jnp.dot(self._k(hd), qt, preferred_element_type=F32)
            inter_t = jnp.dot(state.astype(BF16), qt, preferred_element_type=F32)
            self.early.append((state, qk_t, inter_t))

    def gate_terms(self, brow_ref, bcol_ref):
        add = lambda a, b: a + b
        reverse = self.reverse
        i_c = self.gi_ref[...] + brow_ref[0:1, :]
        ls_c = _log_sigmoid(self.gf_ref[...] + brow_ref[1:2, :])
        self.a_c = i_c - _scan(ls_c, 0, add, 0.0, reverse)
        m_prev_r = self.mrow_scr[0:1, 0:N_UNITS]
        mm_r = jnp.maximum(m_prev_r, jnp.max(self.a_c, axis=0, keepdims=True))
        self.kscale_c = jnp.exp(self.a_c - mm_r)
        self.w_prev_r = jnp.exp(m_prev_r - mm_r)
        self.mrow_scr[0:1, 0:N_UNITS] = jnp.sum(ls_c, axis=0, keepdims=True) + mm_r
        i_r = self.git_ref[...] + bcol_ref[:, 0:1]
        ls_r = _log_sigmoid(self.gft_ref[...] + bcol_ref[:, 1:2])
        cum_r = _scan(ls_r, 1, add, 0.0, reverse)
        a_r = i_r - cum_r
        m_prev_c = self.mcol_scr[0:N_UNITS, 0:1]
        m_t = cum_r + jnp.maximum(_scan(a_r, 1, jnp.maximum, -jnp.inf, reverse), m_prev_c)
        self.r_r = cum_r - m_t
        self.w_inter_r = jnp.exp(cum_r + m_prev_c - m_t)
        self.floor_r = jnp.exp(-m_t)
        self.mcol_scr[0:N_UNITS, 0:1] = (jnp.sum(ls_r, axis=1, keepdims=True)
                                         + jnp.maximum(m_prev_c, jnp.max(a_r, axis=1, keepdims=True)))

    def state_update(self):
        for (u, hd), (state, _, _) in zip(self.units, self.early):
            kw = self._k(hd).astype(F32) * self.kscale_c[:, u:u + 1]
            w_prev = self.w_prev_r[0:1, u:u + 1]
            self.ct_scr[u, 0:M_V, :] = (w_prev * state[0:M_V]
                                        + jnp.dot(self._vt(hd), kw.astype(BF16), preferred_element_type=F32))
            self.ct_scr[u, M_V:M_V + 1, :] = w_prev * state[M_V:M_V + 1] + jnp.sum(kw, axis=0, keepdims=True)

    def outputs(self):
        for (u, hd), (_, qk_t, inter_t) in zip(self.units, self.early):
            decay_t = jnp.exp(jnp.where(self.visible, self.a_c[:, u:u + 1], -jnp.inf) + self.r_r[u:u + 1, :])
            s_t = qk_t * decay_t
            w_inter = self.w_inter_r[u:u + 1, :]
            num_t = (inter_t[0:M_V] * w_inter
                     + jnp.dot(self._vt(hd), s_t.astype(BF16), preferred_element_type=F32))
            den = inter_t[M_V:M_V + 1] * w_inter + jnp.sum(s_t, axis=0, keepdims=True)
            self.h_ref[hd * M_V:(hd + 1) * M_V, :] = (num_t * (
                1.0 / jnp.maximum(jnp.abs(den), self.floor_r[u:u + 1, :]))).astype(self.h_ref.dtype)


def _mlstm_kernel(kf_ref, qtf_ref, vtf_ref, gif_ref, gff_ref, gitf_ref, gftf_ref,
                  kb_ref, qtb_ref, vtb_ref, gib_ref, gfb_ref, gitb_ref, gftb_ref,
                  brow_ref, bcol_ref, *rest, n_cast):
    cast_in, (hf_ref, hb_ref), rest = rest[:n_cast], rest[n_cast:n_cast + 2], rest[n_cast + 2:]
    cast_out, (ct_scr, mrow_scr, mcol_scr) = rest[:n_cast], rest[n_cast:]
    L = kf_ref.shape[0]

    @pl.when(pl.program_id(0) == 0)
    def _():
        ct_scr[...] = jnp.zeros(ct_scr.shape, F32)
        mrow_scr[...] = jnp.zeros(mrow_scr.shape, F32)
        mcol_scr[...] = jnp.zeros(mcol_scr.shape, F32)

    src = lax.broadcasted_iota(jnp.int32, (L, L), 0)
    tgt = lax.broadcasted_iota(jnp.int32, (L, L), 1)
    dirs = (
        _MlstmDirection(False, kf_ref, qtf_ref, vtf_ref, gif_ref, gff_ref, gitf_ref, gftf_ref, hf_ref,
                        ct_scr, mrow_scr.at[0], mcol_scr.at[0], src <= tgt),
        _MlstmDirection(True, kb_ref, qtb_ref, vtb_ref, gib_ref, gfb_ref, gitb_ref, gftb_ref, hb_ref,
                        ct_scr, mrow_scr.at[1], mcol_scr.at[1], src >= tgt),
    )
    for d in dirs:
        d.state_free_matmuls()
    for d in dirs:
        d.gate_terms(brow_ref, bcol_ref)
    for d in dirs:
        d.state_update()
    for d in dirs:
        d.outputs()
    for w_ref, o_ref in zip(cast_in, cast_out):
        o_ref[...] = w_ref[...].astype(BF16)


def _mlstm(mk, mq_t, mv_t, gates, b_i, b_f, n_x, cast_weights):
    L = MLSTM_CHUNK
    rows = mk.shape[0]
    assert rows % L == 0 and n_x % L == 0 and rows - n_x == L
    nx = n_x // L
    steps = nx + 1
    g4 = gates.reshape(rows, N_DIR, 2, M_HEADS)
    g_in = g4[:, :, 0, :].reshape(rows, N_UNITS)
    g_fg = g4[:, :, 1, :].reshape(rows, N_UNITS)
    bias_row = jnp.stack([b_i.reshape(N_UNITS), b_f.reshape(N_UNITS)], axis=0)
    f_chunk = lambda g: g
    b_chunk = lambda g: jnp.where(g == 0, 0, nx + 1 - g)
    f_out = lambda g: (0, jnp.maximum(g - 1, 0))
    b_out = lambda g: (0, jnp.where(g == 0, nx - 1, nx - g))
    f_out_rows = lambda g: (jnp.maximum(g - 1, 0), 0)
    small = lambda g: (0, 0)

    def stream(chunk):
        return [
            pl.BlockSpec((L, MQK_WIDTH), lambda g: (chunk(g), 0)),
            pl.BlockSpec((MQK_WIDTH, L), lambda g: (0, chunk(g))),
            pl.BlockSpec((M_WIDTH, L), lambda g: (0, chunk(g))),
            pl.BlockSpec((L, N_UNITS), lambda g: (chunk(g), 0)),
            pl.BlockSpec((L, N_UNITS), lambda g: (chunk(g), 0)),
            pl.BlockSpec((N_UNITS, L), lambda g: (0, chunk(g))),
            pl.BlockSpec((N_UNITS, L), lambda g: (0, chunk(g))),
        ]

    operands = (mk, mq_t, mv_t, g_in, g_fg, g_in.T, g_fg.T)
    cast_specs, cast_shapes = [], []
    for w in cast_weights:
        assert w.shape[0] % (nx * BF16_SUBLANES) == 0, w.shape
        blk = (w.shape[0] // nx, w.shape[1])
        cast_specs.append(pl.BlockSpec(blk, f_out_rows))
        cast_shapes.append(jax.ShapeDtypeStruct(w.shape, BF16))
    outs = pl.pallas_call(
        functools.partial(_mlstm_kernel, n_cast=len(cast_weights)),
        grid=(steps,),
        in_specs=stream(f_chunk) + stream(b_chunk) + [pl.BlockSpec((2, N_UNITS), small),
                                                       pl.BlockSpec((N_UNITS, 2), small)] + cast_specs,
        out_specs=[pl.BlockSpec((M_WIDTH, L), f_out), pl.BlockSpec((M_WIDTH, L), b_out)] + cast_specs,
        out_shape=[jax.ShapeDtypeStruct((M_WIDTH, n_x), BF16)] * 2 + cast_shapes,
        scratch_shapes=[
            pltpu.VMEM((N_UNITS, M_V + BF16_SUBLANES, M_QK), F32),
            pltpu.VMEM((N_DIR, 8, LANES), F32),
            pltpu.VMEM((N_DIR, 8, LANES), F32),
        ],
        compiler_params=_cparams(("arbitrary",)),
        name="mlstm",
    )(*operands, *operands, bias_row, bias_row.T, *cast_weights)
    return outs[0], outs[1], tuple(outs[2:])


def _outproj_kernel(att_ref, hft_ref, hbt_ref, mo_ref, mg_ref, w_ref, x_ref, mod_ref, n2_ref,
                    x1_ref, h2_ref):
    d = x_ref.shape[1]
    tm = x_ref.shape[0]
    rows = tm // OUT_ROW_SPLIT

    def project(r0):
        y = lax.dot_general(att_ref[:, r0:r0 + rows], w_ref[0:ATT_WIDTH, :], (((0,), (0,)), ((), ())),
                            preferred_element_type=F32)
        for hd in range(M_HEADS):
            c = hd * M_V
            ht = (hft_ref[c:c + M_V, r0:r0 + rows].astype(F32)
                  + hbt_ref[c:c + M_V, r0:r0 + rows].astype(F32))
            ht = ht * lax.rsqrt(jnp.mean(ht * ht, axis=0, keepdims=True) + NORM_EPS) * mg_ref[c:c + M_V, :]
            r = (ht.T * mo_ref[r0:r0 + rows, c:c + M_V].astype(F32)).astype(BF16)
            y = y + jnp.dot(r, w_ref[ATT_WIDTH + c:ATT_WIDTH + c + M_V, :], preferred_element_type=F32)
        return y

    ys = [project(b * rows) for b in range(OUT_ROW_SPLIT)]
    for b, y in enumerate(ys):
        r0 = b * rows
        x1 = x_ref[r0:r0 + rows, :] + mod_ref[0:1, 0:d] * y
        x1_ref[r0:r0 + rows, :] = x1
        h2 = _rms(x1) * n2_ref[...] * (1.0 + mod_ref[0:1, 2 * d:3 * d]) + mod_ref[0:1, d:2 * d]
        h2_ref[r0:r0 + rows, :] = h2.astype(BF16)


def _out_proj(att, hf_t, hb_t, mo, m_gain, w_out, x, mod, norm2):
    t, d = x.shape
    tm = OUT_TM
    row = lambda i: (i, 0)
    full = lambda i: (0, 0)
    return pl.pallas_call(
        _outproj_kernel,
        grid=(t // tm,),
        in_specs=[
            pl.BlockSpec((ATT_WIDTH, tm), lambda i: (0, i)),
            pl.BlockSpec((M_WIDTH, tm), lambda i: (0, i)),
            pl.BlockSpec((M_WIDTH, tm), lambda i: (0, i)),
            pl.BlockSpec((tm, M_WIDTH), row),
            pl.BlockSpec((M_WIDTH, 1), full),
            pl.BlockSpec(w_out.shape, full),
            pl.BlockSpec((tm, d), row),
            pl.BlockSpec(mod.shape, full),
            pl.BlockSpec((1, d), full),
        ],
        out_specs=[pl.BlockSpec((tm, d), row), pl.BlockSpec((tm, d), row)],
        out_shape=[jax.ShapeDtypeStruct((t, d), F32), jax.ShapeDtypeStruct((t, d), BF16)],
        compiler_params=_cparams(("arbitrary",)),
        name="out_proj",
    )(att, hf_t, hb_t, mo, m_gain, w_out, x, mod, norm2)


def _ffn_kernel(h_ref, hp_ref, hn_ref, wg_ref, wv_ref, cwg_ref, cwv_ref, cbg_ref, cbv_ref, wd_ref,
                x1_ref, mod_ref, nf_ref, o_ref, hext_ref, ug_ref, uv_ref):
    i = pl.program_id(0)
    j = pl.program_id(1)
    tm = h_ref.shape[0]
    d = x1_ref.shape[1]

    @pl.when(j == 0)
    def _():
        hext_ref[0:HALO, :] = jnp.where(i == 0, jnp.zeros_like(hp_ref[...]), hp_ref[...])
        hext_ref[HALO:HALO + tm, :] = h_ref[...]
        hext_ref[HALO + tm:, :] = jnp.where(i == pl.num_programs(0) - 1, jnp.zeros_like(hn_ref[...]), hn_ref[...])

    split = tm // 2 + HALO
    cut = tm // 2 - HALO

    def conv(u_ref, cw_ref, cb_ref, r0, n):
        return (cw_ref[0:1, :] * u_ref[HALO - 1 + r0:HALO - 1 + r0 + n, :]
                + cw_ref[1:2, :] * u_ref[HALO + r0:HALO + r0 + n, :]
                + cw_ref[2:3, :] * u_ref[HALO + 1 + r0:HALO + 1 + r0 + n, :]
                + cb_ref[...])

    def hidden_block(first, final):
        for r0, r1 in ((0, split), (split, tm + 2 * HALO)):
            ug_ref[r0:r1, :] = jnp.dot(hext_ref[r0:r1, :], wg_ref[...], preferred_element_type=F32)
            uv_ref[r0:r1, :] = jnp.dot(hext_ref[r0:r1, :], wv_ref[...], preferred_element_type=F32)
        for r0, n in ((0, cut), (cut, tm - cut)):
            g = conv(ug_ref, cwg_ref, cbg_ref, r0, n)
            val = conv(uv_ref, cwv_ref, cbv_ref, r0, n)
            hg = 0.5 * g
            a = ((hg + hg * jnp.tanh(hg)) * val).astype(BF16)
            acc = jnp.dot(a, wd_ref[...], preferred_element_type=F32)
            if not first:
                acc = o_ref[r0:r0 + n, :] + acc
            if final:
                y = x1_ref[r0:r0 + n, :] + mod_ref[0:1, 3 * d:4 * d] * acc
                acc = _rms(y) * nf_ref[...]
            o_ref[r0:r0 + n, :] = acc

    last = pl.num_programs(1) - 1
    pl.when(j == 0)(functools.partial(hidden_block, True, False))
    pl.when(jnp.logical_and(j > 0, j < last))(functools.partial(hidden_block, False, False))
    pl.when(j == last)(functools.partial(hidden_block, False, True))


def _conv_ffn(h2, w_up, conv_w, conv_b, w_down, x1, mod, norm_f):
    t, d = h2.shape
    d_ff = w_down.shape[0]
    tm, tf = FFN_TM, FFN_TF
    nf = d_ff // tf
    assert nf >= 2
    hb = tm // HALO
    last_halo = t // HALO - 1
    in_specs = [
        pl.BlockSpec((tm, d), lambda i, j: (i, 0)),
        pl.BlockSpec((HALO, d), lambda i, j: (jnp.maximum(i * hb - 1, 0), 0)),
        pl.BlockSpec((HALO, d), lambda i, j: (jnp.minimum((i + 1) * hb, last_halo), 0)),
        pl.BlockSpec((d, tf), lambda i, j: (0, j)),
        pl.BlockSpec((d, tf), lambda i, j: (0, j + nf)),
        pl.BlockSpec((3, tf), lambda i, j: (0, j)),
        pl.BlockSpec((3, tf), lambda i, j: (0, j + nf)),
        pl.BlockSpec((1, tf), lambda i, j: (0, j)),
        pl.BlockSpec((1, tf), lambda i, j: (0, j + nf)),
        pl.BlockSpec((tf, d), lambda i, j: (j, 0)),
        pl.BlockSpec((tm, d), lambda i, j: (i, 0)),
        pl.BlockSpec(mod.shape, lambda i, j: (0, 0)),
        pl.BlockSpec((1, d), lambda i, j: (0, 0)),
    ]
    return pl.pallas_call(
        _ffn_kernel,
        grid=(t // tm, nf),
        in_specs=in_specs,
        out_specs=pl.BlockSpec((tm, d), lambda i, j: (i, 0)),
        out_shape=jax.ShapeDtypeStruct((t, d), F32),
        scratch_shapes=[
            pltpu.VMEM((tm + 2 * HALO, d), BF16),
            pltpu.VMEM((tm + 2 * HALO, tf), F32),
            pltpu.VMEM((tm + 2 * HALO, tf), F32),
        ],
        compiler_params=_cparams(("arbitrary", "arbitrary")),
        name="conv_ffn",
    )(h2, h2, h2, w_up, w_up, conv_w, conv_w, conv_b, conv_b, w_down, x1, mod, norm_f)


def _rope_tables(n_ctx, n_tok):
    f32 = np.float32
    rows = n_tok // GRID_W
    row = np.repeat(np.arange(rows, dtype=f32), GRID_W)
    col = np.tile(np.arange(GRID_W, dtype=f32), rows)
    inv_freq = np.power(f32(ROPE_THETA), -np.arange(0, AXIS_DIM, 2, dtype=f32) / f32(AXIS_DIM)).astype(f32)
    ang_r = row[:, None] * inv_freq[None, :]
    ang_c = col[:, None] * inv_freq[None, :]
    cos = np.concatenate([np.cos(ang_r), np.cos(ang_r), np.cos(ang_c), np.cos(ang_c)], axis=1)
    sin = np.concatenate([-np.sin(ang_r), np.sin(ang_r), -np.sin(ang_c), np.sin(ang_c)], axis=1)
    cos = np.concatenate([np.ones((n_ctx, HEAD_DIM), f32), cos], axis=0).astype(f32)
    sin = np.concatenate([np.zeros((n_ctx, HEAD_DIM), f32), sin], axis=0).astype(f32)
    return jnp.asarray(cos), jnp.asarray(sin)


def kernel(x, c, ctx, c_ctx, w_mod, b_mod, norm1, w_in, q_norm, k_norm, b_igate, b_fgate, m_norm,
           w_out, norm2, w_up, conv_w, conv_b, w_down, norm_f):
    batch, n_tok, d = x.shape
    assert batch == 1 and w_mod.shape[0] == 1
    n_ctx = ctx.shape[1]
    x2 = x[0]
    ctx2 = ctx[0]

    cs = jnp.stack([c[0], c_ctx], axis=1)
    mod = _modulation(cs, w_mod[0], b_mod[0], 2 * d)

    w_in_t = w_in[0].T
    w_main = _cast_main_columns(w_in_t, OFF_G)
    w_gate = _gate_weights(w_in_t, OFF_G)
    cos_t, sin_t = _rope_tables(n_ctx, n_tok)
    q_gain = (q_norm[0] * (HEAD_DIM ** -0.5 * LOG2E)).reshape(1, HEAD_DIM)
    k_gain = k_norm[0].reshape(1, HEAD_DIM)
    q_t, k, vt, mq_t, mk, mv_t, mo, gates = _in_proj(
        x2, ctx2, mod, norm1[0].reshape(1, d), w_main, w_gate, q_gain, k_gain, cos_t, sin_t)

    score_bound = HEAD_DIM * jnp.max(jnp.abs(q_gain)) * jnp.max(jnp.abs(k_gain)) * ATTN_BOUND_HEADROOM
    bounded = (score_bound <= ATTN_SCORE_BOUND).astype(jnp.int32).reshape(1)
    att, mod_rest, (w_out_b, w_up_b, w_down_b) = _attention(
        q_t, k, vt, bounded, (w_out[0], w_up[0], w_down[0]), cs, w_mod[0], b_mod[0], 2 * d)

    hf_t, hb_t, _ = _mlstm(mk, mq_t, mv_t, gates, b_igate[0], b_fgate[0], n_tok, ())

    x1, h2 = _out_proj(att, hf_t, hb_t, mo, m_norm[0].reshape(M_WIDTH, 1), w_out_b, x2, mod_rest,
                       norm2[0].reshape(1, d))

    out = _conv_ffn(h2, w_up_b, conv_w[0], conv_b[0].reshape(1, -1), w_down_b, x1, mod_rest, norm_f.reshape(1, d))
    return out[None]
```

```python
import functools

import jax
import jax.numpy as jnp
import numpy as np
from jax import lax
from jax.experimental import pallas as pl
from jax.experimental.pallas import tpu as pltpu

F32 = jnp.float32
BF16 = jnp.bfloat16

GRID_W = 64
HEAD_DIM = 128
N_Q_HEADS = 8
N_KV_HEADS = 2
Q_PER_KV = N_Q_HEADS // N_KV_HEADS
AXIS_DIM = HEAD_DIM // 2
ROPE_THETA = 10000.0
M_HEADS = 4
M_V = 256
M_QK = 128
N_DIR = 2
NORM_EPS = 1e-6
ATT_WIDTH = N_Q_HEADS * HEAD_DIM
KV_WIDTH = N_KV_HEADS * HEAD_DIM
M_WIDTH = M_HEADS * M_V
MQK_WIDTH = M_HEADS * M_QK
N_GATES = N_DIR * 2 * M_HEADS
N_UNITS = N_DIR * M_HEADS

OFF_AQ = 0
OFF_AK = OFF_AQ + ATT_WIDTH
OFF_AV = OFF_AK + KV_WIDTH
OFF_MQ = OFF_AV + KV_WIDTH
OFF_MK = OFF_MQ + MQK_WIDTH
OFF_MV = OFF_MK + MQK_WIDTH
OFF_MO = OFF_MV + M_WIDTH
OFF_G = OFF_MO + M_WIDTH

LANES = 128
MXU_WIDTH = 256
BF16_SUBLANES = 16
VMEM_LIMIT = 50 * 1024 * 1024

ROW_TILE = 256
MLSTM_CHUNK = 256
ATTN_TQ = 512
ATTN_SCORE_BOUND = 64.0
ATTN_BOUND_HEADROOM = 1.02
ATTN_UNIT_Q = 256
ATTN_TK_CAP = 768
ATTN_LOOKAHEAD = 2
OUT_TM = 512
OUT_ROW_SPLIT = 2
OUT_X_SLOTS = 3
FFN_TM = 512
FFN_TF = 512
HALO = BF16_SUBLANES
MOD_TK = 256
WCAST_TN = 512
LOG2E = 1.4426950408889634


def _largest_divisor(n, cap, mult):
    best = None
    for d in range(mult, cap + 1, mult):
        if n % d == 0:
            best = d
    assert best is not None, (n, cap, mult)
    return best


def _cparams(sem, vmem=VMEM_LIMIT):
    return pltpu.CompilerParams(dimension_semantics=sem, vmem_limit_bytes=vmem)


def _rms(x, eps=NORM_EPS):
    return x * lax.rsqrt(jnp.mean(x * x, axis=-1, keepdims=True) + eps)


def _mod_kernel(cs_ref, w_ref, b_ref, o_ref):
    @pl.when(pl.program_id(0) == 0)
    def _():
        o_ref[0:1, :] = b_ref[...]
        o_ref[1:2, :] = b_ref[...]

    cs = cs_ref[...]
    s = cs * jax.nn.sigmoid(cs)
    w = w_ref[...]
    o_ref[0:1, :] += jnp.sum(s[:, 0:1] * w, axis=0, keepdims=True)
    o_ref[1:2, :] += jnp.sum(s[:, 1:2] * w, axis=0, keepdims=True)


def _modulation(cs, w_mod, b_mod, n):
    d = w_mod.shape[0]
    tk = MOD_TK
    return pl.pallas_call(
        _mod_kernel,
        grid=(d // tk,),
        in_specs=[
            pl.BlockSpec((tk, 2), lambda j: (j, 0)),
            pl.BlockSpec((tk, n), lambda j: (j, 0)),
            pl.BlockSpec((1, n), lambda j: (0, 0)),
        ],
        out_specs=pl.BlockSpec((2, n), lambda j: (0, 0)),
        out_shape=jax.ShapeDtypeStruct((2, n), F32),
        compiler_params=_cparams(("arbitrary",)),
        name="modulation",
    )(cs, w_mod, b_mod.reshape(1, -1))


def _gate_weight_kernel(wt_ref, o_ref):
    g = wt_ref[...]
    hi = g.astype(BF16)
    lo = (g - hi.astype(F32)).astype(BF16)
    pad = jnp.zeros((o_ref.shape[0] - 2 * N_GATES, g.shape[1]), BF16)
    o_ref[...] = jnp.concatenate([hi, lo, pad], axis=0)


def _gate_weights(w_t, row0):
    n, d = w_t.shape
    assert row0 % N_GATES == 0 and row0 + N_GATES == n
    return pl.pallas_call(
        _gate_weight_kernel,
        grid=(1,),
        in_specs=[pl.BlockSpec((N_GATES, d), lambda i: (row0 // N_GATES, 0))],
        out_specs=pl.BlockSpec((LANES, d), lambda i: (0, 0)),
        out_shape=jax.ShapeDtypeStruct((LANES, d), BF16),
        compiler_params=_cparams(("arbitrary",)),
        name="w_gate_split",
    )(w_t)


def _wcast_kernel(wt_ref, o_ref):
    o_ref[...] = wt_ref[...].T.astype(BF16)


def _cast_main_columns(w_t, n_main):
    n, d = w_t.shape
    tn = WCAST_TN
    assert n_main % tn == 0 and n_main <= n
    return pl.pallas_call(
        _wcast_kernel,
        grid=(n_main // tn,),
        in_specs=[pl.BlockSpec((tn, d), lambda j: (j, 0))],
        out_specs=pl.BlockSpec((d, tn), lambda j: (0, j)),
        out_shape=jax.ShapeDtypeStruct((d, n_main), BF16),
        compiler_params=_cparams(("arbitrary",)),
        name="w_in_cast",
    )(w_t)


def _swap_rope_halves(y):
    lane = lax.broadcasted_iota(jnp.int32, y.shape, 1)
    fwd = pltpu.roll(y, LANES - AXIS_DIM // 2, axis=1)
    bwd = pltpu.roll(y, AXIS_DIM // 2, axis=1)
    return jnp.where((lane & (AXIS_DIM // 2)) == 0, fwd, bwd)


def _inproj_kernel(x_ref, ctx_ref, mod_ref, n1_ref, w_ref, wg_ref, qg_ref, kg_ref, cos_ref, sin_ref,
                   qt_ref, k_ref, vt_ref, mqt_ref, mk_ref, mvt_ref, mo_ref, g_ref):
    d = x_ref.shape[1]
    is_ctx = pl.program_id(0) == 0
    xin = jnp.where(is_ctx, ctx_ref[...], x_ref[...])
    shift = jnp.where(is_ctx, mod_ref[1:2, 0:d], mod_ref[0:1, 0:d])
    scale = jnp.where(is_ctx, mod_ref[1:2, d:2 * d], mod_ref[0:1, d:2 * d])
    hf = _rms(xin) * n1_ref[...] * (1.0 + scale) + shift
    h = hf.astype(BF16)

    def proj(c0, width):
        return jnp.dot(h, w_ref[:, c0:c0 + width], preferred_element_type=F32)

    cos = cos_ref[...]
    sin = sin_ref[...]

    def norm_rope(y, gain):
        yn = _rms(y) * gain
        return yn * cos + _swap_rope_halves(yn) * sin

    for j in range(ATT_WIDTH // MXU_WIDTH):
        y = proj(OFF_AQ + j * MXU_WIDTH, MXU_WIDTH)
        for t in range(MXU_WIDTH // HEAD_DIM):
            c = j * MXU_WIDTH + t * HEAD_DIM
            qt_ref[c:c + HEAD_DIM, :] = norm_rope(y[:, t * HEAD_DIM:(t + 1) * HEAD_DIM], qg_ref[...]).T.astype(BF16)
    y = proj(OFF_AK, KV_WIDTH)
    for t in range(N_KV_HEADS):
        k_ref[:, t * HEAD_DIM:(t + 1) * HEAD_DIM] = norm_rope(
            y[:, t * HEAD_DIM:(t + 1) * HEAD_DIM], kg_ref[...]).astype(BF16)
    vt_ref[...] = proj(OFF_AV, KV_WIDTH).T.astype(BF16)
    for j in range(MQK_WIDTH // MXU_WIDTH):
        c = j * MXU_WIDTH
        mqt_ref[c:c + MXU_WIDTH, :] = proj(OFF_MQ + c, MXU_WIDTH).T.astype(BF16)
        mk_ref[:, c:c + MXU_WIDTH] = (proj(OFF_MK + c, MXU_WIDTH) * (M_QK ** -0.5)).astype(BF16)
    for j in range(M_WIDTH // MXU_WIDTH):
        c = j * MXU_WIDTH
        mvt_ref[c:c + MXU_WIDTH, :] = proj(OFF_MV + c, MXU_WIDTH).T.astype(BF16)
        mo_ref[:, c:c + MXU_WIDTH] = jax.nn.sigmoid(proj(OFF_MO + c, MXU_WIDTH)).astype(BF16)
    nt = (((1,), (1,)), ((), ()))
    gg = lax.dot_general(h, wg_ref[...], nt, preferred_element_type=F32)
    gg = gg + pltpu.roll(gg, LANES - N_GATES, axis=1)
    g_ref[...] = gg[:, 0:N_GATES]


def _in_proj(x, ctx, mod, norm1, w_main, w_gate, q_gain, k_gain, cos_t, sin_t):
    t, d = x.shape
    n_ctx = ctx.shape[0]
    assert n_ctx == ROW_TILE and t % ROW_TILE == 0
    rows = n_ctx + t
    steps = rows // ROW_TILE
    full = lambda i: (0, 0)
    row_all = lambda i: (i, 0)
    row_x = lambda i: (jnp.maximum(i - 1, 0), 0)
    out_shape = [
        jax.ShapeDtypeStruct((ATT_WIDTH, t), BF16),
        jax.ShapeDtypeStruct((rows, KV_WIDTH), BF16),
        jax.ShapeDtypeStruct((KV_WIDTH, rows), BF16),
        jax.ShapeDtypeStruct((MQK_WIDTH, rows), BF16),
        jax.ShapeDtypeStruct((rows, MQK_WIDTH), BF16),
        jax.ShapeDtypeStruct((M_WIDTH, rows), BF16),
        jax.ShapeDtypeStruct((t, M_WIDTH), BF16),
        jax.ShapeDtypeStruct((rows, N_GATES), F32),
    ]
    out_specs = [
        pl.BlockSpec((ATT_WIDTH, ROW_TILE), lambda i: (0, jnp.maximum(i - 1, 0))),
        pl.BlockSpec((ROW_TILE, KV_WIDTH), row_all),
        pl.BlockSpec((KV_WIDTH, ROW_TILE), lambda i: (0, i)),
        pl.BlockSpec((MQK_WIDTH, ROW_TILE), lambda i: (0, i)),
        pl.BlockSpec((ROW_TILE, MQK_WIDTH), row_all),
        pl.BlockSpec((M_WIDTH, ROW_TILE), lambda i: (0, i)),
        pl.BlockSpec((ROW_TILE, M_WIDTH), row_x),
        pl.BlockSpec((ROW_TILE, N_GATES), row_all),
    ]
    in_specs = [
        pl.BlockSpec((ROW_TILE, d), row_x),
        pl.BlockSpec((ROW_TILE, d), full),
        pl.BlockSpec(mod.shape, full),
        pl.BlockSpec((1, d), full),
        pl.BlockSpec(w_main.shape, full),
        pl.BlockSpec(w_gate.shape, full),
        pl.BlockSpec((1, HEAD_DIM), full),
        pl.BlockSpec((1, HEAD_DIM), full),
        pl.BlockSpec((ROW_TILE, HEAD_DIM), row_all),
        pl.BlockSpec((ROW_TILE, HEAD_DIM), row_all),
    ]
    return pl.pallas_call(
        _inproj_kernel,
        grid=(steps,),
        in_specs=in_specs,
        out_specs=out_specs,
        out_shape=out_shape,
        compiler_params=_cparams(("arbitrary",)),
        name="in_proj",
    )(x, ctx, mod, norm1, w_main, w_gate, q_gain, k_gain, cos_t, sin_t)


def _attn_kernel(bounded_ref, qt_ref, k_ref, vt_ref, cs_ref, wm_a_ref, wm_b_ref, bm_ref, *rest, tk, n_cast):
    cast_in, (o_ref, modb_ref), rest = rest[:n_cast], rest[n_cast:n_cast + 2], rest[n_cast + 2:]
    cast_out, (acc_scr,) = rest[:n_cast], rest[n_cast:]
    uq = ATTN_UNIT_Q

    @pl.when(jnp.logical_and(pl.program_id(0) == 0, pl.program_id(1) == 0))
    def _():
        modb_ref[0:1, :] = bm_ref[...]
        modb_ref[1:2, :] = bm_ref[...]

    n_sub = qt_ref.shape[1] // uq
    n_chunks = k_ref.shape[0] // tk
    streams = [(r, h) for r in range(n_sub) for h in range(Q_PER_KV)]
    zero_row = jnp.zeros((1, uq), F32)

    def q_t(g):
        r, h = streams[g]
        return qt_ref[h * HEAD_DIM:(h + 1) * HEAD_DIM, r * uq:(r + 1) * uq]

    def finish(l):
        for g, (r, h) in enumerate(streams):
            o_ref[h * HEAD_DIM:(h + 1) * HEAD_DIM, r * uq:(r + 1) * uq] = (acc_scr[g] / l[g]).astype(o_ref.dtype)
        for w_ref, wo_ref in zip(cast_in, cast_out):
            wo_ref[...] = w_ref[...].astype(BF16)
        cs = cs_ref[...]
        s = cs * jax.nn.sigmoid(cs)
        half = wm_a_ref.shape[1]
        for n, wm_ref in enumerate((wm_a_ref, wm_b_ref)):
            wm = wm_ref[...]
            modb_ref[0:1, n * half:(n + 1) * half] += jnp.sum(s[:, 0:1] * wm, axis=0, keepdims=True)
            modb_ref[1:2, n * half:(n + 1) * half] += jnp.sum(s[:, 1:2] * wm, axis=0, keepdims=True)

    @pl.when(bounded_ref[0] == 1)
    def _():
        units = [(c, g) for c in range(n_chunks) for g in range(len(streams))]

        def scores(c, g):
            return jnp.dot(k_ref[c * tk:(c + 1) * tk, :], q_t(g), preferred_element_type=F32)

        l = [zero_row for _ in streams]
        pending = [scores(*u) for u in units[:ATTN_LOOKAHEAD]]
        for n, (c, g) in enumerate(units):
            st = pending.pop(0)
            if n + ATTN_LOOKAHEAD < len(units):
                pending.append(scores(*units[n + ATTN_LOOKAHEAD]))
            p = jnp.exp2(st)
            l[g] = l[g] + jnp.sum(p, axis=0, keepdims=True)
            pv = jnp.dot(vt_ref[:, c * tk:(c + 1) * tk], p.astype(BF16), preferred_element_type=F32)
            if c == 0:
                acc_scr[g] = pv
            else:
                acc_scr[g] += pv
        finish(l)

    @pl.when(bounded_ref[0] == 0)
    def _():
        acc_scr[...] = jnp.zeros(acc_scr.shape, F32)

        def chunk(c, carry):
            m, l = carry
            r0 = pl.multiple_of(c * tk, tk)
            kc = k_ref[pl.ds(r0, tk), :]
            vtc = vt_ref[:, pl.ds(r0, tk)]
            m_out, l_out = [], []
            for g in range(len(streams)):
                st = jnp.dot(kc, q_t(g), preferred_element_type=F32)
                m_new = jnp.maximum(m[g], jnp.max(st, axis=0, keepdims=True))
                p = jnp.exp2(st - m_new)
                alpha = jnp.exp2(m[g] - m_new)
                l_out.append(alpha * l[g] + jnp.sum(p, axis=0, keepdims=True))
                acc_scr[g] = alpha * acc_scr[g] + jnp.dot(vtc, p.astype(BF16), preferred_element_type=F32)
                m_out.append(m_new)
            return tuple(m_out), tuple(l_out)

        init = (tuple(jnp.full((1, uq), -jnp.inf, F32) for _ in streams), tuple(zero_row for _ in streams))
        _, l = lax.fori_loop(0, n_chunks, chunk, init)
        finish(l)


def _attention(q_t, k, vt, bounded, cast_weights, cs, w_mod, b_mod, n_done):
    t = q_t.shape[1]
    s_len = k.shape[0]
    tq = ATTN_TQ
    tk = _largest_divisor(s_len, ATTN_TK_CAP, MXU_WIDTH)
    group_w = Q_PER_KV * HEAD_DIM
    n_i = t // tq
    steps = N_KV_HEADS * n_i
    step = lambda h, i: h * n_i + i
    n_rest = w_mod.shape[1] - n_done
    half = n_rest // 2
    assert n_done % half == 0 and w_mod.shape[0] % steps == 0
    mod_rows = w_mod.shape[0] // steps
    mod_specs = [
        pl.BlockSpec((mod_rows, 2), lambda h, i: (step(h, i), 0)),
        pl.BlockSpec((mod_rows, half), lambda h, i: (step(h, i), n_done // half)),
        pl.BlockSpec((mod_rows, half), lambda h, i: (step(h, i), n_done // half + 1)),
        pl.BlockSpec((1, n_rest), lambda h, i: (0, 0)),
    ]
    cast_specs, cast_shapes = [], []
    for w in cast_weights:
        assert w.shape[0] % (steps * BF16_SUBLANES) == 0, w.shape
        cast_specs.append(pl.BlockSpec((w.shape[0] // steps, w.shape[1]), lambda h, i: (step(h, i), 0)))
        cast_shapes.append(jax.ShapeDtypeStruct(w.shape, BF16))
    outs = pl.pallas_call(
        functools.partial(_attn_kernel, tk=tk, n_cast=len(cast_weights)),
        grid=(N_KV_HEADS, n_i),
        in_specs=[
            pl.BlockSpec(memory_space=pltpu.SMEM),
            pl.BlockSpec((group_w, tq), lambda h, i: (h, i)),
            pl.BlockSpec((s_len, HEAD_DIM), lambda h, i: (0, h), pipeline_mode=pl.Buffered(1)),
            pl.BlockSpec((HEAD_DIM, s_len), lambda h, i: (h, 0), pipeline_mode=pl.Buffered(1)),
        ] + mod_specs + cast_specs,
        out_specs=[pl.BlockSpec((group_w, tq), lambda h, i: (h, i)),
                   pl.BlockSpec((2, n_rest), lambda h, i: (0, 0))] + cast_specs,
        out_shape=[jax.ShapeDtypeStruct((ATT_WIDTH, t), BF16),
                   jax.ShapeDtypeStruct((2, n_rest), F32)] + cast_shapes,
        scratch_shapes=[
            pltpu.VMEM((Q_PER_KV * tq // ATTN_UNIT_Q, HEAD_DIM, ATTN_UNIT_Q), F32),
        ],
        compiler_params=_cparams(("arbitrary", "arbitrary")),
        name="attention",
    )(bounded, q_t, k, vt, cs, w_mod, w_mod, b_mod[n_done:].reshape(1, n_rest), *cast_weights)
    return outs[0], outs[1], tuple(outs[2:])


def _log_sigmoid(x):
    return jnp.minimum(x, 0.0) - jnp.log1p(jnp.exp(-jnp.abs(x)))


def _scan(x, axis, op, ident, reverse):
    n = x.shape[axis]
    idx = lax.broadcasted_iota(jnp.int32, x.shape, axis)
    shift = 1
    while shift < n:
        if reverse:
            moved = pltpu.roll(x, n - shift, axis=axis)
            ok = idx < n - shift
        else:
            moved = pltpu.roll(x, shift, axis=axis)
            ok = idx >= shift
        x = op(x, jnp.where(ok, moved, ident))
        shift *= 2
    return x


class _MlstmDirection:
    def __init__(self, reverse, k_ref, qt_ref, vt_ref, gi_ref, gf_ref, git_ref, gft_ref, h_ref,
                 ct_scr, mrow_scr, mcol_scr, visible):
        self.reverse, self.visible = reverse, visible
        self.k_ref, self.qt_ref, self.vt_ref, self.h_ref = k_ref, qt_ref, vt_ref, h_ref
        self.gi_ref, self.gf_ref, self.git_ref, self.gft_ref = gi_ref, gf_ref, git_ref, gft_ref
        self.ct_scr, self.mrow_scr, self.mcol_scr = ct_scr, mrow_scr, mcol_scr
        self.units = [((M_HEADS if reverse else 0) + hd, hd) for hd in range(M_HEADS)]

    def _k(self, hd):
        return self.k_ref[:, hd * M_QK:(hd + 1) * M_QK]

    def _vt(self, hd):
        return self.vt_ref[hd * M_V:(hd + 1) * M_V, :]

    def state_free_matmuls(self):
        self.early = []
        for u, hd in self.units:
            qt = self.qt_ref[hd * M_QK:(hd + 1) * M_QK, :]
            state = self.ct_scr[u]
            qk_t = jnp.dot(self._k(hd), qt, preferred_element_type=F32)
            inter_t = jnp.dot(state.astype(BF16), qt, preferred_element_type=F32)
            self.early.append((state, qk_t, inter_t))

    def gate_terms(self, brow_ref, bcol_ref):
        add = lambda a, b: a + b
        reverse = self.reverse
        i_c = self.gi_ref[...] + brow_ref[0:1, :]
        ls_c = _log_sigmoid(self.gf_ref[...] + brow_ref[1:2, :])
        self.a_c = i_c - _scan(ls_c, 0, add, 0.0, reverse)
        m_prev_r = self.mrow_scr[0:1, 0:N_UNITS]
        mm_r = jnp.maximum(m_prev_r, jnp.max(self.a_c, axis=0, keepdims=True))
        self.kscale_c = jnp.exp(self.a_c - mm_r)
        self.w_prev_r = jnp.exp(m_prev_r - mm_r)
        self.mrow_scr[0:1, 0:N_UNITS] = jnp.sum(ls_c, axis=0, keepdims=True) + mm_r
        i_r = self.git_ref[...] + bcol_ref[:, 0:1]
        ls_r = _log_sigmoid(self.gft_ref[...] + bcol_ref[:, 1:2])
        cum_r = _scan(ls_r, 1, add, 0.0, reverse)
        a_r = i_r - cum_r
        m_prev_c = self.mcol_scr[0:N_UNITS, 0:1]
        m_t = cum_r + jnp.maximum(_scan(a_r, 1, jnp.maximum, -jnp.inf, reverse), m_prev_c)
        self.r_r = cum_r - m_t
        self.w_inter_r = jnp.exp(cum_r + m_prev_c - m_t)
        self.floor_r = jnp.exp(-m_t)
        self.mcol_scr[0:N_UNITS, 0:1] = (jnp.sum(ls_r, axis=1, keepdims=True)
                                         + jnp.maximum(m_prev_c, jnp.max(a_r, axis=1, keepdims=True)))

    def state_update(self):
        for (u, hd), (state, _, _) in zip(self.units, self.early):
            kw = self._k(hd).astype(F32) * self.kscale_c[:, u:u + 1]
            w_prev = self.w_prev_r[0:1, u:u + 1]
            self.ct_scr[u, 0:M_V, :] = (w_prev * state[0:M_V]
                                        + jnp.dot(self._vt(hd), kw.astype(BF16), preferred_element_type=F32))
            self.ct_scr[u, M_V:M_V + 1, :] = w_prev * state[M_V:M_V + 1] + jnp.sum(kw, axis=0, keepdims=True)

    def outputs(self):
        for (u, hd), (_, qk_t, inter_t) in zip(self.units, self.early):
            decay_t = jnp.exp(jnp.where(self.visible, self.a_c[:, u:u + 1], -jnp.inf) + self.r_r[u:u + 1, :])
            s_t = qk_t * decay_t
            w_inter = self.w_inter_r[u:u + 1, :]
            num_t = (inter_t[0:M_V] * w_inter
                     + jnp.dot(self._vt(hd), s_t.astype(BF16), preferred_element_type=F32))
            den = inter_t[M_V:M_V + 1] * w_inter + jnp.sum(s_t, axis=0, keepdims=True)
            self.h_ref[hd * M_V:(hd + 1) * M_V, :] = (num_t * (
                1.0 / jnp.maximum(jnp.abs(den), self.floor_r[u:u + 1, :]))).astype(self.h_ref.dtype)


def _mlstm_kernel(kf_ref, qtf_ref, vtf_ref, gif_ref, gff_ref, gitf_ref, gftf_ref,
                  kb_ref, qtb_ref, vtb_ref, gib_ref, gfb_ref, gitb_ref, gftb_ref,
                  brow_ref, bcol_ref, *rest, n_cast):
    cast_in, (hf_ref, hb_ref), rest = rest[:n_cast], rest[n_cast:n_cast + 2], rest[n_cast + 2:]
    cast_out, (ct_scr, mrow_scr, mcol_scr) = rest[:n_cast], rest[n_cast:]
    L = kf_ref.shape[0]

    @pl.when(pl.program_id(0) == 0)
    def _():
        ct_scr[...] = jnp.zeros(ct_scr.shape, F32)
        mrow_scr[...] = jnp.zeros(mrow_scr.shape, F32)
        mcol_scr[...] = jnp.zeros(mcol_scr.shape, F32)

    src = lax.broadcasted_iota(jnp.int32, (L, L), 0)
    tgt = lax.broadcasted_iota(jnp.int32, (L, L), 1)
    dirs = (
        _MlstmDirection(False, kf_ref, qtf_ref, vtf_ref, gif_ref, gff_ref, gitf_ref, gftf_ref, hf_ref,
                        ct_scr, mrow_scr.at[0], mcol_scr.at[0], src <= tgt),
        _MlstmDirection(True, kb_ref, qtb_ref, vtb_ref, gib_ref, gfb_ref, gitb_ref, gftb_ref, hb_ref,
                        ct_scr, mrow_scr.at[1], mcol_scr.at[1], src >= tgt),
    )
    for d in dirs:
        d.state_free_matmuls()
    for d in dirs:
        d.gate_terms(brow_ref, bcol_ref)
    for d in dirs:
        d.state_update()
    for d in dirs:
        d.outputs()
    for w_ref, o_ref in zip(cast_in, cast_out):
        o_ref[...] = w_ref[...].astype(BF16)


def _mlstm(mk, mq_t, mv_t, gates, b_i, b_f, n_x, cast_weights):
    L = MLSTM_CHUNK
    rows = mk.shape[0]
    assert rows % L == 0 and n_x % L == 0 and rows - n_x == L
    nx = n_x // L
    steps = nx + 1
    g4 = gates.reshape(rows, N_DIR, 2, M_HEADS)
    g_in = g4[:, :, 0, :].reshape(rows, N_UNITS)
    g_fg = g4[:, :, 1, :].reshape(rows, N_UNITS)
    bias_row = jnp.stack([b_i.reshape(N_UNITS), b_f.reshape(N_UNITS)], axis=0)
    f_chunk = lambda g: g
    b_chunk = lambda g: jnp.where(g == 0, 0, nx + 1 - g)
    f_out = lambda g: (0, jnp.maximum(g - 1, 0))
    b_out = lambda g: (0, jnp.where(g == 0, nx - 1, nx - g))
    f_out_rows = lambda g: (jnp.maximum(g - 1, 0), 0)
    small = lambda g: (0, 0)

    def stream(chunk):
        return [
            pl.BlockSpec((L, MQK_WIDTH), lambda g: (chunk(g), 0)),
            pl.BlockSpec((MQK_WIDTH, L), lambda g: (0, chunk(g))),
            pl.BlockSpec((M_WIDTH, L), lambda g: (0, chunk(g))),
            pl.BlockSpec((L, N_UNITS), lambda g: (chunk(g), 0)),
            pl.BlockSpec((L, N_UNITS), lambda g: (chunk(g), 0)),
            pl.BlockSpec((N_UNITS, L), lambda g: (0, chunk(g))),
            pl.BlockSpec((N_UNITS, L), lambda g: (0, chunk(g))),
        ]

    operands = (mk, mq_t, mv_t, g_in, g_fg, g_in.T, g_fg.T)
    cast_specs, cast_shapes = [], []
    for w in cast_weights:
        assert w.shape[0] % (nx * BF16_SUBLANES) == 0, w.shape
        blk = (w.shape[0] // nx, w.shape[1])
        cast_specs.append(pl.BlockSpec(blk, f_out_rows))
        cast_shapes.append(jax.ShapeDtypeStruct(w.shape, BF16))
    outs = pl.pallas_call(
        functools.partial(_mlstm_kernel, n_cast=len(cast_weights)),
        grid=(steps,),
        in_specs=stream(f_chunk) + stream(b_chunk) + [pl.BlockSpec((2, N_UNITS), small),
                                                       pl.BlockSpec((N_UNITS, 2), small)] + cast_specs,
        out_specs=[pl.BlockSpec((M_WIDTH, L), f_out), pl.BlockSpec((M_WIDTH, L), b_out)] + cast_specs,
        out_shape=[jax.ShapeDtypeStruct((M_WIDTH, n_x), BF16)] * 2 + cast_shapes,
        scratch_shapes=[
            pltpu.VMEM((N_UNITS, M_V + BF16_SUBLANES, M_QK), F32),
            pltpu.VMEM((N_DIR, 8, LANES), F32),
            pltpu.VMEM((N_DIR, 8, LANES), F32),
        ],
        compiler_params=_cparams(("arbitrary",)),
        name="mlstm",
    )(*operands, *operands, bias_row, bias_row.T, *cast_weights)
    return outs[0], outs[1], tuple(outs[2:])


def _outproj_kernel(att_ref, hft_ref, hbt_ref, mo_ref, mg_ref, w_ref, x_hbm, mod_ref, n2_ref,
                    x1_ref, h2_ref, xbuf_ref, xsem):
    _, tm, d = xbuf_ref.shape
    rows = tm // OUT_ROW_SPLIT
    step = pl.program_id(0)
    ahead = OUT_X_SLOTS - 1

    def x_copy(s):
        slot = s % OUT_X_SLOTS
        return pltpu.make_async_copy(x_hbm.at[pl.ds(s * tm, tm), :], xbuf_ref.at[slot], xsem.at[slot])

    @pl.when(step == 0)
    def _():
        for s in range(ahead):
            x_copy(s).start()

    @pl.when(step + ahead < pl.num_programs(0))
    def _():
        x_copy(step + ahead).start()

    def project(r0):
        y = lax.dot_general(att_ref[:, r0:r0 + rows], w_ref[0:ATT_WIDTH, :], (((0,), (0,)), ((), ())),
                            preferred_element_type=F32)
        for hd in range(M_HEADS):
            c = hd * M_V
            ht = (hft_ref[c:c + M_V, r0:r0 + rows].astype(F32)
                  + hbt_ref[c:c + M_V, r0:r0 + rows].astype(F32))
            ht = ht * lax.rsqrt(jnp.mean(ht * ht, axis=0, keepdims=True) + NORM_EPS) * mg_ref[c:c + M_V, :]
            r = (ht.T * mo_ref[r0:r0 + rows, c:c + M_V].astype(F32)).astype(BF16)
            y = y + jnp.dot(r, w_ref[ATT_WIDTH + c:ATT_WIDTH + c + M_V, :], preferred_element_type=F32)
        return y

    ys = [project(b * rows) for b in range(OUT_ROW_SPLIT)]
    x_copy(step).wait()
    x_ref = xbuf_ref.at[step % OUT_X_SLOTS]
    for b, y in enumerate(ys):
        r0 = b * rows
        x1 = x_ref[r0:r0 + rows, :] + mod_ref[0:1, 0:d] * y
        x1_ref[r0:r0 + rows, :] = x1
        h2 = _rms(x1) * n2_ref[...] * (1.0 + mod_ref[0:1, 2 * d:3 * d]) + mod_ref[0:1, d:2 * d]
        h2_ref[r0:r0 + rows, :] = h2.astype(BF16)


def _out_proj(att, hf_t, hb_t, mo, m_gain, w_out, x, mod, norm2):
    t, d = x.shape
    tm = OUT_TM
    assert t // tm >= OUT_X_SLOTS - 1, "the residual ring's prologue starts OUT_X_SLOTS - 1 copies"
    row = lambda i: (i, 0)
    full = lambda i: (0, 0)
    return pl.pallas_call(
        _outproj_kernel,
        grid=(t // tm,),
        in_specs=[
            pl.BlockSpec((ATT_WIDTH, tm), lambda i: (0, i)),
            pl.BlockSpec((M_WIDTH, tm), lambda i: (0, i)),
            pl.BlockSpec((M_WIDTH, tm), lambda i: (0, i)),
            pl.BlockSpec((tm, M_WIDTH), row),
            pl.BlockSpec((M_WIDTH, 1), full),
            pl.BlockSpec(w_out.shape, full, pipeline_mode=pl.Buffered(1)),
            pl.BlockSpec(memory_space=pl.ANY),
            pl.BlockSpec(mod.shape, full),
            pl.BlockSpec((1, d), full),
        ],
        out_specs=[pl.BlockSpec((tm, d), row), pl.BlockSpec((tm, d), row)],
        out_shape=[jax.ShapeDtypeStruct((t, d), F32), jax.ShapeDtypeStruct((t, d), BF16)],
        scratch_shapes=[pltpu.VMEM((OUT_X_SLOTS, tm, d), F32), pltpu.SemaphoreType.DMA((OUT_X_SLOTS,))],
        compiler_params=_cparams(("arbitrary",)),
        name="out_proj",
    )(att, hf_t, hb_t, mo, m_gain, w_out, x, mod, norm2)


def _ffn_kernel(h_ref, hp_ref, hn_ref, wg_ref, wv_ref, cwg_ref, cwv_ref, cbg_ref, cbv_ref, wd_ref,
                x1_ref, mod_ref, nf_ref, o_ref, hext_ref, ug_ref, uv_ref):
    i = pl.program_id(0)
    j = pl.program_id(1)
    tm = h_ref.shape[0]
    d = x1_ref.shape[1]

    @pl.when(j == 0)
    def _():
        hext_ref[0:HALO, :] = jnp.where(i == 0, jnp.zeros_like(hp_ref[...]), hp_ref[...])
        hext_ref[HALO:HALO + tm, :] = h_ref[...]
        hext_ref[HALO + tm:, :] = jnp.where(i == pl.num_programs(0) - 1, jnp.zeros_like(hn_ref[...]), hn_ref[...])
        o_ref[...] = jnp.zeros(o_ref.shape, F32)

    split = tm // 2 + HALO
    cut = tm // 2 - HALO

    def conv(u_ref, cw_ref, cb_ref, r0, n):
        return (cw_ref[0:1, :] * u_ref[HALO - 1 + r0:HALO - 1 + r0 + n, :]
                + cw_ref[1:2, :] * u_ref[HALO + r0:HALO + r0 + n, :]
                + cw_ref[2:3, :] * u_ref[HALO + 1 + r0:HALO + 1 + r0 + n, :]
                + cb_ref[...])

    def hidden_block(final):
        for r0, r1 in ((0, split), (split, tm + 2 * HALO)):
            ug_ref[r0:r1, :] = jnp.dot(hext_ref[r0:r1, :], wg_ref[...], preferred_element_type=F32)
            uv_ref[r0:r1, :] = jnp.dot(hext_ref[r0:r1, :], wv_ref[...], preferred_element_type=F32)
        for r0, n in ((0, cut), (cut, tm - cut)):
            g = conv(ug_ref, cwg_ref, cbg_ref, r0, n)
            val = conv(uv_ref, cwv_ref, cbv_ref, r0, n)
            hg = 0.5 * g
            a = ((hg + hg * jnp.tanh(hg)) * val).astype(BF16)
            acc = o_ref[r0:r0 + n, :] + jnp.dot(a, wd_ref[...], preferred_element_type=F32)
            if final:
                y = x1_ref[r0:r0 + n, :] + mod_ref[0:1, 3 * d:4 * d] * acc
                acc = _rms(y) * nf_ref[...]
            o_ref[r0:r0 + n, :] = acc

    last = pl.num_programs(1) - 1
    pl.when(j != last)(functools.partial(hidden_block, False))
    pl.when(j == last)(functools.partial(hidden_block, True))


def _conv_ffn(h2, w_up, conv_w, conv_b, w_down, x1, mod, norm_f):
    t, d = h2.shape
    d_ff = w_down.shape[0]
    tm, tf = FFN_TM, FFN_TF
    nf = d_ff // tf
    hb = tm // HALO
    last_halo = t // HALO - 1
    in_specs = [
        pl.BlockSpec((tm, d), lambda i, j: (i, 0)),
        pl.BlockSpec((HALO, d), lambda i, j: (jnp.maximum(i * hb - 1, 0), 0)),
        pl.BlockSpec((HALO, d), lambda i, j: (jnp.minimum((i + 1) * hb, last_halo), 0)),
        pl.BlockSpec((d, tf), lambda i, j: (0, j)),
        pl.BlockSpec((d, tf), lambda i, j: (0, j + nf)),
        pl.BlockSpec((3, tf), lambda i, j: (0, j)),
        pl.BlockSpec((3, tf), lambda i, j: (0, j + nf)),
        pl.BlockSpec((1, tf), lambda i, j: (0, j)),
        pl.BlockSpec((1, tf), lambda i, j: (0, j + nf)),
        pl.BlockSpec((tf, d), lambda i, j: (j, 0)),
        pl.BlockSpec((tm, d), lambda i, j: (i, 0)),
        pl.BlockSpec(mod.shape, lambda i, j: (0, 0)),
        pl.BlockSpec((1, d), lambda i, j: (0, 0)),
    ]
    return pl.pallas_call(
        _ffn_kernel,
        grid=(t // tm, nf),
        in_specs=in_specs,
        out_specs=pl.BlockSpec((tm, d), lambda i, j: (i, 0)),
        out_shape=jax.ShapeDtypeStruct((t, d), F32),
        scratch_shapes=[
            pltpu.VMEM((tm + 2 * HALO, d), BF16),
            pltpu.VMEM((tm + 2 * HALO, tf), F32),
            pltpu.VMEM((tm + 2 * HALO, tf), F32),
        ],
        compiler_params=_cparams(("arbitrary", "arbitrary")),
        name="conv_ffn",
    )(h2, h2, h2, w_up, w_up, conv_w, conv_w, conv_b, conv_b, w_down, x1, mod, norm_f)


def _rope_tables(n_ctx, n_tok):
    f32 = np.float32
    rows = n_tok // GRID_W
    row = np.repeat(np.arange(rows, dtype=f32), GRID_W)
    col = np.tile(np.arange(GRID_W, dtype=f32), rows)
    inv_freq = np.power(f32(ROPE_THETA), -np.arange(0, AXIS_DIM, 2, dtype=f32) / f32(AXIS_DIM)).astype(f32)
    ang_r = row[:, None] * inv_freq[None, :]
    ang_c = col[:, None] * inv_freq[None, :]
    cos = np.concatenate([np.cos(ang_r), np.cos(ang_r), np.cos(ang_c), np.cos(ang_c)], axis=1)
    sin = np.concatenate([-np.sin(ang_r), np.sin(ang_r), -np.sin(ang_c), np.sin(ang_c)], axis=1)
    cos = np.concatenate([np.ones((n_ctx, HEAD_DIM), f32), cos], axis=0).astype(f32)
    sin = np.concatenate([np.zeros((n_ctx, HEAD_DIM), f32), sin], axis=0).astype(f32)
    return jnp.asarray(cos), jnp.asarray(sin)


def kernel(x, c, ctx, c_ctx, w_mod, b_mod, norm1, w_in, q_norm, k_norm, b_igate, b_fgate, m_norm,
           w_out, norm2, w_up, conv_w, conv_b, w_down, norm_f):
    batch, n_tok, d = x.shape
    assert batch == 1 and w_mod.shape[0] == 1
    n_ctx = ctx.shape[1]
    x2 = x[0]
    ctx2 = ctx[0]

    cs = jnp.stack([c[0], c_ctx], axis=1)
    mod = _modulation(cs, w_mod[0], b_mod[0], 2 * d)

    w_in_t = w_in[0].T
    w_main = _cast_main_columns(w_in_t, OFF_G)
    w_gate = _gate_weights(w_in_t, OFF_G)
    cos_t, sin_t = _rope_tables(n_ctx, n_tok)
    q_gain = (q_norm[0] * (HEAD_DIM ** -0.5 * LOG2E)).reshape(1, HEAD_DIM)
    k_gain = k_norm[0].reshape(1, HEAD_DIM)
    q_t, k, vt, mq_t, mk, mv_t, mo, gates = _in_proj(
        x2, ctx2, mod, norm1[0].reshape(1, d), w_main, w_gate, q_gain, k_gain, cos_t, sin_t)

    score_bound = HEAD_DIM * jnp.max(jnp.abs(q_gain)) * jnp.max(jnp.abs(k_gain)) * ATTN_BOUND_HEADROOM
    bounded = (score_bound <= ATTN_SCORE_BOUND).astype(jnp.int32).reshape(1)
    att, mod_rest, (w_out_b, w_up_b, w_down_b) = _attention(
        q_t, k, vt, bounded, (w_out[0], w_up[0], w_down[0]), cs, w_mod[0], b_mod[0], 2 * d)

    hf_t, hb_t, _ = _mlstm(mk, mq_t, mv_t, gates, b_igate[0], b_fgate[0], n_tok, ())

    x1, h2 = _out_proj(att, hf_t, hb_t, mo, m_norm[0].reshape(M_WIDTH, 1), w_out_b, x2, mod_rest,
                       norm2[0].reshape(1, d))

    out = _conv_ffn(h2, w_up_b, conv_w[0], conv_b[0].reshape(1, -1), w_down_b, x1, mod_rest, norm_f.reshape(1, d))
    return out[None]
```
